```python
import math
import jax, jax.numpy as jnp
from jax import lax
import numpy as np

D_MODEL = 1024
BATCH = 4
SEQ = 4096
DEPTH = 4
DEC_BATCH = 32
DEC_SEQ = 1
PAST_LEN = 8192
PAGE_SIZE = 128

HEAD_DIM = 64
CONV_CH = 512
CONV_W = 3
H_B = 8
G_B = 2
R_B = H_B // G_B
L_CMP = 32
D_CMP = 16
L_SEL = 64
N_SEL = 16
WINDOW = 512
CMP_HID = 256
H_C = 8
D_FF = 2816
N_BUCKETS = 32
MAX_DIST = 128
N_SUB = 3
Q_BLOCK = 128
EPS = 1e-6
NEG = -1e30
FORCE_BONUS = 1e4

W_A = CONV_CH
W_B = H_B * HEAD_DIM
W_C = H_C * HEAD_DIM
W_MIX = W_A + W_B + W_C
KV_B = G_B * HEAD_DIM
SPLIT_SIZES = (W_A, W_A, W_A, W_B, KV_B, KV_B, KV_B, KV_B, KV_B, KV_B, 3 * H_B, W_C, W_C, W_C, 3 * D_MODEL)
P_TOTAL = sum(SPLIT_SIZES)
SPLIT_POINTS = tuple(int(s) for s in np.cumsum(SPLIT_SIZES)[:-1])

kernel_name = 'hybrid_conv_nsa_stickbreaking_decode_step'


def rmsnorm(x, g):
    xf = x.astype(jnp.float32)
    xf = xf * lax.rsqrt(jnp.mean(xf * xf, axis=-1, keepdims=True) + EPS)
    return xf.astype(x.dtype) * g


def masked_softmax(s, mask):
    s = jnp.where(mask, s.astype(jnp.float32), NEG)
    e = jnp.exp(s - jnp.max(s, axis=-1, keepdims=True)) * mask
    return e / jnp.maximum(jnp.sum(e, axis=-1, keepdims=True), 1e-30)


def block_size(tq):
    return Q_BLOCK if tq % Q_BLOCK == 0 else tq


def t5_bucket(dist):
    n = jnp.maximum(dist, 0)
    max_exact = N_BUCKETS // 2
    nf = jnp.maximum(n, 1).astype(jnp.float32)
    log_b = max_exact + (jnp.log(nf / max_exact) / math.log(MAX_DIST / max_exact)
                         * (N_BUCKETS - max_exact)).astype(jnp.int32)
    return jnp.where(n < max_exact, n, jnp.minimum(log_b, N_BUCKETS - 1))


def swiglu(x, wg, wu, wd):
    return (jax.nn.silu(x @ wg) * (x @ wu)) @ wd


def gather_pages(pool, page_table):
    pages = pool[page_table]
    return pages.reshape((page_table.shape[0], -1) + pool.shape[2:])


def short_conv(v, b, c, conv_w, ctx):
    u = c * v
    ux = jnp.concatenate([ctx, u], axis=1)
    T = u.shape[1]
    y = sum(conv_w[k] * ux[:, k:k + T] for k in range(CONV_W))
    return b * y, ux[:, T:]


def compress_blocks(rows, pos_emb, w1, w2):
    B, T, G, hd = rows.shape
    n_cmp = (T - L_CMP) // D_CMP + 1
    idx = jnp.arange(n_cmp)[:, None] * D_CMP + jnp.arange(L_CMP)[None, :]
    blocks = rows[:, idx] + pos_emb[:, None, :]
    flat = blocks.transpose(0, 1, 3, 2, 4).reshape(B, n_cmp, G, L_CMP * hd)
    return jax.nn.silu(flat @ w1) @ w2


def nsa_compressed_selected(q, k_cr, v_cr, k_sr, v_sr, pos0, cmp_pos, w_cmp1, w_cmp2, rel_bias):
    B, Tq, H, hd = q.shape
    T = k_cr.shape[1]
    scale = hd ** -0.5
    qg = q.reshape(B, Tq, G_B, R_B, hd)
    qpos = pos0 + jnp.arange(Tq)
    k_cmp = compress_blocks(k_cr, cmp_pos[0], w_cmp1[0], w_cmp2[0])
    v_cmp = compress_blocks(v_cr, cmp_pos[1], w_cmp1[1], w_cmp2[1])
    n_cmp = k_cmp.shape[1]
    c_start = jnp.arange(n_cmp) * D_CMP
    dist_c = qpos[:, None] - (c_start + L_CMP - 1)[None, :]
    bias_c = rel_bias[t5_bucket(dist_c)].reshape(Tq, n_cmp, G_B, R_B).transpose(2, 3, 0, 1)
    s_c = jnp.einsum('bqgrd,bcgd->bgrqc', qg, k_cmp).astype(jnp.float32) * scale + bias_c
    p_cmp = masked_softmax(s_c, dist_c >= 0)
    o_cmp = jnp.einsum('bgrqc,bcgd->bqgrd', p_cmp.astype(v_cmp.dtype), v_cmp).reshape(B, Tq, H, hd)
    n_slc = -(-T // L_SEL)
    s_start = jnp.arange(n_slc) * L_SEL
    overlap = ((c_start[:, None] < s_start[None, :] + L_SEL)
               & (c_start[:, None] + L_CMP > s_start[None, :])).astype(jnp.float32)
    imp = jnp.einsum('bgrqc,cj->bgqj', p_cmp, overlap)
    jj = jnp.arange(n_slc)[None, :]
    cur = (qpos // L_SEL)[:, None]
    valid = s_start[None, :] <= qpos[:, None]
    forced = (jj == 0) | (jj == cur) | (jj == cur - 1)
    score = jnp.where(valid, imp + FORCE_BONUS * forced, NEG)
    n_top = min(N_SEL, n_slc)
    _, top_idx = lax.top_k(score, n_top)
    pad = n_slc * L_SEL - T
    def to_blocks(r):
        r = jnp.pad(r, ((0, 0), (0, pad), (0, 0), (0, 0)))
        return r.reshape(B, n_slc, L_SEL, G_B, hd).transpose(0, 3, 1, 2, 4)
    k_blk = to_blocks(k_sr)
    v_blk = to_blocks(v_sr)
    qb = block_size(Tq)
    nb = Tq // qb
    q_blocks = qg.reshape(B, nb, qb, G_B, R_B, hd).transpose(1, 0, 3, 4, 2, 5)
    idx_blocks = top_idx.reshape(B, G_B, nb, qb, n_top).transpose(2, 0, 1, 3, 4)
    pos_blocks = qpos.reshape(nb, qb)
    rb = rel_bias.reshape(N_BUCKETS, G_B, R_B).transpose(1, 0, 2)
    gather = jax.vmap(jax.vmap(lambda blk, ix: blk[ix]))
    def sel_block(args):
        qblk, ix, qp = args
        kg = gather(k_blk, ix)
        vg = gather(v_blk, ix)
        kpos = ix[..., None] * L_SEL + jnp.arange(L_SEL)
        dist = qp[:, None, None] - kpos
        bias = rb[jnp.arange(G_B)[None, :, None, None, None], t5_bucket(dist)]
        s = (jnp.einsum('bgrqd,bgqnkd->bgrqnk', qblk, kg).astype(jnp.float32) * scale
             + bias.transpose(0, 1, 5, 2, 3, 4))
        p = masked_softmax(s.reshape(B, G_B, R_B, qb, n_top * L_SEL),
                           (dist >= 0).reshape(B, G_B, 1, qb, n_top * L_SEL))
        p = p.reshape(B, G_B, R_B, qb, n_top, L_SEL).astype(vg.dtype)
        return jnp.einsum('bgrqnk,bgqnkd->bgrqd', p, vg)
    o_sel = lax.map(sel_block, (q_blocks, idx_blocks, pos_blocks))
    o_sel = o_sel.transpose(1, 0, 4, 2, 3, 5).reshape(B, Tq, H, hd)
    return o_cmp, o_sel


def nsa_window(q, k_all, v_all, n_ctx, pos0, rel_bias):
    B, Tq, H, hd = q.shape
    qb = block_size(Tq)
    nb = Tq // qb
    idx = jnp.arange(nb)[:, None] * qb + jnp.arange(n_ctx + qb)[None, :]
    kb = k_all[:, idx]
    vb = v_all[:, idx]
    qg = q.reshape(B, nb, qb, G_B, R_B, hd)
    qpos = pos0 + jnp.arange(Tq).reshape(nb, qb)
    kpos = pos0 - n_ctx + idx
    dist = qpos[:, :, None] - kpos[:, None, :]
    mask = (dist >= 0) & (dist <= WINDOW) & (kpos[:, None, :] >= 0)
    bias = rel_bias[t5_bucket(dist)].reshape(nb, qb, n_ctx + qb, G_B, R_B).transpose(0, 3, 4, 1, 2)
    s = jnp.einsum('bnqgrd,bnkgd->bngrqk', qg, kb).astype(jnp.float32) * hd ** -0.5 + bias
    p = masked_softmax(s, mask[None, :, None, None]).astype(vb.dtype)
    o = jnp.einsum('bngrqk,bnkgd->bnqgrd', p, vb)
    return o.reshape(B, Tq, H, hd)


def stick_breaking(q, k, v, pos0):
    B, Tq, H, hd = q.shape
    T = k.shape[1]
    qb = block_size(Tq)
    nb = Tq // qb
    qs = q.reshape(B, nb, qb, H, hd).transpose(1, 0, 2, 3, 4)
    kpos = jnp.arange(T)
    scale = hd ** -0.5
    def sb_block(args):
        qblk, start = args
        z = jnp.einsum('bqhd,bkhd->bhqk', qblk, k).astype(jnp.float32) * scale
        qpos = pos0 + start + jnp.arange(qb)
        causal = kpos[None, :] < qpos[:, None]
        log_1mb = jnp.where(causal, jax.nn.log_sigmoid(-z), 0.0)
        after = lax.cumsum(log_1mb, axis=3, reverse=True) - log_1mb
        a = jnp.where(causal, jnp.exp(jax.nn.log_sigmoid(z) + after), 0.0)
        return jnp.einsum('bhqk,bkhd->bqhd', a.astype(v.dtype), v)
    o = lax.map(sb_block, (qs, jnp.arange(nb) * qb))
    return o.transpose(1, 0, 2, 3, 4).reshape(B, Tq, H, hd)


def token_mixing(h, lw, rel_bias, pos0, past):
    B, Tq, _ = h.shape
    proj = h @ lw['w_in']
    (a_v, a_b, a_c, b_q, b_kc, b_vc, b_ks, b_vs, b_kw, b_vw, b_g,
     c_q, c_k, c_v, g_merge) = jnp.split(proj, SPLIT_POINTS, axis=-1)
    heads = lambda t, n: t.reshape(B, Tq, n, HEAD_DIM)
    nsa_rows = jnp.stack([heads(b_kc, G_B), heads(b_vc, G_B), heads(b_ks, G_B), heads(b_vs, G_B)], axis=2)
    sb_rows = jnp.stack([heads(c_k, H_C), heads(c_v, H_C)], axis=2)
    win_rows = jnp.stack([heads(b_kw, G_B), heads(b_vw, G_B)], axis=2)
    if past is None:
        conv_ctx = jnp.zeros((B, CONV_W - 1, CONV_CH), h.dtype)
        nsa_all, sb_all = nsa_rows, sb_rows
        win_ctx = jnp.zeros((B, WINDOW, 2, G_B, HEAD_DIM), h.dtype)
        n_keep = min(WINDOW, Tq)
    else:
        sb_past, nsa_past, win_ctx, conv_ctx = past
        nsa_all = jnp.concatenate([nsa_past, nsa_rows], axis=1)
        sb_all = jnp.concatenate([sb_past, sb_rows], axis=1)
        n_keep = win_ctx.shape[1]
    y_a, conv_state = short_conv(a_v, a_b, a_c, lw['conv_w'], conv_ctx)
    q_b = heads(b_q, H_B)
    o_cmp, o_sel = nsa_compressed_selected(q_b, nsa_all[:, :, 0], nsa_all[:, :, 1], nsa_all[:, :, 2],
                                           nsa_all[:, :, 3], pos0, lw['cmp_pos'], lw['w_cmp1'],
                                           lw['w_cmp2'], rel_bias)
    win_all = jnp.concatenate([win_ctx, win_rows], axis=1)
    o_win = nsa_window(q_b, win_all[:, :, 0], win_all[:, :, 1], win_ctx.shape[1], pos0, rel_bias)
    gb = jax.nn.sigmoid(b_g).reshape(B, Tq, H_B, 3, 1)
    y_b = (gb[..., 0, :] * o_cmp + gb[..., 1, :] * o_sel + gb[..., 2, :] * o_win).reshape(B, Tq, W_B)
    y_c = stick_breaking(heads(c_q, H_C), sb_all[:, :, 0], sb_all[:, :, 1], pos0).reshape(B, Tq, W_C)
    gm = jax.nn.sigmoid(g_merge).reshape(B, Tq, 3, D_MODEL)
    wb = lw['w_branch']
    merged = (gm[:, :, 0] * (y_a @ wb[:W_A]) + gm[:, :, 1] * (y_b @ wb[W_A:W_A + W_B])
              + gm[:, :, 2] * (y_c @ wb[W_A + W_B:]))
    out = merged @ lw['w_out']
    return out, (sb_rows, nsa_rows, win_all[:, -n_keep:], conv_state)


def decoder_layer(x, c, pos0, past, lw, rel_bias):
    B = x.shape[0]
    ada = (jax.nn.silu(c) @ lw['w_ada'] + lw['b_ada']).reshape(B, N_SUB, 3, D_MODEL)
    def modulated(h, i):
        return rmsnorm(h, lw['norm_pre'][i]) * (1 + ada[:, i, 1, None]) + ada[:, i, 0, None]
    def post(h, i):
        return ada[:, i, 2, None] * rmsnorm(h, lw['norm_post'][i])
    x = x + 0.5 * post(swiglu(modulated(x, 0), lw['w_ffn_gate'][0], lw['w_ffn_up'][0], lw['w_ffn_down'][0]), 0)
    mix, state = token_mixing(modulated(x, 1), lw, rel_bias, pos0, past)
    x = x + post(mix, 1)
    x = x + 0.5 * post(swiglu(modulated(x, 2), lw['w_ffn_gate'][1], lw['w_ffn_up'][1], lw['w_ffn_down'][1]), 2)
    return x, state


def setup_inputs(seed: int = 0) -> dict:
    key = jax.random.key(seed)
    ks = jax.random.split(key, 24)
    f32 = jnp.float32
    nrm = lambda k, shape, s: s * jax.random.normal(k, shape, f32)
    n_pages = PAST_LEN // PAGE_SIZE
    n_pool = (DEC_BATCH * n_pages * 5) // 4
    w_buf = min(WINDOW, PAST_LEN)
    page_table = jax.random.permutation(ks[0], n_pool)[:DEC_BATCH * n_pages].reshape(DEC_BATCH, n_pages).astype(jnp.int32)
    return {
        'x_prompt': nrm(ks[1], (BATCH, SEQ, D_MODEL), 1.0),
        'x_sample': nrm(ks[2], (DEC_BATCH, DEC_SEQ, D_MODEL), 1.0),
        'cache_sb': nrm(ks[3], (DEPTH, n_pool, PAGE_SIZE, 2, H_C, HEAD_DIM), 1.0),
        'cache_nsa': nrm(ks[4], (DEPTH, n_pool, PAGE_SIZE, 4, G_B, HEAD_DIM), 1.0),
        'state_win': nrm(ks[5], (DEPTH, DEC_BATCH, w_buf, 2, G_B, HEAD_DIM), 1.0),
        'state_conv': nrm(ks[6], (DEPTH, DEC_BATCH, CONV_W - 1, CONV_CH), 1.0),
        'page_table': page_table,
        'c_prompt': nrm(ks[7], (BATCH, D_MODEL), 1.0),
        'c_sample': nrm(ks[8], (DEC_BATCH, D_MODEL), 1.0),
        'rel_bias': nrm(ks[9], (N_BUCKETS, H_B), 0.3),
        'norm_pre': 1.0 + nrm(ks[10], (DEPTH, N_SUB, D_MODEL), 0.02),
        'norm_post': 1.0 + nrm(ks[11], (DEPTH, N_SUB, D_MODEL), 0.02),
        'w_ada': nrm(ks[12], (DEPTH, D_MODEL, N_SUB * 3 * D_MODEL), 0.5 * D_MODEL ** -0.5),
        'b_ada': nrm(ks[13], (DEPTH, N_SUB * 3 * D_MODEL), 0.02),
        'w_ffn_gate': nrm(ks[14], (DEPTH, 2, D_MODEL, D_FF), D_MODEL ** -0.5),
        'w_ffn_up': nrm(ks[15], (DEPTH, 2, D_MODEL, D_FF), D_MODEL ** -0.5),
        'w_ffn_down': nrm(ks[16], (DEPTH, 2, D_FF, D_MODEL), D_FF ** -0.5),
        'w_in': nrm(ks[17], (DEPTH, D_MODEL, P_TOTAL), D_MODEL ** -0.5),
        'conv_w': nrm(ks[18], (DEPTH, CONV_W, CONV_CH), CONV_W ** -0.5),
        'cmp_pos': nrm(ks[19], (DEPTH, 2, L_CMP, HEAD_DIM), 0.1),
        'w_cmp1': nrm(ks[20], (DEPTH, 2, L_CMP * HEAD_DIM, CMP_HID), (L_CMP * HEAD_DIM) ** -0.5),
        'w_cmp2': nrm(ks[21], (DEPTH, 2, CMP_HID, HEAD_DIM), CMP_HID ** -0.5),
        'w_branch': nrm(ks[22], (DEPTH, W_MIX, D_MODEL), W_A ** -0.5),
        'w_out': nrm(ks[23], (DEPTH, D_MODEL, D_MODEL), D_MODEL ** -0.5),
    }


def reference(x_prompt, x_sample, cache_sb, cache_nsa, state_win, state_conv, page_table, c_prompt, c_sample,
              rel_bias, norm_pre, norm_post, w_ada, b_ada, w_ffn_gate, w_ffn_up, w_ffn_down, w_in, conv_w,
              cmp_pos, w_cmp1, w_cmp2, w_branch, w_out):
    y_p = x_prompt
    y_s = x_sample
    p_states = []
    s_states = []
    for l in range(DEPTH):
        lw = {'norm_pre': norm_pre[l], 'norm_post': norm_post[l], 'w_ada': w_ada[l], 'b_ada': b_ada[l],
              'w_ffn_gate': w_ffn_gate[l], 'w_ffn_up': w_ffn_up[l], 'w_ffn_down': w_ffn_down[l],
              'w_in': w_in[l], 'conv_w': conv_w[l], 'cmp_pos': cmp_pos[l], 'w_cmp1': w_cmp1[l],
              'w_cmp2': w_cmp2[l], 'w_branch': w_branch[l], 'w_out': w_out[l]}
        y_p, st_p = decoder_layer(y_p, c_prompt, 0, None, lw, rel_bias)
        past = (gather_pages(cache_sb[l], page_table), gather_pages(cache_nsa[l], page_table),
                state_win[l], state_conv[l])
        y_s, st_s = decoder_layer(y_s, c_sample, PAST_LEN, past, lw, rel_bias)
        p_states.append(st_p)
        s_states.append(st_s)
    sb_p = jnp.stack([s[0] for s in p_states])
    sb_s = jnp.stack([s[0] for s in s_states])
    nsa_p = jnp.stack([s[1] for s in p_states])
    nsa_s = jnp.stack([s[1] for s in s_states])
    win_p = jnp.stack([s[2] for s in p_states])
    win_s = jnp.stack([s[2] for s in s_states])
    conv_p = jnp.stack([s[3] for s in p_states])
    conv_s = jnp.stack([s[3] for s in s_states])
    return (y_p, y_s, sb_p, sb_s, nsa_p, nsa_s, win_p, win_s, conv_p, conv_s)
```

```python
import functools
import math

import numpy as np
import jax
import jax.numpy as jnp
from jax import lax
from jax.experimental import pallas as pl
from jax.experimental.pallas import tpu as pltpu

F32 = jnp.float32
BF16 = jnp.bfloat16

HEAD_DIM = 64
CONV_CH = 512
CONV_W = 3
H_B = 8
G_B = 2
R_B = H_B // G_B
L_CMP = 32
D_CMP = 16
L_SEL = 64
N_SEL = 16
WINDOW = 512
H_C = 8
N_BUCKETS = 32
MAX_DIST = 128
N_SUB = 3
Q_BLOCK = 128
PAGE_SIZE = 128
EPS = 1e-6
NEG = -1e30
MASKED_BELOW = -1e29
FORCE_BONUS = 1e4

W_A = CONV_CH
W_B = H_B * HEAD_DIM
W_C = H_C * HEAD_DIM
KV_B = G_B * HEAD_DIM
LANES = 128
VMEM_LIMIT = 56 * 1024 * 1024

KT_SB = 2 * W_C
KT_NSA = 4 * KV_B
KT_WIN = 2 * KV_B


def _layout(d_model):
    sizes = (W_A, W_A, W_A, W_B, KV_B, KV_B, KV_B, KV_B, KV_B, KV_B, 3 * H_B, W_C, W_C, W_C, 3 * d_model)
    starts = np.concatenate([[0], np.cumsum(sizes)]).astype(np.int64)
    names = ('a_v', 'a_b', 'a_c', 'b_q', 'b_kc', 'b_vc', 'b_ks', 'b_vs', 'b_kw', 'b_vw', 'b_g',
             'c_q', 'c_k', 'c_v', 'g_merge')
    seg = {n: (int(starts[i]), int(starts[i + 1])) for i, n in enumerate(names)}
    q0 = seg['b_q'][0]
    head = lambda h: (q0 + h * HEAD_DIM, q0 + (h + 1) * HEAD_DIM)
    bq_pairs = []
    for p in range(R_B):
        bq_pairs += [head(p), head(p + R_B)]
    row_part = [('g_merge', [seg['g_merge']], 0),
                ('conv', [seg['a_v'], seg['a_b'], seg['a_c']], 0),
                ('b_q', bq_pairs, 0),
                ('c_q', [seg['c_q']], 0),
                ('b_g', [seg['b_g']], LANES - 3 * H_B)]
    t_part = [('sb', [seg['c_k'], seg['c_v']], 0),
              ('nsa', [seg['b_kc'], seg['b_vc'], seg['b_ks'], seg['b_vs']], 0),
              ('win', [seg['b_kw'], seg['b_vw']], 0)]
    offs, o = {}, 0
    for n, rngs, padn in row_part:
        offs[n] = o
        o += sum(b - a for a, b in rngs) + padn
    n_row = o
    for n, rngs, padn in t_part:
        offs[n + '_s'] = o
        o += sum(b - a for a, b in rngs) + padn
    return offs, row_part, t_part, n_row, o


def _gather_cols(w, parts, scale_names):
    cols = []
    for n, rngs, padn in parts:
        for a, b in rngs:
            c = w[:, :, a:b]
            cols.append(c * (HEAD_DIM ** -0.5) if n in scale_names else c)
        if padn:
            cols.append(jnp.zeros(w.shape[:2] + (padn,), w.dtype))
    return jnp.concatenate(cols, axis=-1)


def _t5_thresholds():
    max_exact = N_BUCKETS // 2
    def bucket(n):
        if n < max_exact:
            return n
        b = max_exact + int(math.log(max(n, 1) / max_exact) / math.log(MAX_DIST / max_exact)
                            * (N_BUCKETS - max_exact))
        return min(b, N_BUCKETS - 1)
    thr = []
    for k in range(1, N_BUCKETS):
        n = 0
        while bucket(n) < k:
            n += 1
        thr.append(n)
    return tuple(thr)


_T5_THR = _t5_thresholds()


def _cparams(n_grid, vmem=VMEM_LIMIT):
    return pltpu.CompilerParams(dimension_semantics=("arbitrary",) * n_grid, vmem_limit_bytes=vmem)


def _dot(a, b):
    return jnp.dot(a, b, preferred_element_type=F32)


def _dot_nt(a, b):
    return lax.dot_general(a, b, (((1,), (1,)), ((), ())), preferred_element_type=F32)


def _rms(x):
    return x * lax.rsqrt(jnp.mean(x * x, axis=-1, keepdims=True) + EPS)


def _silu(x):
    return x * jax.nn.sigmoid(x)


def _lane_iota(shape):
    return lax.broadcasted_iota(jnp.int32, shape, len(shape) - 1)


def _row_iota(shape):
    return lax.broadcasted_iota(jnp.int32, shape, len(shape) - 2)


def _split_bf16(x):
    hi = x.astype(BF16)
    lo = (x - hi.astype(F32)).astype(BF16)
    return hi, lo


def _ada_kernel(c_ref, w_ref, b_ref, o_ref):
    h = _silu(c_ref[...]).astype(BF16)
    o_ref[...] = _dot(h, w_ref[...].astype(BF16)) + b_ref[...]


def _ada_all(c_all, w_ada, b_ada, tn=1536):
    depth, d, n = w_ada.shape
    rows = c_all.shape[0]
    return pl.pallas_call(
        _ada_kernel,
        grid=(depth, n // tn),
        in_specs=[pl.BlockSpec((rows, d), lambda l, j: (0, 0)),
                  pl.BlockSpec((None, d, tn), lambda l, j: (l, 0, j)),
                  pl.BlockSpec((None, 1, tn), lambda l, j: (l, 0, j))],
        out_specs=pl.BlockSpec((None, rows, tn), lambda l, j: (l, 0, j)),
        out_shape=jax.ShapeDtypeStruct((depth, rows, n), F32),
        compiler_params=_cparams(2),
        name="ada",
    )(c_all, w_ada, b_ada.reshape(depth, 1, n))


def _table_kernel(rel_ref, o_ref, *, base0, tile_step, row_step, col_step, max_dist):
    t = pl.program_id(0)
    h = pl.program_id(1)
    shape = o_ref.shape
    dist = (base0 + t * tile_step + _row_iota(shape) * row_step - _lane_iota(shape) * col_step)
    b = jnp.full(shape, rel_ref[0, h], F32)
    for k, thr in enumerate(_T5_THR):
        b = jnp.where(dist >= thr, rel_ref[k + 1, h], b)
    ok = dist >= 0
    if max_dist is not None:
        ok = ok & (dist <= max_dist)
    o_ref[...] = jnp.where(ok, b, NEG)


def _bias_table(rel_bias, n_tiles, rows, cols, *, base0, tile_step, row_step, col_step, max_dist=None):
    n_heads = rel_bias.shape[1]
    out = pl.pallas_call(
        functools.partial(_table_kernel, base0=base0, tile_step=tile_step, row_step=row_step,
                          col_step=col_step, max_dist=max_dist),
        grid=(n_tiles, n_heads),
        in_specs=[pl.BlockSpec(memory_space=pltpu.SMEM)],
        out_specs=pl.BlockSpec((None, None, rows, cols), lambda t, h: (t, h, 0, 0)),
        out_shape=jax.ShapeDtypeStruct((n_tiles, n_heads, rows, cols), F32),
        compiler_params=_cparams(2),
        name="bias_table",
    )(rel_bias)
    return out.reshape(n_tiles, n_heads * rows, cols)


def _ffn_kernel(x_ref, ada_ref, gpre_ref, gpost_ref, wg_ref, wu_ref, wd_ref, o_ref, h_ref, acc_ref, *, sub):
    f = pl.program_id(2)

    @pl.when(f == 0)
    def _():
        h = _rms(x_ref[...]) * gpre_ref[...]
        h = h * (1.0 + ada_ref[3 * sub + 1]) + ada_ref[3 * sub]
        h_ref[...] = h.astype(BF16)
        acc_ref[...] = jnp.zeros_like(acc_ref)

    h = h_ref[...]
    a = _silu(_dot(h, wg_ref[...])) * _dot(h, wu_ref[...])
    acc_ref[...] += _dot(a.astype(BF16), wd_ref[...])

    @pl.when(f == pl.num_programs(2) - 1)
    def _():
        y = _rms(acc_ref[...]) * gpost_ref[...]
        o_ref[...] = x_ref[...] + 0.5 * (ada_ref[3 * sub + 2] * y)


def _ffn(x, ada, gpre, gpost, wg, wu, wd, *, sub, tm, tf):
    bg, t, d = x.shape
    f = wg.shape[1]
    mrows = ada.shape[2]
    return pl.pallas_call(
        functools.partial(_ffn_kernel, sub=sub),
        grid=(bg, t // tm, f // tf),
        in_specs=[pl.BlockSpec((None, tm, d), lambda b, i, j: (b, i, 0)),
                  pl.BlockSpec((None, 3 * N_SUB, mrows, d), lambda b, i, j: (b, 0, 0, 0)),
                  pl.BlockSpec((1, d), lambda b, i, j: (0, 0)),
                  pl.BlockSpec((1, d), lambda b, i, j: (0, 0)),
                  pl.BlockSpec((d, tf), lambda b, i, j: (0, j)),
                  pl.BlockSpec((d, tf), lambda b, i, j: (0, j)),
                  pl.BlockSpec((tf, d), lambda b, i, j: (j, 0))],
        out_specs=pl.BlockSpec((None, tm, d), lambda b, i, j: (b, i, 0)),
        out_shape=jax.ShapeDtypeStruct(x.shape, F32),
        scratch_shapes=[pltpu.VMEM((tm, d), BF16), pltpu.VMEM((tm, d), F32)],
        compiler_params=_cparams(3),
        name="ffn",
    )(x, ada, gpre, gpost, wg, wu, wd)


def _mod1(x_ref, ada_ref, gpre_ref):
    h = _rms(x_ref[...]) * gpre_ref[...]
    return (h * (1.0 + ada_ref[4]) + ada_ref[3]).astype(BF16)


def _proj_kernel(x_ref, ada_ref, gpre_ref, w_ref, o_ref, h_ref):
    @pl.when(pl.program_id(2) == 0)
    def _():
        h_ref[...] = _mod1(x_ref, ada_ref, gpre_ref)

    o_ref[...] = _dot(h_ref[...], w_ref[...])


def _proj(x, ada, gpre, w, *, tm, tn):
    bg, t, d = x.shape
    n = w.shape[1]
    mrows = ada.shape[2]
    return pl.pallas_call(
        _proj_kernel,
        grid=(bg, t // tm, n // tn),
        in_specs=[pl.BlockSpec((None, tm, d), lambda b, i, j: (b, i, 0)),
                  pl.BlockSpec((None, 3 * N_SUB, mrows, d), lambda b, i, j: (b, 0, 0, 0)),
                  pl.BlockSpec((1, d), lambda b, i, j: (0, 0)),
                  pl.BlockSpec((d, tn), lambda b, i, j: (0, j))],
        out_specs=pl.BlockSpec((None, tm, tn), lambda b, i, j: (b, i, j)),
        out_shape=jax.ShapeDtypeStruct((bg, t, n), F32),
        scratch_shapes=[pltpu.VMEM((tm, d), BF16)],
        compiler_params=_cparams(3),
        name="proj",
    )(x, ada, gpre, w)


def _proj_t_kernel(x_ref, ada_ref, gpre_ref, w_ref, sb_ref, nsa_ref, win_ref):
    h = _mod1(x_ref, ada_ref, gpre_ref)
    kt = _dot_nt(w_ref[...], h)
    sb_ref[...] = kt[0:KT_SB]
    nsa_ref[...] = kt[KT_SB:KT_SB + KT_NSA]
    win_ref[...] = kt[KT_SB + KT_NSA:KT_SB + KT_NSA + KT_WIN]


def _proj_t(x, ada, gpre, w_t, *, tm):
    bg, t, d = x.shape
    nf = w_t.shape[0]
    out = lambda rows: (pl.BlockSpec((None, rows, tm), lambda b, i: (b, 0, i)),
                        jax.ShapeDtypeStruct((bg, rows, t), F32))
    specs, shapes = zip(out(KT_SB), out(KT_NSA), out(KT_WIN))
    return pl.pallas_call(
        _proj_t_kernel,
        grid=(bg, t // tm),
        in_specs=[pl.BlockSpec((None, tm, d), lambda b, i: (b, i, 0)),
                  pl.BlockSpec((None, 3 * N_SUB, 1, d), lambda b, i: (b, 0, 0, 0)),
                  pl.BlockSpec((1, d), lambda b, i: (0, 0)),
                  pl.BlockSpec((nf, d), lambda b, i: (0, 0))],
        out_specs=list(specs),
        out_shape=list(shapes),
        compiler_params=_cparams(2),
        name="proj_t",
    )(x, ada, gpre, w_t)


def _compress_kernel(h_ref, pos_ref, w1_ref, w2_ref, o_ref):
    ncp = h_ref.shape[1]
    half = h_ref.shape[2]
    out = None
    for g in range(G_B):
        hg = h_ref[g]
        a = _dot((hg + pos_ref[0]).astype(BF16), w1_ref[0:half, :])
        bm = _dot((hg + pos_ref[1]).astype(BF16), w1_ref[half:2 * half, :])
        pre = a + pltpu.roll(bm, ncp - 1, 0)
        o = _dot(_silu(pre).astype(BF16), w2_ref[g])
        out = o if out is None else out + o
    o_ref[...] = out


def _compress_prompt(hrows, pos, w1, w2p):
    b, _, g, ncp, half = hrows.shape
    hid = w1.shape[2]
    return pl.pallas_call(
        _compress_kernel,
        grid=(b, 2),
        in_specs=[pl.BlockSpec((None, None, g, ncp, half), lambda i, k: (i, k, 0, 0, 0)),
                  pl.BlockSpec((None, 2, 1, half), lambda i, k: (k, 0, 0, 0)),
                  pl.BlockSpec((None, 2 * half, hid), lambda i, k: (k, 0, 0)),
                  pl.BlockSpec((None, g, hid, LANES), lambda i, k: (k, 0, 0, 0))],
        out_specs=pl.BlockSpec((None, None, ncp, LANES), lambda i, k: (i, k, 0, 0)),
        out_shape=jax.ShapeDtypeStruct((b, 2, ncp, LANES), F32),
        compiler_params=_cparams(2),
        name="compress",
    )(hrows, pos, w1, w2p)


def _stack_heads(blk_fn, tq):
    lane = _lane_iota((tq, LANES))
    lo = [jnp.where(lane < HEAD_DIM, blk_fn(p), 0.0) for p in range(R_B)]
    hi = [jnp.where(lane >= HEAD_DIM, blk_fn(p), 0.0) for p in range(R_B)]
    return jnp.concatenate(lo + hi, axis=0)


def _unstack_gated(o, gsig, branch, tq):
    lane = _lane_iota((tq, LANES))
    out = []
    for p in range(R_B):
        c_lo = 3 * p + branch
        c_hi = 3 * (p + R_B) + branch
        blk = jnp.where(lane < HEAD_DIM, o[p * tq:(p + 1) * tq], o[(p + R_B) * tq:(p + R_B + 1) * tq])
        gate = jnp.where(lane < HEAD_DIM, gsig[:, c_lo:c_lo + 1], gsig[:, c_hi:c_hi + 1])
        out.append(blk * gate)
    return out


def _topk_penalty(score_t, n_top):
    nb = score_t.shape[0]
    jrow = _row_iota(score_t.shape).astype(F32)
    x = score_t
    sel = jnp.zeros(score_t.shape, F32)
    for _ in range(n_top):
        mx = jnp.max(x, axis=0, keepdims=True)
        first = jnp.min(jnp.where(x == mx, jrow, float(nb)), axis=0, keepdims=True)
        hit = jrow == first
        sel = jnp.where(hit, 1.0, sel)
        x = jnp.where(hit, -jnp.inf, x)
    return jnp.where(sel > 0.5, 0.0, NEG)


def _cmp_kernel(q_ref, kc_ref, vc_ref, tab_ref, ovl_ref, bg_ref, o_ref, pen_ref, *, n_top):
    tq = q_ref.shape[0]
    i = pl.program_id(1)
    q8 = _stack_heads(lambda p: q_ref[:, p * LANES:(p + 1) * LANES], tq).astype(BF16)
    s = _dot_nt(q8, kc_ref[...].astype(BF16)) + tab_ref[...]
    ok = s > MASKED_BELOW
    e = jnp.where(ok, jnp.exp(s - jnp.max(s, axis=-1, keepdims=True)), 0.0)
    p = (e / jnp.maximum(jnp.sum(e, axis=-1, keepdims=True), 1e-30)).astype(BF16)
    o = _dot(p, vc_ref[...].astype(BF16))
    gsig = jax.nn.sigmoid(bg_ref[...])
    for pi, blk in enumerate(_unstack_gated(o, gsig, 0, tq)):
        o_ref[:, pi * LANES:(pi + 1) * LANES] = blk
    imp8 = _dot(p, ovl_ref[...])
    lane = _lane_iota((tq, LANES))
    g0 = imp8[0:tq] + imp8[tq:2 * tq] + imp8[2 * tq:3 * tq] + imp8[3 * tq:4 * tq]
    g1 = imp8[4 * tq:5 * tq] + imp8[5 * tq:6 * tq] + imp8[6 * tq:7 * tq] + imp8[7 * tq:8 * tq]
    imp = jnp.where(lane < L_SEL, g0, g1)
    qpos = i * tq + _row_iota((tq, LANES))
    j = lane % L_SEL
    cur = qpos // L_SEL
    forced = (j == 0) | (j == cur) | (j == cur - 1)
    score = jnp.where(j * L_SEL <= qpos, imp + FORCE_BONUS * forced.astype(F32), NEG)
    st = score.T
    pen_t = jnp.concatenate([_topk_penalty(st[g * L_SEL:(g + 1) * L_SEL], n_top) for g in range(G_B)], axis=0)
    pen_ref[...] = pen_t.T


def _cmp_prompt(projr, kvc, tab, ovl, offs, *, n_top, tq=Q_BLOCK):
    b, t, _ = projr.shape
    ncp = kvc.shape[2]
    qb = offs['b_q'] // W_B
    gb = offs['b_g'] // LANES
    return pl.pallas_call(
        functools.partial(_cmp_kernel, n_top=n_top),
        grid=(b, t // tq),
        in_specs=[pl.BlockSpec((None, tq, W_B), lambda bi, i: (bi, i, qb)),
                  pl.BlockSpec((None, None, ncp, LANES), lambda bi, i: (bi, 0, 0, 0)),
                  pl.BlockSpec((None, None, ncp, LANES), lambda bi, i: (bi, 1, 0, 0)),
                  pl.BlockSpec((None, H_B * tq, ncp), lambda bi, i: (i, 0, 0)),
                  pl.BlockSpec((ncp, LANES), lambda bi, i: (0, 0)),
                  pl.BlockSpec((None, tq, LANES), lambda bi, i: (bi, i, gb))],
        out_specs=[pl.BlockSpec((None, tq, W_B), lambda bi, i: (bi, i, 0)),
                   pl.BlockSpec((None, tq, LANES), lambda bi, i: (bi, i, 0))],
        out_shape=[jax.ShapeDtypeStruct((b, t, W_B), F32), jax.ShapeDtypeStruct((b, t, LANES), F32)],
        compiler_params=_cparams(2),
        name="nsa_cmp",
    )(projr, kvc, kvc, tab, ovl, projr)


def _nsa_flash_kernel(*refs, mode, branch):
    if mode == 'sel':
        q_ref, k_ref, v_ref, tab_ref, pen_ref, bg_ref, o_ref, qa_ref, m_ref, l_ref, acc_ref = refs
        far_ref = None
    else:
        q_ref, k_ref, v_ref, tab_ref, far_ref, bg_ref, o_ref, qa_ref, m_ref, l_ref, acc_ref = refs
        pen_ref = None
    tq = q_ref.shape[0]
    tk = tq
    i = pl.program_id(1)

    qa_ref[:, 0:LANES] = _stack_heads(lambda p: q_ref[:, p * LANES:(p + 1) * LANES], tq).astype(BF16)
    if mode == 'sel':
        lane = _lane_iota((tq, LANES))
        pen = pen_ref[...]
        lo = jnp.where(lane < L_SEL, pen, 0.0).astype(BF16)
        hi = jnp.where(lane >= L_SEL, pen, 0.0).astype(BF16)
        qa_ref[:, LANES:2 * LANES] = jnp.concatenate([lo] * R_B + [hi] * R_B, axis=0)
    m_ref[...] = jnp.full_like(m_ref, NEG)
    l_ref[...] = jnp.zeros_like(l_ref)
    acc_ref[...] = jnp.zeros_like(acc_ref)

    def step(j, bias):
        k0 = pl.multiple_of(j * tk, tk)
        kt = k_ref[:, pl.ds(k0, tk)].astype(BF16)
        if mode == 'sel':
            blk = 2 * j + _lane_iota((LANES, tk)) // L_SEL
            et = jnp.where(blk == _row_iota((LANES, tk)) % L_SEL, 1.0, 0.0).astype(BF16)
            kt = jnp.concatenate([kt, et], axis=0)
        s = _dot(qa_ref[...], kt) + bias
        m_old = m_ref[...]
        m_new = jnp.maximum(m_old, jnp.max(s, axis=-1, keepdims=True))
        alpha = jnp.exp(m_old - m_new)
        p = jnp.where(s > MASKED_BELOW, jnp.exp(s - m_new), 0.0)
        l_ref[...] = alpha * l_ref[...] + jnp.sum(p, axis=-1, keepdims=True)
        acc_ref[...] = alpha * acc_ref[...] + _dot_nt(p.astype(BF16), v_ref[:, pl.ds(k0, tk)].astype(BF16))
        m_ref[...] = m_new

    far_bias = tab_ref[1, :, 0:1]
    n_far = WINDOW // tk - 2

    def far_body(j, c):
        step(j, far_bias)
        return c

    if mode == 'sel':
        lax.fori_loop(0, jnp.maximum(i - 1, 0), far_body, 0)
    else:
        lax.fori_loop(jnp.maximum(i - 1 - n_far, 0), jnp.maximum(i - 1, 0), far_body, 0)

        @pl.when(i >= n_far + 2)
        def _():
            step(i - n_far - 2, far_ref[0])

    @pl.when(i >= 1)
    def _():
        step(i - 1, tab_ref[1])

    step(i, tab_ref[0])

    o = acc_ref[...] / jnp.maximum(l_ref[...], 1e-30)
    gsig = jax.nn.sigmoid(bg_ref[...])
    for pi, blk in enumerate(_unstack_gated(o, gsig, branch, tq)):
        o_ref[:, pi * LANES:(pi + 1) * LANES] = blk


def _nsa_flash_prompt(projr, kt, tab, extra, offs, *, mode, tq=Q_BLOCK):
    b, t, _ = projr.shape
    assert WINDOW % tq == 0 and WINDOW // tq >= 2
    qb = offs['b_q'] // W_B
    gb = offs['b_g'] // LANES
    if mode == 'sel':
        kb, vb, branch = 2, 3, 1
        extra_spec = pl.BlockSpec((None, tq, LANES), lambda bi, i: (bi, i, 0))
        qa_w = 2 * LANES
    else:
        kb, vb, branch = 0, 1, 2
        extra_spec = pl.BlockSpec((1, H_B * tq, tq), lambda bi, i: (0, 0, 0))
        qa_w = LANES
    return pl.pallas_call(
        functools.partial(_nsa_flash_kernel, mode=mode, branch=branch),
        grid=(b, t // tq),
        in_specs=[pl.BlockSpec((None, tq, W_B), lambda bi, i: (bi, i, qb)),
                  pl.BlockSpec((None, KV_B, t), lambda bi, i: (bi, kb, 0)),
                  pl.BlockSpec((None, KV_B, t), lambda bi, i: (bi, vb, 0)),
                  pl.BlockSpec((2, H_B * tq, tq), lambda bi, i: (0, 0, 0)),
                  extra_spec,
                  pl.BlockSpec((None, tq, LANES), lambda bi, i: (bi, i, gb))],
        out_specs=pl.BlockSpec((None, tq, W_B), lambda bi, i: (bi, i, 0)),
        out_shape=jax.ShapeDtypeStruct((b, t, W_B), F32),
        scratch_shapes=[pltpu.VMEM((H_B * tq, qa_w), BF16),
                        pltpu.VMEM((H_B * tq, 1), F32),
                        pltpu.VMEM((H_B * tq, 1), F32),
                        pltpu.VMEM((H_B * tq, LANES), F32)],
        compiler_params=_cparams(2),
        name="nsa_" + mode,
    )(projr, kt, kt, tab, extra, projr)


def _sb_kernel(q_ref, k_ref, v_ref, o_ref, qs_ref, carry_ref, acc_ref):
    tq = q_ref.shape[0]
    tk = tq
    i = pl.program_id(2)
    lane = _lane_iota((tq, LANES))
    q = q_ref[...]
    qs_ref[...] = jnp.concatenate([jnp.where(lane < HEAD_DIM, q, 0.0),
                                   jnp.where(lane >= HEAD_DIM, q, 0.0)], axis=0).astype(BF16)
    carry_ref[...] = jnp.zeros_like(carry_ref)
    acc_ref[...] = jnp.zeros_like(acc_ref)
    tri = jnp.where(_row_iota((tk, tk)) > _lane_iota((tk, tk)), 1.0, 0.0).astype(BF16)

    def step(j, diagonal):
        k0 = pl.multiple_of(j * tk, tk)
        z = _dot(qs_ref[...], k_ref[:, pl.ds(k0, tk)].astype(BF16))
        t = jnp.log1p(jnp.exp(-jnp.abs(z)))
        log_1mb = -(jnp.maximum(z, 0.0) + t)
        log_b = jnp.minimum(z, 0.0) - t
        if diagonal:
            rr = _row_iota((2 * tq, tk))
            causal = _lane_iota((2 * tq, tk)) < jnp.where(rr >= tq, rr - tq, rr)
            log_1mb = jnp.where(causal, log_1mb, 0.0)
        hi, lo = _split_bf16(log_1mb)
        after = _dot(hi, tri) + _dot(lo, tri) + carry_ref[...]
        a = jnp.exp(log_b + after)
        if diagonal:
            a = jnp.where(causal, a, 0.0)
        carry_ref[...] += jnp.sum(log_1mb, axis=-1, keepdims=True)
        acc_ref[...] += _dot_nt(a.astype(BF16), v_ref[:, pl.ds(k0, tk)].astype(BF16))

    step(i, True)

    def body(n, c):
        step(i - 1 - n, False)
        return c

    lax.fori_loop(0, i, body, 0)
    acc = acc_ref[...]
    o_ref[...] = jnp.where(lane < HEAD_DIM, acc[0:tq], acc[tq:2 * tq])


def _sb_prompt(projr, sbt, offs, *, tq=Q_BLOCK):
    b, t, _ = projr.shape
    n_pairs = W_C // LANES
    qb = offs['c_q'] // LANES
    return pl.pallas_call(
        _sb_kernel,
        grid=(b, n_pairs, t // tq),
        in_specs=[pl.BlockSpec((None, tq, LANES), lambda bi, p, i: (bi, i, qb + p)),
                  pl.BlockSpec((None, LANES, t), lambda bi, p, i: (bi, p, 0)),
                  pl.BlockSpec((None, LANES, t), lambda bi, p, i: (bi, n_pairs + p, 0))],
        out_specs=pl.BlockSpec((None, tq, LANES), lambda bi, p, i: (bi, i, p)),
        out_shape=jax.ShapeDtypeStruct((b, t, W_C), F32),
        scratch_shapes=[pltpu.VMEM((2 * tq, LANES), BF16),
                        pltpu.VMEM((2 * tq, 1), F32),
                        pltpu.VMEM((2 * tq, LANES), F32)],
        compiler_params=_cparams(3),
        name="sb",
    )(projr, sbt, sbt)


def _merge_kernel(*refs, decode):
    if decode:
        (x_ref, ada_ref, gpost_ref, conv_ref, s0_ref, s1_ref, gm0_ref, gm1_ref, gm2_ref, yb0_ref, yb1_ref,
         yb2_ref, yc_ref, cw_ref, wba_ref, wbb_ref, wbc_ref, wo_ref, o_ref, u_ref) = refs
    else:
        (x_ref, ada_ref, gpost_ref, conv_ref, halo_ref, gm0_ref, gm1_ref, gm2_ref, yb0_ref, yb1_ref, yb2_ref,
         yc_ref, cw_ref, wba_ref, wbb_ref, wbc_ref, wo_ref, o_ref, u_ref) = refs
    tm = x_ref.shape[0]
    u = conv_ref[:, 2 * W_A:3 * W_A] * conv_ref[:, 0:W_A]
    if decode:
        um2 = s0_ref[...]
        um1 = s1_ref[...]
        u_ref[...] = u
    else:
        i = pl.program_id(1)
        keep = (i > 0).astype(F32)
        hu = (halo_ref[:, 2 * W_A:3 * W_A] * halo_ref[:, 0:W_A]) * keep
        row = _row_iota((tm, W_A))
        um1 = jnp.where(row >= 1, pltpu.roll(u, 1, 0), hu[7:8])
        um2 = jnp.where(row >= 2, pltpu.roll(u, 2, 0), jnp.where(row == 1, hu[7:8], hu[6:7]))

        @pl.when(i == pl.num_programs(1) - 1)
        def _():
            u_ref[...] = u[tm - 8:tm]

    y_a = conv_ref[:, W_A:2 * W_A] * (cw_ref[0:1] * um2 + cw_ref[1:2] * um1 + cw_ref[2:3] * u)
    y_b = yb0_ref[...] + yb1_ref[...] + yb2_ref[...]
    merged = (jax.nn.sigmoid(gm0_ref[...]) * _dot(y_a.astype(BF16), wba_ref[...])
              + jax.nn.sigmoid(gm1_ref[...]) * _dot(y_b.astype(BF16), wbb_ref[...])
              + jax.nn.sigmoid(gm2_ref[...]) * _dot(yc_ref[...].astype(BF16), wbc_ref[...]))
    out = _dot(merged.astype(BF16), wo_ref[...])
    o_ref[...] = x_ref[...] + ada_ref[5] * (_rms(out) * gpost_ref[...])


def _merge(x, ada, gpost, projr, conv_state, yb, yc, cw, wba, wbb, wbc, wo, offs, *, tm, decode):
    bg, t, d = x.shape
    mrows = ada.shape[2]
    cb = offs['conv'] // (3 * W_A)
    gmb = offs['g_merge'] // d
    row_spec = lambda w, cblk: pl.BlockSpec((None, tm, w), lambda b, i: (b, i, cblk))
    full = lambda a: pl.BlockSpec(a.shape, lambda b, i: (0,) * a.ndim)
    if decode:
        ctx_specs = [row_spec(W_A, 0), row_spec(W_A, 0)]
        ctx = list(conv_state)
        u_shape, u_spec = (bg, t, W_A), row_spec(W_A, 0)
    else:
        r8 = tm // 8
        ctx_specs = [pl.BlockSpec((None, 8, 3 * W_A), lambda b, i: (b, jnp.maximum(i * r8 - 1, 0), cb))]
        ctx = [projr]
        u_shape, u_spec = (bg, 8, W_A), pl.BlockSpec((None, 8, W_A), lambda b, i: (b, 0, 0))
    out, u = pl.pallas_call(
        functools.partial(_merge_kernel, decode=decode),
        grid=(bg, t // tm),
        in_specs=[row_spec(d, 0),
                  pl.BlockSpec((None, 3 * N_SUB, mrows, d), lambda b, i: (b, 0, 0, 0)),
                  pl.BlockSpec((1, d), lambda b, i: (0, 0)),
                  row_spec(3 * W_A, cb)] + ctx_specs + [
                  row_spec(d, gmb), row_spec(d, gmb + 1), row_spec(d, gmb + 2),
                  row_spec(W_B, 0), row_spec(W_B, 0), row_spec(W_B, 0), row_spec(W_C, 0),
                  full(cw), full(wba), full(wbb), full(wbc), full(wo)],
        out_specs=[row_spec(d, 0), u_spec],
        out_shape=[jax.ShapeDtypeStruct(x.shape, F32), jax.ShapeDtypeStruct(u_shape, F32)],
        compiler_params=_cparams(2),
        name="merge",
    )(x, ada, gpost, projr, *ctx, projr, projr, projr, yb[0], yb[1], yb[2], yc, cw, wba, wbb, wbc, wo)
    return out, u


def _dec_cmp_kernel(pt_ref, *refs, n_pages_step, n_top, past):
    del pt_ref
    pages = refs[:n_pages_step]
    (q_ref, pos_ref, w1_ref, w2_ref, tab_ref, ovl_ref, bg_ref, o_ref, idx_ref, h_ref) = refs[n_pages_step:]
    c = pl.program_id(1)
    lane = _lane_iota((8, LANES))
    rpp = PAGE_SIZE // D_CMP
    ii = _row_iota((PAGE_SIZE, PAGE_SIZE))
    perm = jnp.where(_lane_iota((PAGE_SIZE, PAGE_SIZE)) == D_CMP * (ii % rpp) + ii // rpp, 1.0, 0.0).astype(BF16)
    for k in range(n_pages_step):
        row0 = pl.multiple_of((c * n_pages_step + k) * rpp, rpp)
        xt = pages[k][...].reshape(2 * KV_B, PAGE_SIZE).astype(BF16)
        rows = _dot_nt(perm, xt)
        for m in range(D_CMP // 2):
            ev = rows[2 * m * rpp:(2 * m + 1) * rpp]
            od = rows[(2 * m + 1) * rpp:(2 * m + 2) * rpp]
            for kind in range(2):
                e = ev[:, kind * LANES:(kind + 1) * LANES]
                o = od[:, kind * LANES:(kind + 1) * LANES]
                h_ref[kind, 0, pl.ds(row0, rpp), m * LANES:(m + 1) * LANES] = (
                    jnp.where(lane < HEAD_DIM, e, pltpu.roll(o, HEAD_DIM, 1)))
                h_ref[kind, 1, pl.ds(row0, rpp), m * LANES:(m + 1) * LANES] = (
                    jnp.where(lane < HEAD_DIM, pltpu.roll(e, HEAD_DIM, 1), o))

    @pl.when(c == pl.num_programs(1) - 1)
    def _():
        ncp = h_ref.shape[2]
        half = h_ref.shape[3]
        kv = []
        for kind in range(2):
            out = None
            for g in range(G_B):
                hg = h_ref[kind, g]
                a = _dot((hg + pos_ref[kind, 0]).astype(BF16), w1_ref[kind, 0:half, :])
                bm = _dot((hg + pos_ref[kind, 1]).astype(BF16), w1_ref[kind, half:2 * half, :])
                pre = a + pltpu.roll(bm, ncp - 1, 0)
                o = _dot(_silu(pre).astype(BF16), w2_ref[kind, g])
                out = o if out is None else out + o
            kv.append(out.astype(BF16))
        q8 = _stack_heads(lambda p: q_ref[:, p * LANES:(p + 1) * LANES], 1).astype(BF16)
        s = _dot_nt(q8, kv[0]) + tab_ref[...]
        ok = s > MASKED_BELOW
        e = jnp.where(ok, jnp.exp(s - jnp.max(s, axis=-1, keepdims=True)), 0.0)
        p = (e / jnp.maximum(jnp.sum(e, axis=-1, keepdims=True), 1e-30)).astype(BF16)
        o = _dot(p, kv[1])
        gsig = jax.nn.sigmoid(bg_ref[...])
        for pi, blk in enumerate(_unstack_gated(o, gsig, 0, 1)):
            o_ref[:, pi * LANES:(pi + 1) * LANES] = blk
        imp8 = _dot(p, ovl_ref[...])
        imp = jnp.concatenate([imp8[0:1] + imp8[1:2] + imp8[2:3] + imp8[3:4],
                               imp8[4:5] + imp8[5:6] + imp8[6:7] + imp8[7:8]], axis=0)
        nbl = imp.shape[1]
        j = _lane_iota(imp.shape)
        cur = past // L_SEL
        forced = (j == 0) | (j == cur) | (j == cur - 1)
        x = jnp.where(j * L_SEL <= past, imp + FORCE_BONUS * forced.astype(F32), NEG)
        jf = j.astype(F32)
        slot = _lane_iota((G_B, LANES))
        idx = jnp.zeros((G_B, LANES), F32)
        for it in range(n_top):
            mx = jnp.max(x, axis=-1, keepdims=True)
            first = jnp.min(jnp.where(x == mx, jf, float(nbl)), axis=-1, keepdims=True)
            idx = jnp.where(slot == it, first, idx)
            x = jnp.where(jf == first, -jnp.inf, x)
        idx_ref[...] = jnp.zeros(idx_ref.shape, jnp.int32)
        idx_ref[0:G_B, :] = idx.astype(jnp.int32)


def _dec_cmp(layer, page_table, cache_nsa_t, proj_s, pos, w1, w2p, tab, ovl, offs, *, n_top, n_pages_step=8):
    nseq, n_pages = page_table.shape
    past = n_pages * PAGE_SIZE
    ncp = past // D_CMP
    half = D_CMP * HEAD_DIM
    qb = offs['b_q'] // W_B
    gb = offs['b_g'] // LANES
    page_specs = [pl.BlockSpec((None, None, 2, G_B, HEAD_DIM, PAGE_SIZE),
                               functools.partial(
                                   lambda b, c, pt, k: (layer, pt[b, c * n_pages_step + k], 0, 0, 0, 0), k=k))
                  for k in range(n_pages_step)]
    full = lambda a: pl.BlockSpec(a.shape, lambda b, c, pt: (0,) * a.ndim)
    grid_spec = pltpu.PrefetchScalarGridSpec(
        num_scalar_prefetch=1,
        grid=(nseq, n_pages // n_pages_step),
        in_specs=page_specs + [pl.BlockSpec((None, 1, W_B), lambda b, c, pt: (b, 0, qb)),
                               full(pos), full(w1), full(w2p), full(tab), full(ovl),
                               pl.BlockSpec((None, 1, LANES), lambda b, c, pt: (b, 0, gb))],
        out_specs=[pl.BlockSpec((None, 1, W_B), lambda b, c, pt: (b, 0, 0)),
                   pl.BlockSpec((None, 8, LANES), lambda b, c, pt: (b, 0, 0))],
        scratch_shapes=[pltpu.VMEM((2, G_B, ncp, half), F32)],
    )
    return pl.pallas_call(
        functools.partial(_dec_cmp_kernel, n_pages_step=n_pages_step, n_top=n_top, past=past),
        grid_spec=grid_spec,
        out_shape=[jax.ShapeDtypeStruct((nseq, 1, W_B), F32), jax.ShapeDtypeStruct((nseq, 8, LANES), jnp.int32)],
        compiler_params=_cparams(2),
        name="dec_cmp",
    )(page_table, *([cache_nsa_t] * n_pages_step), proj_s, pos, w1, w2p, tab, ovl, proj_s)


def _dec_selwin_kernel(sp_ref, *refs, n_top, past, n_pages):
    n_blk = G_B * n_top
    blocks = refs[:n_blk]
    (q_ref, ksn_ref, vsn_ref, wst_ref, kwn_ref, vwn_ref, tabw_ref, relt_ref, bg_ref,
     osel_ref, owin_ref, kc_ref, vc_ref) = refs[n_blk:]
    b = pl.program_id(0)
    n_past_blk = past // L_SEL
    halves = PAGE_SIZE // L_SEL
    q8f = _stack_heads(lambda p: q_ref[:, p * LANES:(p + 1) * LANES], 1)
    q8 = q8f.astype(BF16)
    q8r = q8.astype(F32)
    gsig = jax.nn.sigmoid(bg_ref[...])
    row1 = _row_iota((H_B, 1))

    for n in range(n_blk):
        kc_ref[:, n * PAGE_SIZE:(n + 1) * PAGE_SIZE] = blocks[n][0].reshape(KV_B, PAGE_SIZE).astype(BF16)
        vc_ref[:, n * PAGE_SIZE:(n + 1) * PAGE_SIZE] = blocks[n][1].reshape(KV_B, PAGE_SIZE).astype(BF16)
    nk = n_blk * PAGE_SIZE
    s = _dot(q8, kc_ref[...])
    lane = _lane_iota((H_B, nk))
    slot = lane // PAGE_SIZE
    within = lane % PAGE_SIZE
    blk_of = jnp.zeros((H_B, nk), jnp.int32)
    has_new = [jnp.int32(0)] * G_B
    for n in range(n_blk):
        bi = sp_ref[b, n_pages + n]
        blk_of = jnp.where(slot == n, bi, blk_of)
        has_new[n // n_top] = jnp.maximum(has_new[n // n_top], (bi == n_past_blk).astype(jnp.int32))
    kpos = (blk_of // halves) * PAGE_SIZE + within
    dist = past - kpos
    row = _row_iota((H_B, nk))
    ok = ((within // L_SEL == blk_of % halves) & (blk_of < n_past_blk)
          & (row // R_B == slot // n_top))
    bias = jnp.broadcast_to(relt_ref[:, 0:1], (H_B, nk))
    for k, thr in enumerate(_T5_THR):
        bias = jnp.where(dist >= thr, relt_ref[:, k + 1:k + 2], bias)
    s = jnp.where(ok, s + bias, NEG)
    new_ok = jnp.where(row1 < R_B, has_new[0], has_new[1]) > 0
    s_new = jnp.sum(q8r * ksn_ref[...].astype(BF16).astype(F32), axis=-1, keepdims=True)
    s_new = jnp.where(new_ok, s_new + relt_ref[:, 0:1], NEG)
    m = jnp.maximum(jnp.max(s, axis=-1, keepdims=True), s_new)
    e = jnp.where(ok, jnp.exp(s - m), 0.0)
    e_new = jnp.where(new_ok, jnp.exp(s_new - m), 0.0)
    den = jnp.maximum(jnp.sum(e, axis=-1, keepdims=True) + e_new, 1e-30)
    o = (_dot_nt(e.astype(BF16), vc_ref[...]) + e_new * vsn_ref[...].astype(BF16).astype(F32)) / den
    for pi, blk in enumerate(_unstack_gated(o, gsig, 1, 1)):
        osel_ref[:, pi * LANES:(pi + 1) * LANES] = blk

    n_ctx = wst_ref.shape[-1]
    sw = _dot(q8, wst_ref[0].reshape(KV_B, n_ctx).astype(BF16)) + tabw_ref[...]
    okw = sw > MASKED_BELOW
    sw_new = jnp.sum(q8r * kwn_ref[...].astype(BF16).astype(F32), axis=-1, keepdims=True) + relt_ref[:, 0:1]
    mw = jnp.maximum(jnp.max(sw, axis=-1, keepdims=True), sw_new)
    ew = jnp.where(okw, jnp.exp(sw - mw), 0.0)
    ew_new = jnp.exp(sw_new - mw)
    denw = jnp.maximum(jnp.sum(ew, axis=-1, keepdims=True) + ew_new, 1e-30)
    ow = (_dot_nt(ew.astype(BF16), wst_ref[1].reshape(KV_B, n_ctx).astype(BF16))
          + ew_new * vwn_ref[...].astype(BF16).astype(F32)) / denw
    for pi, blk in enumerate(_unstack_gated(ow, gsig, 2, 1)):
        owin_ref[:, pi * LANES:(pi + 1) * LANES] = blk


def _dec_selwin(layer, sel_table, n_pages, cache_nsa_t, proj_s, win_state_t, tabw, rel_t, offs, *, n_top):
    nseq = sel_table.shape[0]
    past = n_pages * PAGE_SIZE
    n_blk = G_B * n_top
    n_past_blk = past // L_SEL
    halves = PAGE_SIZE // L_SEL
    n_ctx = win_state_t.shape[-1]
    qb = offs['b_q'] // W_B
    gb = offs['b_g'] // LANES
    nb = offs['nsa_s'] // LANES
    wb = offs['win_s'] // LANES

    def blk_map(b, sp, n):
        bi = jnp.clip(sp[b, n_pages + n], 0, n_past_blk - 1)
        return (layer, sp[b, bi // halves], 1, 0, 0, 0)

    blk_specs = [pl.BlockSpec((None, None, 2, G_B, HEAD_DIM, PAGE_SIZE), functools.partial(blk_map, n=n))
                 for n in range(n_blk)]
    row = lambda cblk: pl.BlockSpec((None, 1, LANES), lambda b, sp: (b, 0, cblk))
    full = lambda a: pl.BlockSpec(a.shape, lambda b, sp: (0,) * a.ndim)
    grid_spec = pltpu.PrefetchScalarGridSpec(
        num_scalar_prefetch=1,
        grid=(nseq,),
        in_specs=blk_specs + [pl.BlockSpec((None, 1, W_B), lambda b, sp: (b, 0, qb)),
                              row(nb + 2), row(nb + 3),
                              pl.BlockSpec((None, None, 2, G_B, HEAD_DIM, n_ctx),
                                           lambda b, sp: (layer, b, 0, 0, 0, 0)),
                              row(wb), row(wb + 1), full(tabw), full(rel_t), row(gb)],
        out_specs=[pl.BlockSpec((None, 1, W_B), lambda b, sp: (b, 0, 0)),
                   pl.BlockSpec((None, 1, W_B), lambda b, sp: (b, 0, 0))],
        scratch_shapes=[pltpu.VMEM((KV_B, n_blk * PAGE_SIZE), BF16), pltpu.VMEM((KV_B, n_blk * PAGE_SIZE), BF16)],
    )
    return pl.pallas_call(
        functools.partial(_dec_selwin_kernel, n_top=n_top, past=past, n_pages=n_pages),
        grid_spec=grid_spec,
        out_shape=[jax.ShapeDtypeStruct((nseq, 1, W_B), F32)] * 2,
        compiler_params=_cparams(1),
        name="dec_selwin",
    )(sel_table, *([cache_nsa_t] * n_blk), proj_s, proj_s, proj_s, win_state_t, proj_s, proj_s,
      tabw, rel_t, proj_s)


def _dec_sb_kernel(pt_ref, *refs, n_pages_step):
    del pt_ref
    pages = refs[:n_pages_step]
    q_ref, o_ref, qcol_ref, carry_ref, acc_ref = refs[n_pages_step:]
    c = pl.program_id(1)
    ps = PAGE_SIZE

    @pl.when(c == 0)
    def _():
        qt = jnp.broadcast_to(q_ref[...], (LANES, W_C)).T
        qcol_ref[...] = qt.reshape(H_C, HEAD_DIM, LANES)
        carry_ref[...] = jnp.zeros_like(carry_ref)
        acc_ref[...] = jnp.zeros_like(acc_ref)

    tri = jnp.where(_row_iota((ps, ps)) > _lane_iota((ps, ps)), 1.0, 0.0).astype(BF16)
    hrow = _row_iota((H_C, ps))
    for k in range(n_pages_step):
        z = jnp.zeros((H_C, ps), F32)
        for h in range(H_C):
            zh = jnp.sum(qcol_ref[h] * pages[k][0, h], axis=0, keepdims=True)
            z = jnp.where(hrow == h, zh, z)
        t = jnp.log1p(jnp.exp(-jnp.abs(z)))
        log_1mb = -(jnp.maximum(z, 0.0) + t)
        log_b = jnp.minimum(z, 0.0) - t
        hi, lo = _split_bf16(log_1mb)
        after = _dot(hi, tri) + _dot(lo, tri) + carry_ref[...]
        a = jnp.exp(log_b + after)
        carry_ref[...] += jnp.sum(log_1mb, axis=-1, keepdims=True)
        for h in range(H_C):
            acc_ref[h] += a[h:h + 1, :] * pages[k][1, h]

    @pl.when(c == pl.num_programs(1) - 1)
    def _():
        acc = acc_ref[...].reshape(W_C, LANES)
        ones = jnp.ones((8, LANES), BF16)
        hi = acc.astype(BF16)
        r1 = acc - hi.astype(F32)
        mid = r1.astype(BF16)
        lo = (r1 - mid.astype(F32)).astype(BF16)
        tot = _dot_nt(ones, hi) + _dot_nt(ones, mid) + _dot_nt(ones, lo)
        o_ref[...] = tot[0:1]


def _dec_sb(layer, page_table, cache_sb_t, proj_s, offs, *, n_pages_step=8):
    nseq, n_pages = page_table.shape
    qb = offs['c_q'] // W_C
    page_specs = [pl.BlockSpec((None, None, 2, H_C, HEAD_DIM, PAGE_SIZE),
                               functools.partial(
                                   lambda b, c, pt, k: (layer, pt[b, n_pages - 1 - (c * n_pages_step + k)],
                                                        0, 0, 0, 0), k=k))
                  for k in range(n_pages_step)]
    grid_spec = pltpu.PrefetchScalarGridSpec(
        num_scalar_prefetch=1,
        grid=(nseq, n_pages // n_pages_step),
        in_specs=page_specs + [pl.BlockSpec((None, 1, W_C), lambda b, c, pt: (b, 0, qb))],
        out_specs=pl.BlockSpec((None, 1, W_C), lambda b, c, pt: (b, 0, 0)),
        scratch_shapes=[pltpu.VMEM((H_C, HEAD_DIM, LANES), F32), pltpu.VMEM((H_C, 1), F32),
                        pltpu.VMEM((H_C, HEAD_DIM, LANES), F32)],
    )
    return pl.pallas_call(
        functools.partial(_dec_sb_kernel, n_pages_step=n_pages_step),
        grid_spec=grid_spec,
        out_shape=jax.ShapeDtypeStruct((nseq, 1, W_C), F32),
        compiler_params=_cparams(2),
        name="dec_sb",
    )(page_table, *([cache_sb_t] * n_pages_step), proj_s)


def _overlap(ncp, n_lanes, reps):
    c0 = np.arange(ncp)[:, None] * D_CMP
    s0 = np.arange(n_lanes)[None, :] * L_SEL
    ov = ((c0 < s0 + L_SEL) & (c0 + L_CMP > s0)).astype(np.float32)
    return jnp.asarray(np.tile(ov, (1, reps)), BF16)


def _flat16_t(kt):
    b, _, t = kt.shape
    r = kt.reshape(b, G_B, HEAD_DIM, t // D_CMP, D_CMP).transpose(0, 1, 3, 4, 2)
    return r.reshape(b, G_B, t // D_CMP, D_CMP * HEAD_DIM)


def kernel(x_prompt, x_sample, cache_sb, cache_nsa, state_win, state_conv, page_table, c_prompt, c_sample,
           rel_bias, norm_pre, norm_post, w_ada, b_ada, w_ffn_gate, w_ffn_up, w_ffn_down, w_in, conv_w,
           cmp_pos, w_cmp1, w_cmp2, w_branch, w_out):
    batch, seq, d = x_prompt.shape
    nseq = x_sample.shape[0]
    depth = w_in.shape[0]
    n_pages = page_table.shape[1]
    past = n_pages * PAGE_SIZE
    n_ctx = state_win.shape[2]
    assert x_sample.shape[1] == 1 and seq % Q_BLOCK == 0 and seq // L_SEL <= L_SEL
    assert cache_sb.shape[2] == PAGE_SIZE and n_ctx == WINDOW
    offs, row_part, t_part, n_row, n_all = _layout(d)
    half = D_CMP * HEAD_DIM

    q_names = ('b_q', 'c_q')
    w_row = _gather_cols(w_in, row_part, q_names)
    w_kv = _gather_cols(w_in, t_part, q_names)
    tn = 1152
    n_s = -(-n_all // tn) * tn
    w_r = w_row.astype(BF16)
    w_t = w_kv.transpose(0, 2, 1).astype(BF16)
    w_s = jnp.concatenate([w_row, w_kv, jnp.zeros((depth, d, n_s - n_all), F32)], axis=-1).astype(BF16)
    wg = w_ffn_gate.astype(BF16)
    wu = w_ffn_up.astype(BF16)
    wd = w_ffn_down.astype(BF16)
    wba = w_branch[:, 0:W_A].astype(BF16)
    hb = lambda h: w_branch[:, W_A + h * HEAD_DIM:W_A + (h + 1) * HEAD_DIM]
    wbb = jnp.concatenate([x for p in range(R_B) for x in (hb(p), hb(p + R_B))], axis=1).astype(BF16)
    wbc = w_branch[:, W_A + W_B:].astype(BF16)
    wo = w_out.astype(BF16)
    w1 = w_cmp1.astype(BF16)
    z = jnp.zeros_like(w_cmp2)
    w2p = jnp.stack([jnp.concatenate([w_cmp2, z], axis=-1), jnp.concatenate([z, w_cmp2], axis=-1)],
                    axis=2).astype(BF16)
    pos = cmp_pos.reshape(depth, 2, 2, 1, half)

    rows_c = -(-(batch + nseq) // 16) * 16
    c_all = jnp.concatenate([c_prompt, c_sample, jnp.zeros((rows_c - batch - nseq, d), F32)], axis=0)
    ada_all = _ada_all(c_all, w_ada, b_ada)
    ada_p = ada_all[:, :batch].reshape(depth, batch, 3 * N_SUB, 1, d)
    ada_s = ada_all[:, batch:batch + nseq].reshape(depth, nseq, 3 * N_SUB, d).transpose(0, 2, 1, 3)[:, None]

    tq = Q_BLOCK
    ncp_p = seq // D_CMP
    tab_toep = _bias_table(rel_bias, 2, tq, tq, base0=0, tile_step=tq, row_step=1, col_step=1)
    tab_edge = _bias_table(rel_bias, 1, tq, tq, base0=WINDOW, tile_step=0, row_step=1, col_step=1,
                           max_dist=WINDOW)
    tab_cmp = _bias_table(rel_bias, seq // tq, tq, ncp_p, base0=-(L_CMP - 1), tile_step=tq, row_step=1,
                          col_step=D_CMP)
    ncp_s = past // D_CMP
    tab_cmp_s = _bias_table(rel_bias, 1, 1, ncp_s, base0=past - (L_CMP - 1), tile_step=0, row_step=0,
                            col_step=D_CMP)[0]
    tab_win_s = _bias_table(rel_bias, 1, 1, n_ctx, base0=n_ctx, tile_step=0, row_step=0, col_step=1,
                            max_dist=WINDOW)[0]
    rel_t = rel_bias.T
    ovl_p = _overlap(ncp_p, L_SEL, G_B)
    nsl_s = -(-(past // L_SEL + 1) // LANES) * LANES
    ovl_s = _overlap(ncp_s, nsl_s, 1)
    n_top_p = min(N_SEL, -(-seq // L_SEL))
    n_top_s = min(N_SEL, -(-(past + 1) // L_SEL))

    cache_sb_t = cache_sb.transpose(0, 1, 3, 4, 5, 2)
    cache_nsa_t = cache_nsa.transpose(0, 1, 3, 4, 5, 2)
    win_state_t = state_win.transpose(0, 1, 3, 4, 5, 2)

    tm_p = 512 if seq % 512 == 0 else Q_BLOCK
    tf = w_ffn_gate.shape[-1] // 2

    y_p = x_prompt
    y_s = x_sample.reshape(1, nseq, d)
    outs = {k: [] for k in ('sb_p', 'sb_s', 'nsa_p', 'nsa_s', 'win_p', 'win_s', 'conv_p', 'conv_s')}
    for l in range(depth):
        gpre = [norm_pre[l, i][None] for i in range(N_SUB)]
        gpost = [norm_post[l, i][None] for i in range(N_SUB)]
        y_p = _ffn(y_p, ada_p[l], gpre[0], gpost[0], wg[l, 0], wu[l, 0], wd[l, 0], sub=0, tm=tm_p, tf=tf)
        projr = _proj(y_p, ada_p[l], gpre[1], w_r[l], tm=tm_p, tn=tn)
        sbt, nsat, wint = _proj_t(y_p, ada_p[l], gpre[1], w_t[l], tm=tm_p)
        hrows = jnp.stack([_flat16_t(nsat[:, 0:KV_B]), _flat16_t(nsat[:, KV_B:2 * KV_B])], axis=1)
        kvc = _compress_prompt(hrows, pos[l], w1[l], w2p[l])
        y_cmp, pen = _cmp_prompt(projr, kvc, tab_cmp, ovl_p, offs, n_top=n_top_p)
        y_sel = _nsa_flash_prompt(projr, nsat, tab_toep, pen, offs, mode='sel')
        y_win = _nsa_flash_prompt(projr, wint, tab_toep, tab_edge, offs, mode='win')
        y_c = _sb_prompt(projr, sbt, offs)
        y_p, u_tail = _merge(y_p, ada_p[l], gpost[1], projr, None, (y_cmp, y_sel, y_win), y_c, conv_w[l],
                             wba[l], wbb[l], wbc[l], wo[l], offs, tm=min(tm_p, 256), decode=False)
        y_p = _ffn(y_p, ada_p[l], gpre[2], gpost[2], wg[l, 1], wu[l, 1], wd[l, 1], sub=2, tm=tm_p, tf=tf)
        n_keep = min(WINDOW, seq)
        to_state = lambda a, k, g: a.reshape(batch, k, g, HEAD_DIM, a.shape[-1]).transpose(0, 4, 1, 2, 3)
        outs['sb_p'].append(to_state(sbt, 2, H_C))
        outs['nsa_p'].append(to_state(nsat, 4, G_B))
        outs['win_p'].append(to_state(wint[:, :, seq - n_keep:], 2, G_B))
        outs['conv_p'].append(u_tail[:, 8 - (CONV_W - 1):])
        y_s = _ffn(y_s, ada_s[l], gpre[0], gpost[0], wg[l, 0], wu[l, 0], wd[l, 0], sub=0, tm=nseq, tf=tf)
        proj_s = _proj(y_s, ada_s[l], gpre[1], w_s[l], tm=nseq, tn=tn)
        proj_r = proj_s.reshape(nseq, 1, n_s)
        s_cmp, top_idx = _dec_cmp(l, page_table, cache_nsa_t, proj_r, pos[l], w1[l], w2p[l], tab_cmp_s, ovl_s,
                                  offs, n_top=n_top_s)
        sel_table = jnp.concatenate([page_table, top_idx[:, 0, 0:n_top_s], top_idx[:, 1, 0:n_top_s]], axis=1)
        sel_table = jnp.pad(sel_table, ((0, 0), (0, -sel_table.shape[1] % LANES)))
        s_sel, s_win = _dec_selwin(l, sel_table, n_pages, cache_nsa_t, proj_r, win_state_t, tab_win_s,
                                   rel_t, offs, n_top=n_top_s)
        s_c = _dec_sb(l, page_table, cache_sb_t, proj_r, offs)
        to_rows = lambda a: a.reshape(1, nseq, a.shape[-1])
        ctx = (state_conv[l, :, 0][None], state_conv[l, :, 1][None])
        y_s, u_s = _merge(y_s, ada_s[l], gpost[1], proj_s, ctx, (to_rows(s_cmp), to_rows(s_sel), to_rows(s_win)),
                          to_rows(s_c), conv_w[l], wba[l], wbb[l], wbc[l], wo[l], offs, tm=nseq, decode=True)
        y_s = _ffn(y_s, ada_s[l], gpre[2], gpost[2], wg[l, 1], wu[l, 1], wd[l, 1], sub=2, tm=nseq, tf=tf)
        ps = proj_s[0]
        o_s, o_n, o_w = offs['sb_s'], offs['nsa_s'], offs['win_s']
        outs['sb_s'].append(ps[:, o_s:o_s + KT_SB].reshape(nseq, 1, 2, H_C, HEAD_DIM))
        outs['nsa_s'].append(ps[:, o_n:o_n + KT_NSA].reshape(nseq, 1, 4, G_B, HEAD_DIM))
        win_new = ps[:, o_w:o_w + KT_WIN].reshape(nseq, 1, 2, G_B, HEAD_DIM)
        outs['win_s'].append(jnp.concatenate([state_win[l][:, 1:], win_new], axis=1))
        outs['conv_s'].append(jnp.stack([state_conv[l, :, 1], u_s[0]], axis=1))
    st = {k: jnp.stack(v) for k, v in outs.items()}
    return (y_p, y_s.reshape(nseq, 1, d), st['sb_p'], st['sb_s'], st['nsa_p'], st['nsa_s'],
            st['win_p'], st['win_s'], st['conv_p'], st['conv_s'])
```

```python
import functools
import math

import numpy as np
import jax
import jax.numpy as jnp
from jax import lax
from jax.experimental import pallas as pl
from jax.experimental.pallas import tpu as pltpu

F32 = jnp.float32
BF16 = jnp.bfloat16

HEAD_DIM = 64
CONV_CH = 512
CONV_W = 3
H_B = 8
G_B = 2
R_B = H_B // G_B
L_CMP = 32
D_CMP = 16
L_SEL = 64
N_SEL = 16
WINDOW = 512
H_C = 8
N_BUCKETS = 32
MAX_DIST = 128
N_SUB = 3
Q_BLOCK = 128
PAGE_SIZE = 128
EPS = 1e-6
NEG = -1e30
MASKED_BELOW = -1e29
FORCE_BONUS = 1e4

W_A = CONV_CH
W_B = H_B * HEAD_DIM
W_C = H_C * HEAD_DIM
KV_B = G_B * HEAD_DIM
LANES = 128
VMEM_LIMIT = 56 * 1024 * 1024

KT_SB = 2 * W_C
KT_NSA = 4 * KV_B
KT_WIN = 2 * KV_B


def _layout(d_model):
    sizes = (W_A, W_A, W_A, W_B, KV_B, KV_B, KV_B, KV_B, KV_B, KV_B, 3 * H_B, W_C, W_C, W_C, 3 * d_model)
    starts = np.concatenate([[0], np.cumsum(sizes)]).astype(np.int64)
    names = ('a_v', 'a_b', 'a_c', 'b_q', 'b_kc', 'b_vc', 'b_ks', 'b_vs', 'b_kw', 'b_vw', 'b_g',
             'c_q', 'c_k', 'c_v', 'g_merge')
    seg = {n: (int(starts[i]), int(starts[i + 1])) for i, n in enumerate(names)}
    q0 = seg['b_q'][0]
    head = lambda h: (q0 + h * HEAD_DIM, q0 + (h + 1) * HEAD_DIM)
    bq_pairs = []
    for p in range(R_B):
        bq_pairs += [head(p), head(p + R_B)]
    row_part = [('g_merge', [seg['g_merge']], 0),
                ('conv', [seg['a_v'], seg['a_b'], seg['a_c']], 0),
                ('b_q', bq_pairs, 0),
                ('c_q', [seg['c_q']], 0),
                ('b_g', [seg['b_g']], LANES - 3 * H_B),
                ('k_rows', [seg['b_ks'], seg['b_kw']], LANES)]
    t_part = [('sb', [seg['c_k'], seg['c_v']], 0),
              ('nsa', [seg['b_kc'], seg['b_vc'], seg['b_ks'], seg['b_vs']], 0),
              ('win', [seg['b_kw'], seg['b_vw']], 0)]
    offs, o = {}, 0
    for n, rngs, padn in row_part:
        offs[n] = o
        o += sum(b - a for a, b in rngs) + padn
    n_row = o
    for n, rngs, padn in t_part:
        offs[n + '_s'] = o
        o += sum(b - a for a, b in rngs) + padn
    return offs, row_part, t_part, n_row, o


def _gather_cols(w, parts, scale_names):
    cols = []
    for n, rngs, padn in parts:
        for a, b in rngs:
            c = w[:, :, a:b]
            cols.append(c * (HEAD_DIM ** -0.5) if n in scale_names else c)
        if padn:
            cols.append(jnp.zeros(w.shape[:2] + (padn,), w.dtype))
    return jnp.concatenate(cols, axis=-1)


def _t5_thresholds():
    max_exact = N_BUCKETS // 2
    def bucket(n):
        if n < max_exact:
            return n
        b = max_exact + int(math.log(max(n, 1) / max_exact) / math.log(MAX_DIST / max_exact)
                            * (N_BUCKETS - max_exact))
        return min(b, N_BUCKETS - 1)
    thr = []
    for k in range(1, N_BUCKETS):
        n = 0
        while bucket(n) < k:
            n += 1
        thr.append(n)
    return tuple(thr)


_T5_THR = _t5_thresholds()


def _cparams(n_grid, vmem=VMEM_LIMIT):
    return pltpu.CompilerParams(dimension_semantics=("arbitrary",) * n_grid, vmem_limit_bytes=vmem)


def _dot(a, b):
    return jnp.dot(a, b, preferred_element_type=F32)


def _dot_nt(a, b):
    return lax.dot_general(a, b, (((1,), (1,)), ((), ())), preferred_element_type=F32)


def _rms(x):
    return x * lax.rsqrt(jnp.mean(x * x, axis=-1, keepdims=True) + EPS)


def _silu(x):
    return x * jax.nn.sigmoid(x)


def _lane_iota(shape):
    return lax.broadcasted_iota(jnp.int32, shape, len(shape) - 1)


def _row_iota(shape):
    return lax.broadcasted_iota(jnp.int32, shape, len(shape) - 2)


def _split_bf16(x):
    hi = x.astype(BF16)
    lo = (x - hi.astype(F32)).astype(BF16)
    return hi, lo


def _ada_kernel(c_ref, w_ref, b_ref, o_ref):
    h = _silu(c_ref[...]).astype(BF16)
    o_ref[...] = _dot(h, w_ref[...].astype(BF16)) + b_ref[...]


def _ada_all(c_all, w_ada, b_ada, tn=1536):
    depth, d, n = w_ada.shape
    rows = c_all.shape[0]
    return pl.pallas_call(
        _ada_kernel,
        grid=(depth, n // tn),
        in_specs=[pl.BlockSpec((rows, d), lambda l, j: (0, 0)),
                  pl.BlockSpec((None, d, tn), lambda l, j: (l, 0, j)),
                  pl.BlockSpec((None, 1, tn), lambda l, j: (l, 0, j))],
        out_specs=pl.BlockSpec((None, rows, tn), lambda l, j: (l, 0, j)),
        out_shape=jax.ShapeDtypeStruct((depth, rows, n), F32),
        compiler_params=_cparams(2),
        name="ada",
    )(c_all, w_ada, b_ada.reshape(depth, 1, n))


def _table_kernel(rel_ref, o_ref, *, base0, tile_step, row_step, col_step, max_dist):
    t = pl.program_id(0)
    h = pl.program_id(1)
    shape = o_ref.shape
    dist = (base0 + t * tile_step + _row_iota(shape) * row_step - _lane_iota(shape) * col_step)
    b = jnp.full(shape, rel_ref[0, h], F32)
    for k, thr in enumerate(_T5_THR):
        b = jnp.where(dist >= thr, rel_ref[k + 1, h], b)
    ok = dist >= 0
    if max_dist is not None:
        ok = ok & (dist <= max_dist)
    o_ref[...] = jnp.where(ok, b, NEG)


def _bias_table(rel_bias, n_tiles, rows, cols, *, base0, tile_step, row_step, col_step, max_dist=None,
                heads_on_lanes=False):
    n_heads = rel_bias.shape[1]
    if heads_on_lanes:
        return pl.pallas_call(
            functools.partial(_table_kernel, base0=base0, tile_step=tile_step, row_step=row_step,
                              col_step=col_step, max_dist=max_dist),
            grid=(n_tiles, n_heads),
            in_specs=[pl.BlockSpec(memory_space=pltpu.SMEM)],
            out_specs=pl.BlockSpec((None, rows, cols), lambda t, h: (t, 0, h)),
            out_shape=jax.ShapeDtypeStruct((n_tiles, rows, n_heads * cols), F32),
            compiler_params=_cparams(2),
            name="bias_table",
        )(rel_bias)
    out = pl.pallas_call(
        functools.partial(_table_kernel, base0=base0, tile_step=tile_step, row_step=row_step,
                          col_step=col_step, max_dist=max_dist),
        grid=(n_tiles, n_heads),
        in_specs=[pl.BlockSpec(memory_space=pltpu.SMEM)],
        out_specs=pl.BlockSpec((None, None, rows, cols), lambda t, h: (t, h, 0, 0)),
        out_shape=jax.ShapeDtypeStruct((n_tiles, n_heads, rows, cols), F32),
        compiler_params=_cparams(2),
        name="bias_table",
    )(rel_bias)
    return out.reshape(n_tiles, n_heads * rows, cols)


def _ffn_kernel(x_ref, ada_ref, gpre_ref, gpost_ref, wg_ref, wu_ref, wd_ref, o_ref, h_ref, acc_ref, *, sub):
    f = pl.program_id(2)

    @pl.when(f == 0)
    def _():
        h = _rms(x_ref[...]) * gpre_ref[...]
        h = h * (1.0 + ada_ref[3 * sub + 1]) + ada_ref[3 * sub]
        h_ref[...] = h.astype(BF16)
        acc_ref[...] = jnp.zeros_like(acc_ref)

    h = h_ref[...]
    a = _silu(_dot(h, wg_ref[...])) * _dot(h, wu_ref[...])
    acc_ref[...] += _dot(a.astype(BF16), wd_ref[...])

    @pl.when(f == pl.num_programs(2) - 1)
    def _():
        y = _rms(acc_ref[...]) * gpost_ref[...]
        o_ref[...] = x_ref[...] + 0.5 * (ada_ref[3 * sub + 2] * y)


def _ffn(x, ada, gpre, gpost, wg, wu, wd, *, sub, tm, tf):
    bg, t, d = x.shape
    f = wg.shape[1]
    mrows = ada.shape[2]
    return pl.pallas_call(
        functools.partial(_ffn_kernel, sub=sub),
        grid=(bg, t // tm, f // tf),
        in_specs=[pl.BlockSpec((None, tm, d), lambda b, i, j: (b, i, 0)),
                  pl.BlockSpec((None, 3 * N_SUB, mrows, d), lambda b, i, j: (b, 0, 0, 0)),
                  pl.BlockSpec((1, d), lambda b, i, j: (0, 0)),
                  pl.BlockSpec((1, d), lambda b, i, j: (0, 0)),
                  pl.BlockSpec((d, tf), lambda b, i, j: (0, j)),
                  pl.BlockSpec((d, tf), lambda b, i, j: (0, j)),
                  pl.BlockSpec((tf, d), lambda b, i, j: (j, 0))],
        out_specs=pl.BlockSpec((None, tm, d), lambda b, i, j: (b, i, 0)),
        out_shape=jax.ShapeDtypeStruct(x.shape, F32),
        scratch_shapes=[pltpu.VMEM((tm, d), BF16), pltpu.VMEM((tm, d), F32)],
        compiler_params=_cparams(3),
        name="ffn",
    )(x, ada, gpre, gpost, wg, wu, wd)


def _mod1(x_ref, ada_ref, gpre_ref):
    h = _rms(x_ref[...]) * gpre_ref[...]
    return (h * (1.0 + ada_ref[4]) + ada_ref[3]).astype(BF16)


def _proj_kernel(x_ref, ada_ref, gpre_ref, w_ref, o_ref, h_ref):
    @pl.when(pl.program_id(2) == 0)
    def _():
        h_ref[...] = _mod1(x_ref, ada_ref, gpre_ref)

    o_ref[...] = _dot(h_ref[...], w_ref[...])


def _proj(x, ada, gpre, w, *, tm, tn):
    bg, t, d = x.shape
    n = w.shape[1]
    mrows = ada.shape[2]
    return pl.pallas_call(
        _proj_kernel,
        grid=(bg, t // tm, n // tn),
        in_specs=[pl.BlockSpec((None, tm, d), lambda b, i, j: (b, i, 0)),
                  pl.BlockSpec((None, 3 * N_SUB, mrows, d), lambda b, i, j: (b, 0, 0, 0)),
                  pl.BlockSpec((1, d), lambda b, i, j: (0, 0)),
                  pl.BlockSpec((d, tn), lambda b, i, j: (0, j))],
        out_specs=pl.BlockSpec((None, tm, tn), lambda b, i, j: (b, i, j)),
        out_shape=jax.ShapeDtypeStruct((bg, t, n), F32),
        scratch_shapes=[pltpu.VMEM((tm, d), BF16)],
        compiler_params=_cparams(3),
        name="proj",
    )(x, ada, gpre, w)


def _proj_t_kernel(x_ref, ada_ref, gpre_ref, w_ref, sb_ref, nsa_ref, win_ref):
    h = _mod1(x_ref, ada_ref, gpre_ref)
    kt = _dot_nt(w_ref[...], h)
    sb_ref[...] = kt[0:KT_SB]
    nsa_ref[...] = kt[KT_SB:KT_SB + KT_NSA]
    win_ref[...] = kt[KT_SB + KT_NSA:KT_SB + KT_NSA + KT_WIN]


def _proj_t(x, ada, gpre, w_t, *, tm):
    bg, t, d = x.shape
    nf = w_t.shape[0]
    out = lambda rows: (pl.BlockSpec((None, rows, tm), lambda b, i: (b, 0, i)),
                        jax.ShapeDtypeStruct((bg, rows, t), F32))
    specs, shapes = zip(out(KT_SB), out(KT_NSA), out(KT_WIN))
    return pl.pallas_call(
        _proj_t_kernel,
        grid=(bg, t // tm),
        in_specs=[pl.BlockSpec((None, tm, d), lambda b, i: (b, i, 0)),
                  pl.BlockSpec((None, 3 * N_SUB, 1, d), lambda b, i: (b, 0, 0, 0)),
                  pl.BlockSpec((1, d), lambda b, i: (0, 0)),
                  pl.BlockSpec((nf, d), lambda b, i: (0, 0))],
        out_specs=list(specs),
        out_shape=list(shapes),
        compiler_params=_cparams(2),
        name="proj_t",
    )(x, ada, gpre, w_t)


def _compress_kernel(h_ref, pos_ref, w1_ref, w2_ref, o_ref):
    ncp = h_ref.shape[1]
    half = h_ref.shape[2]
    out = None
    for g in range(G_B):
        hg = h_ref[g]
        a = _dot((hg + pos_ref[0]).astype(BF16), w1_ref[0:half, :])
        bm = _dot((hg + pos_ref[1]).astype(BF16), w1_ref[half:2 * half, :])
        pre = a + pltpu.roll(bm, ncp - 1, 0)
        o = _dot(_silu(pre).astype(BF16), w2_ref[g])
        out = o if out is None else out + o
    o_ref[...] = out


def _compress_prompt(hrows, pos, w1, w2p):
    b, _, g, ncp, half = hrows.shape
    hid = w1.shape[2]
    return pl.pallas_call(
        _compress_kernel,
        grid=(b, 2),
        in_specs=[pl.BlockSpec((None, None, g, ncp, half), lambda i, k: (i, k, 0, 0, 0)),
                  pl.BlockSpec((None, 2, 1, half), lambda i, k: (k, 0, 0, 0)),
                  pl.BlockSpec((None, 2 * half, hid), lambda i, k: (k, 0, 0)),
                  pl.BlockSpec((None, g, hid, LANES), lambda i, k: (k, 0, 0, 0))],
        out_specs=pl.BlockSpec((None, None, ncp, LANES), lambda i, k: (i, k, 0, 0)),
        out_shape=jax.ShapeDtypeStruct((b, 2, ncp, LANES), F32),
        compiler_params=_cparams(2),
        name="compress",
    )(hrows, pos, w1, w2p)


def _stack_heads(blk_fn, tq):
    lane = _lane_iota((tq, LANES))
    lo = [jnp.where(lane < HEAD_DIM, blk_fn(p), 0.0) for p in range(R_B)]
    hi = [jnp.where(lane >= HEAD_DIM, blk_fn(p), 0.0) for p in range(R_B)]
    return jnp.concatenate(lo + hi, axis=0)


def _unstack_gated(o, gsig, branch, tq):
    lane = _lane_iota((tq, LANES))
    out = []
    for p in range(R_B):
        c_lo = 3 * p + branch
        c_hi = 3 * (p + R_B) + branch
        blk = jnp.where(lane < HEAD_DIM, o[p * tq:(p + 1) * tq], o[(p + R_B) * tq:(p + R_B + 1) * tq])
        gate = jnp.where(lane < HEAD_DIM, gsig[:, c_lo:c_lo + 1], gsig[:, c_hi:c_hi + 1])
        out.append(blk * gate)
    return out


def _topk_penalty(score_t, n_top):
    nb = score_t.shape[0]
    jrow = _row_iota(score_t.shape).astype(F32)
    x = score_t
    sel = jnp.zeros(score_t.shape, F32)
    for _ in range(n_top):
        mx = jnp.max(x, axis=0, keepdims=True)
        first = jnp.min(jnp.where(x == mx, jrow, float(nb)), axis=0, keepdims=True)
        hit = jrow == first
        sel = jnp.where(hit, 1.0, sel)
        x = jnp.where(hit, -jnp.inf, x)
    return jnp.where(sel > 0.5, 0.0, NEG)


def _cmp_kernel(q_ref, kc_ref, vc_ref, tab_ref, ovl_ref, bg_ref, o_ref, pen_ref, *, n_top):
    tq = q_ref.shape[0]
    i = pl.program_id(1)
    q8 = _stack_heads(lambda p: q_ref[:, p * LANES:(p + 1) * LANES], tq).astype(BF16)
    s = _dot_nt(q8, kc_ref[...].astype(BF16)) + tab_ref[...]
    ok = s > MASKED_BELOW
    e = jnp.where(ok, jnp.exp(s - jnp.max(s, axis=-1, keepdims=True)), 0.0)
    p = (e / jnp.maximum(jnp.sum(e, axis=-1, keepdims=True), 1e-30)).astype(BF16)
    o = _dot(p, vc_ref[...].astype(BF16))
    gsig = jax.nn.sigmoid(bg_ref[...])
    for pi, blk in enumerate(_unstack_gated(o, gsig, 0, tq)):
        o_ref[:, pi * LANES:(pi + 1) * LANES] = blk
    imp8 = _dot(p, ovl_ref[...])
    lane = _lane_iota((tq, LANES))
    g0 = imp8[0:tq] + imp8[tq:2 * tq] + imp8[2 * tq:3 * tq] + imp8[3 * tq:4 * tq]
    g1 = imp8[4 * tq:5 * tq] + imp8[5 * tq:6 * tq] + imp8[6 * tq:7 * tq] + imp8[7 * tq:8 * tq]
    imp = jnp.where(lane < L_SEL, g0, g1)
    qpos = i * tq + _row_iota((tq, LANES))
    j = lane % L_SEL
    cur = qpos // L_SEL
    forced = (j == 0) | (j == cur) | (j == cur - 1)
    score = jnp.where(j * L_SEL <= qpos, imp + FORCE_BONUS * forced.astype(F32), NEG)
    st = score.T
    pen_t = jnp.concatenate([_topk_penalty(st[g * L_SEL:(g + 1) * L_SEL], n_top) for g in range(G_B)], axis=0)
    pen_ref[...] = pen_t


def _cmp_prompt(projr, kvc, tab, ovl, offs, *, n_top, tq=Q_BLOCK):
    b, t, _ = projr.shape
    ncp = kvc.shape[2]
    qb = offs['b_q'] // W_B
    gb = offs['b_g'] // LANES
    return pl.pallas_call(
        functools.partial(_cmp_kernel, n_top=n_top),
        grid=(b, t // tq),
        in_specs=[pl.BlockSpec((None, tq, W_B), lambda bi, i: (bi, i, qb)),
                  pl.BlockSpec((None, None, ncp, LANES), lambda bi, i: (bi, 0, 0, 0)),
                  pl.BlockSpec((None, None, ncp, LANES), lambda bi, i: (bi, 1, 0, 0)),
                  pl.BlockSpec((None, H_B * tq, ncp), lambda bi, i: (i, 0, 0)),
                  pl.BlockSpec((ncp, LANES), lambda bi, i: (0, 0)),
                  pl.BlockSpec((None, tq, LANES), lambda bi, i: (bi, i, gb))],
        out_specs=[pl.BlockSpec((None, tq, W_B), lambda bi, i: (bi, i, 0)),
                   pl.BlockSpec((None, LANES, tq), lambda bi, i: (bi, 0, i))],
        out_shape=[jax.ShapeDtypeStruct((b, t, W_B), F32), jax.ShapeDtypeStruct((b, LANES, t), F32)],
        compiler_params=_cparams(2),
        name="nsa_cmp",
    )(projr, kvc, kvc, tab, ovl, projr)


def _nsa_flash_kernel(q_ref, k_ref, v_ref, tab_ref, pen_ref, bg_ref, o_ref, qa_ref, m_ref, l_ref, acc_ref, *,
                      mode, branch):
    tq = q_ref.shape[0]
    tk = 2 * tq
    i = pl.program_id(1)
    jd = i // 2
    par = i % 2

    row = _row_iota((LANES, tq))
    for p in range(R_B):
        qt = q_ref[:, p * LANES:(p + 1) * LANES].T
        qa_ref[0:LANES, p * tq:(p + 1) * tq] = jnp.where(row < HEAD_DIM, qt, 0.0).astype(BF16)
        qa_ref[0:LANES, (p + R_B) * tq:(p + R_B + 1) * tq] = jnp.where(row >= HEAD_DIM, qt, 0.0).astype(BF16)
    if mode == 'sel':
        pen = pen_ref[...]
        lo = jnp.where(row < L_SEL, pen, 0.0).astype(BF16)
        hi = jnp.where(row >= L_SEL, pen, 0.0).astype(BF16)
        qa_ref[LANES:2 * LANES, :] = jnp.concatenate([lo] * R_B + [hi] * R_B, axis=1)
    m_ref[...] = jnp.full_like(m_ref, NEG)
    l_ref[...] = jnp.zeros_like(l_ref)
    acc_ref[...] = jnp.zeros_like(acc_ref)

    def step(j, bias):
        k0 = pl.multiple_of(j * tk, tk)
        kt = k_ref[pl.ds(k0, tk), :].astype(BF16)
        if mode == 'sel':
            blk = (tk // L_SEL) * j + _row_iota((tk, LANES)) // L_SEL
            et = jnp.where(blk == _lane_iota((tk, LANES)) % L_SEL, 1.0, 0.0).astype(BF16)
            kt = jnp.concatenate([kt, et], axis=1)
        s = _dot(kt, qa_ref[...]) + bias
        m_old = m_ref[...]
        m_new = jnp.maximum(m_old, jnp.max(s, axis=0, keepdims=True))
        alpha = jnp.exp(m_old - m_new)
        p = jnp.where(s > MASKED_BELOW, jnp.exp(s - m_new), 0.0)
        l_ref[...] = alpha * l_ref[...] + jnp.sum(p, axis=0, keepdims=True)
        acc_ref[...] = alpha * acc_ref[...] + _dot(v_ref[:, pl.ds(k0, tk)].astype(BF16), p.astype(BF16))
        m_ref[...] = m_new

    n_near = tab_ref.shape[0] // 2
    if mode == 'sel':
        far_bias = tab_ref[3, 0:1, :]

        def far_body(j, c):
            step(j, far_bias)
            return c

        lax.fori_loop(0, jnp.maximum(jd - (n_near - 1), 0), far_body, 0)
    for dj in range(n_near - 1, 0, -1):
        @pl.when(jd >= dj)
        def _(dj=dj):
            step(jd - dj, tab_ref[par + 2 * dj])

    step(jd, tab_ref[par])

    ot = acc_ref[...] / jnp.maximum(l_ref[...], 1e-30)
    o = jnp.concatenate([ot[:, h * tq:(h + 1) * tq].T for h in range(H_B)], axis=0)
    gsig = jax.nn.sigmoid(bg_ref[...])
    for pi, blk in enumerate(_unstack_gated(o, gsig, branch, tq)):
        o_ref[:, pi * LANES:(pi + 1) * LANES] = blk


def _nsa_flash_prompt(projr, vt, tab, pen_t, offs, *, mode, tq=Q_BLOCK):
    b, t, _ = projr.shape
    assert t % (2 * tq) == 0 and WINDOW == 4 * tq and tq == LANES
    qb = offs['b_q'] // W_B
    gb = offs['b_g'] // LANES
    kb = offs['k_rows'] // LANES
    if mode == 'sel':
        vb, branch, qa_rows = 3, 1, 2 * LANES
    else:
        kb, vb, branch, qa_rows = kb + 1, 1, 2, LANES
    return pl.pallas_call(
        functools.partial(_nsa_flash_kernel, mode=mode, branch=branch),
        grid=(b, t // tq),
        in_specs=[pl.BlockSpec((None, tq, W_B), lambda bi, i: (bi, i, qb)),
                  pl.BlockSpec((None, t, KV_B), lambda bi, i: (bi, 0, kb)),
                  pl.BlockSpec((None, KV_B, t), lambda bi, i: (bi, vb, 0)),
                  pl.BlockSpec(tab.shape, lambda bi, i: (0, 0, 0)),
                  pl.BlockSpec((None, LANES, tq), lambda bi, i: (bi, 0, i)),
                  pl.BlockSpec((None, tq, LANES), lambda bi, i: (bi, i, gb))],
        out_specs=pl.BlockSpec((None, tq, W_B), lambda bi, i: (bi, i, 0)),
        out_shape=jax.ShapeDtypeStruct((b, t, W_B), F32),
        scratch_shapes=[pltpu.VMEM((qa_rows, H_B * tq), BF16),
                        pltpu.VMEM((1, H_B * tq), F32),
                        pltpu.VMEM((1, H_B * tq), F32),
                        pltpu.VMEM((KV_B, H_B * tq), F32)],
        compiler_params=_cparams(2),
        name="nsa_" + mode,
    )(projr, projr, vt, tab, pen_t, projr)


def _sb_kernel(q_ref, k_ref, v_ref, o_ref, qs_ref, carry_ref, acc_ref):
    tq = q_ref.shape[0]
    tk = tq
    i = pl.program_id(2)
    lane = _lane_iota((tq, LANES))
    q = q_ref[...]
    qs_ref[...] = jnp.concatenate([jnp.where(lane < HEAD_DIM, q, 0.0),
                                   jnp.where(lane >= HEAD_DIM, q, 0.0)], axis=0).astype(BF16)
    carry_ref[...] = jnp.zeros_like(carry_ref)
    acc_ref[...] = jnp.zeros_like(acc_ref)
    tri = jnp.where(_row_iota((tk, tk)) > _lane_iota((tk, tk)), 1.0, 0.0).astype(BF16)

    def step(j, diagonal):
        k0 = pl.multiple_of(j * tk, tk)
        z = _dot(qs_ref[...], k_ref[:, pl.ds(k0, tk)].astype(BF16))
        t = jnp.log(1.0 + jnp.exp(-jnp.abs(z)))
        log_1mb = -(jnp.maximum(z, 0.0) + t)
        log_b = jnp.minimum(z, 0.0) - t
        if diagonal:
            rr = _row_iota((2 * tq, tk))
            causal = _lane_iota((2 * tq, tk)) < jnp.where(rr >= tq, rr - tq, rr)
            log_1mb = jnp.where(causal, log_1mb, 0.0)
        hi, lo = _split_bf16(log_1mb)
        after = _dot(hi, tri) + _dot(lo, tri) + carry_ref[...]
        a = jnp.exp(log_b + after)
        if diagonal:
            a = jnp.where(causal, a, 0.0)
        carry_ref[...] += jnp.sum(log_1mb, axis=-1, keepdims=True)
        acc_ref[...] += _dot_nt(a.astype(BF16), v_ref[:, pl.ds(k0, tk)].astype(BF16))

    step(i, True)

    def body(n, c):
        step(i - 1 - n, False)
        return c

    lax.fori_loop(0, i, body, 0)
    acc = acc_ref[...]
    o_ref[...] = jnp.where(lane < HEAD_DIM, acc[0:tq], acc[tq:2 * tq])


def _sb_prompt(projr, sbt, offs, *, tq=2 * Q_BLOCK):
    b, t, _ = projr.shape
    n_pairs = W_C // LANES
    qb = offs['c_q'] // LANES
    return pl.pallas_call(
        _sb_kernel,
        grid=(b, n_pairs, t // tq),
        in_specs=[pl.BlockSpec((None, tq, LANES), lambda bi, p, i: (bi, i, qb + p)),
                  pl.BlockSpec((None, LANES, t), lambda bi, p, i: (bi, p, 0)),
                  pl.BlockSpec((None, LANES, t), lambda bi, p, i: (bi, n_pairs + p, 0))],
        out_specs=pl.BlockSpec((None, tq, LANES), lambda bi, p, i: (bi, i, p)),
        out_shape=jax.ShapeDtypeStruct((b, t, W_C), F32),
        scratch_shapes=[pltpu.VMEM((2 * tq, LANES), BF16),
                        pltpu.VMEM((2 * tq, 1), F32),
                        pltpu.VMEM((2 * tq, LANES), F32)],
        compiler_params=_cparams(3),
        name="sb",
    )(projr, sbt, sbt)


def _merge_kernel(*refs, decode):
    if decode:
        (x_ref, ada_ref, gpost_ref, conv_ref, s0_ref, s1_ref, gm0_ref, gm1_ref, gm2_ref, yb0_ref, yb1_ref,
         yb2_ref, yc_ref, cw_ref, wba_ref, wbb_ref, wbc_ref, wo_ref, o_ref, u_ref) = refs
    else:
        (x_ref, ada_ref, gpost_ref, conv_ref, halo_ref, gm0_ref, gm1_ref, gm2_ref, yb0_ref, yb1_ref, yb2_ref,
         yc_ref, cw_ref, wba_ref, wbb_ref, wbc_ref, wo_ref, o_ref, u_ref) = refs
    tm = x_ref.shape[0]
    u = conv_ref[:, 2 * W_A:3 * W_A] * conv_ref[:, 0:W_A]
    if decode:
        um2 = s0_ref[...]
        um1 = s1_ref[...]
        u_ref[...] = u
    else:
        i = pl.program_id(1)
        keep = (i > 0).astype(F32)
        hu = (halo_ref[:, 2 * W_A:3 * W_A] * halo_ref[:, 0:W_A]) * keep
        row = _row_iota((tm, W_A))
        um1 = jnp.where(row >= 1, pltpu.roll(u, 1, 0), hu[7:8])
        um2 = jnp.where(row >= 2, pltpu.roll(u, 2, 0), jnp.where(row == 1, hu[7:8], hu[6:7]))

        @pl.when(i == pl.num_programs(1) - 1)
        def _():
            u_ref[...] = u[tm - 8:tm]

    y_a = conv_ref[:, W_A:2 * W_A] * (cw_ref[0:1] * um2 + cw_ref[1:2] * um1 + cw_ref[2:3] * u)
    y_b = yb0_ref[...] + yb1_ref[...] + yb2_ref[...]
    merged = (jax.nn.sigmoid(gm0_ref[...]) * _dot(y_a.astype(BF16), wba_ref[...])
              + jax.nn.sigmoid(gm1_ref[...]) * _dot(y_b.astype(BF16), wbb_ref[...])
              + jax.nn.sigmoid(gm2_ref[...]) * _dot(yc_ref[...].astype(BF16), wbc_ref[...]))
    out = _dot(merged.astype(BF16), wo_ref[...])
    o_ref[...] = x_ref[...] + ada_ref[5] * (_rms(out) * gpost_ref[...])


def _merge(x, ada, gpost, projr, conv_state, yb, yc, cw, wba, wbb, wbc, wo, offs, *, tm, decode):
    bg, t, d = x.shape
    mrows = ada.shape[2]
    cb = offs['conv'] // (3 * W_A)
    gmb = offs['g_merge'] // d
    row_spec = lambda w, cblk: pl.BlockSpec((None, tm, w), lambda b, i: (b, i, cblk))
    full = lambda a: pl.BlockSpec(a.shape, lambda b, i: (0,) * a.ndim)
    if decode:
        ctx_specs = [row_spec(W_A, 0), row_spec(W_A, 0)]
        ctx = list(conv_state)
        u_shape, u_spec = (bg, t, W_A), row_spec(W_A, 0)
    else:
        r8 = tm // 8
        ctx_specs = [pl.BlockSpec((None, 8, 3 * W_A), lambda b, i: (b, jnp.maximum(i * r8 - 1, 0), cb))]
        ctx = [projr]
        u_shape, u_spec = (bg, 8, W_A), pl.BlockSpec((None, 8, W_A), lambda b, i: (b, 0, 0))
    out, u = pl.pallas_call(
        functools.partial(_merge_kernel, decode=decode),
        grid=(bg, t // tm),
        in_specs=[row_spec(d, 0),
                  pl.BlockSpec((None, 3 * N_SUB, mrows, d), lambda b, i: (b, 0, 0, 0)),
                  pl.BlockSpec((1, d), lambda b, i: (0, 0)),
                  row_spec(3 * W_A, cb)] + ctx_specs + [
                  row_spec(d, gmb), row_spec(d, gmb + 1), row_spec(d, gmb + 2),
                  row_spec(W_B, 0), row_spec(W_B, 0), row_spec(W_B, 0), row_spec(W_C, 0),
                  full(cw), full(wba), full(wbb), full(wbc), full(wo)],
        out_specs=[row_spec(d, 0), u_spec],
        out_shape=[jax.ShapeDtypeStruct(x.shape, F32), jax.ShapeDtypeStruct(u_shape, F32)],
        compiler_params=_cparams(2),
        name="merge",
    )(x, ada, gpost, projr, *ctx, projr, projr, projr, yb[0], yb[1], yb[2], yc, cw, wba, wbb, wbc, wo)
    return out, u


def _dec_cmp_kernel(pt_ref, *refs, n_pages_step, n_top, past):
    del pt_ref
    pages = refs[:n_pages_step]
    (q_ref, pos_ref, w1_ref, w2_ref, tab_ref, ovl_ref, bg_ref, o_ref, idx_ref, h_ref) = refs[n_pages_step:]
    c = pl.program_id(1)
    lane = _lane_iota((8, LANES))
    rpp = PAGE_SIZE // D_CMP
    ii = _row_iota((PAGE_SIZE, PAGE_SIZE))
    perm = jnp.where(_lane_iota((PAGE_SIZE, PAGE_SIZE)) == D_CMP * (ii % rpp) + ii // rpp, 1.0, 0.0).astype(BF16)
    for k in range(n_pages_step):
        row0 = pl.multiple_of((c * n_pages_step + k) * rpp, rpp)
        xt = pages[k][...].reshape(2 * KV_B, PAGE_SIZE).astype(BF16)
        rows = _dot_nt(perm, xt)
        for m in range(D_CMP // 2):
            ev = rows[2 * m * rpp:(2 * m + 1) * rpp]
            od = rows[(2 * m + 1) * rpp:(2 * m + 2) * rpp]
            for kind in range(2):
                e = ev[:, kind * LANES:(kind + 1) * LANES]
                o = od[:, kind * LANES:(kind + 1) * LANES]
                h_ref[kind, 0, pl.ds(row0, rpp), m * LANES:(m + 1) * LANES] = (
                    jnp.where(lane < HEAD_DIM, e, pltpu.roll(o, HEAD_DIM, 1)))
                h_ref[kind, 1, pl.ds(row0, rpp), m * LANES:(m + 1) * LANES] = (
                    jnp.where(lane < HEAD_DIM, pltpu.roll(e, HEAD_DIM, 1), o))

    @pl.when(c == pl.num_programs(1) - 1)
    def _():
        ncp = h_ref.shape[2]
        half = h_ref.shape[3]
        kv = []
        for kind in range(2):
            out = None
            for g in range(G_B):
                hg = h_ref[kind, g]
                a = _dot((hg + pos_ref[kind, 0]).astype(BF16), w1_ref[kind, 0:half, :])
                bm = _dot((hg + pos_ref[kind, 1]).astype(BF16), w1_ref[kind, half:2 * half, :])
                pre = a + pltpu.roll(bm, ncp - 1, 0)
                o = _dot(_silu(pre).astype(BF16), w2_ref[kind, g])
                out = o if out is None else out + o
            kv.append(out.astype(BF16))
        q8 = _stack_heads(lambda p: q_ref[:, p * LANES:(p + 1) * LANES], 1).astype(BF16)
        s = _dot_nt(q8, kv[0]) + tab_ref[...]
        ok = s > MASKED_BELOW
        e = jnp.where(ok, jnp.exp(s - jnp.max(s, axis=-1, keepdims=True)), 0.0)
        p = (e / jnp.maximum(jnp.sum(e, axis=-1, keepdims=True), 1e-30)).astype(BF16)
        o = _dot(p, kv[1])
        gsig = jax.nn.sigmoid(bg_ref[...])
        for pi, blk in enumerate(_unstack_gated(o, gsig, 0, 1)):
            o_ref[:, pi * LANES:(pi + 1) * LANES] = blk
        imp8 = _dot(p, ovl_ref[...])
        imp = jnp.concatenate([imp8[0:1] + imp8[1:2] + imp8[2:3] + imp8[3:4],
                               imp8[4:5] + imp8[5:6] + imp8[6:7] + imp8[7:8]], axis=0)
        nbl = imp.shape[1]
        j = _lane_iota(imp.shape)
        cur = past // L_SEL
        forced = (j == 0) | (j == cur) | (j == cur - 1)
        x = jnp.where(j * L_SEL <= past, imp + FORCE_BONUS * forced.astype(F32), NEG)
        jf = j.astype(F32)
        slot = _lane_iota((G_B, LANES))
        idx = jnp.zeros((G_B, LANES), F32)
        for it in range(n_top):
            mx = jnp.max(x, axis=-1, keepdims=True)
            first = jnp.min(jnp.where(x == mx, jf, float(nbl)), axis=-1, keepdims=True)
            idx = jnp.where(slot == it, first, idx)
            x = jnp.where(jf == first, -jnp.inf, x)
        idx_ref[...] = jnp.zeros(idx_ref.shape, jnp.int32)
        idx_ref[0:G_B, :] = idx.astype(jnp.int32)


def _dec_cmp(layer, page_table, cache_nsa_t, proj_s, pos, w1, w2p, tab, ovl, offs, *, n_top, n_pages_step=8):
    nseq, n_pages = page_table.shape
    past = n_pages * PAGE_SIZE
    ncp = past // D_CMP
    half = D_CMP * HEAD_DIM
    qb = offs['b_q'] // W_B
    gb = offs['b_g'] // LANES
    page_specs = [pl.BlockSpec((None, None, 2, G_B, HEAD_DIM, PAGE_SIZE),
                               functools.partial(
                                   lambda b, c, pt, k: (layer, pt[b, c * n_pages_step + k], 0, 0, 0, 0), k=k))
                  for k in range(n_pages_step)]
    full = lambda a: pl.BlockSpec(a.shape, lambda b, c, pt: (0,) * a.ndim)
    grid_spec = pltpu.PrefetchScalarGridSpec(
        num_scalar_prefetch=1,
        grid=(nseq, n_pages // n_pages_step),
        in_specs=page_specs + [pl.BlockSpec((None, 1, W_B), lambda b, c, pt: (b, 0, qb)),
                               full(pos), full(w1), full(w2p), full(tab), full(ovl),
                               pl.BlockSpec((None, 1, LANES), lambda b, c, pt: (b, 0, gb))],
        out_specs=[pl.BlockSpec((None, 1, W_B), lambda b, c, pt: (b, 0, 0)),
                   pl.BlockSpec((None, 8, LANES), lambda b, c, pt: (b, 0, 0))],
        scratch_shapes=[pltpu.VMEM((2, G_B, ncp, half), F32)],
    )
    return pl.pallas_call(
        functools.partial(_dec_cmp_kernel, n_pages_step=n_pages_step, n_top=n_top, past=past),
        grid_spec=grid_spec,
        out_shape=[jax.ShapeDtypeStruct((nseq, 1, W_B), F32), jax.ShapeDtypeStruct((nseq, 8, LANES), jnp.int32)],
        compiler_params=_cparams(2),
        name="dec_cmp",
    )(page_table, *([cache_nsa_t] * n_pages_step), proj_s, pos, w1, w2p, tab, ovl, proj_s)


def _dec_selwin_kernel(sp_ref, *refs, n_top, past, n_pages):
    n_blk = G_B * n_top
    blocks = refs[:n_blk]
    (q_ref, ksn_ref, vsn_ref, wst_ref, kwn_ref, vwn_ref, tabw_ref, relt_ref, bg_ref,
     osel_ref, owin_ref, kc_ref, vc_ref) = refs[n_blk:]
    b = pl.program_id(0)
    n_past_blk = past // L_SEL
    halves = PAGE_SIZE // L_SEL
    q8f = _stack_heads(lambda p: q_ref[:, p * LANES:(p + 1) * LANES], 1)
    q8 = q8f.astype(BF16)
    q8r = q8.astype(F32)
    gsig = jax.nn.sigmoid(bg_ref[...])
    row1 = _row_iota((H_B, 1))

    for n in range(n_blk):
        kc_ref[:, n * PAGE_SIZE:(n + 1) * PAGE_SIZE] = blocks[n][0].reshape(KV_B, PAGE_SIZE).astype(BF16)
        vc_ref[:, n * PAGE_SIZE:(n + 1) * PAGE_SIZE] = blocks[n][1].reshape(KV_B, PAGE_SIZE).astype(BF16)
    nk = n_blk * PAGE_SIZE
    s = _dot(q8, kc_ref[...])
    lane = _lane_iota((H_B, nk))
    slot = lane // PAGE_SIZE
    within = lane % PAGE_SIZE
    blk_of = jnp.zeros((H_B, nk), jnp.int32)
    has_new = [jnp.int32(0)] * G_B
    for n in range(n_blk):
        bi = sp_ref[b, n_pages + n]
        blk_of = jnp.where(slot == n, bi, blk_of)
        has_new[n // n_top] = jnp.maximum(has_new[n // n_top], (bi == n_past_blk).astype(jnp.int32))
    kpos = (blk_of // halves) * PAGE_SIZE + within
    dist = past - kpos
    row = _row_iota((H_B, nk))
    ok = ((within // L_SEL == blk_of % halves) & (blk_of < n_past_blk)
          & (row // R_B == slot // n_top))
    bias = jnp.broadcast_to(relt_ref[:, 0:1], (H_B, nk))
    for k, thr in enumerate(_T5_THR):
        bias = jnp.where(dist >= thr, relt_ref[:, k + 1:k + 2], bias)
    s = jnp.where(ok, s + bias, NEG)
    new_ok = jnp.where(row1 < R_B, has_new[0], has_new[1]) > 0
    s_new = jnp.sum(q8r * ksn_ref[...].astype(BF16).astype(F32), axis=-1, keepdims=True)
    s_new = jnp.where(new_ok, s_new + relt_ref[:, 0:1], NEG)
    m = jnp.maximum(jnp.max(s, axis=-1, keepdims=True), s_new)
    e = jnp.where(ok, jnp.exp(s - m), 0.0)
    e_new = jnp.where(new_ok, jnp.exp(s_new - m), 0.0)
    den = jnp.maximum(jnp.sum(e, axis=-1, keepdims=True) + e_new, 1e-30)
    o = (_dot_nt(e.astype(BF16), vc_ref[...]) + e_new * vsn_ref[...].astype(BF16).astype(F32)) / den
    for pi, blk in enumerate(_unstack_gated(o, gsig, 1, 1)):
        osel_ref[:, pi * LANES:(pi + 1) * LANES] = blk

    n_ctx = wst_ref.shape[-1]
    sw = _dot(q8, wst_ref[0].reshape(KV_B, n_ctx).astype(BF16)) + tabw_ref[...]
    okw = sw > MASKED_BELOW
    sw_new = jnp.sum(q8r * kwn_ref[...].astype(BF16).astype(F32), axis=-1, keepdims=True) + relt_ref[:, 0:1]
    mw = jnp.maximum(jnp.max(sw, axis=-1, keepdims=True), sw_new)
    ew = jnp.where(okw, jnp.exp(sw - mw), 0.0)
    ew_new = jnp.exp(sw_new - mw)
    denw = jnp.maximum(jnp.sum(ew, axis=-1, keepdims=True) + ew_new, 1e-30)
    ow = (_dot_nt(ew.astype(BF16), wst_ref[1].reshape(KV_B, n_ctx).astype(BF16))
          + ew_new * vwn_ref[...].astype(BF16).astype(F32)) / denw
    for pi, blk in enumerate(_unstack_gated(ow, gsig, 2, 1)):
        owin_ref[:, pi * LANES:(pi + 1) * LANES] = blk


def _dec_selwin(layer, sel_table, n_pages, cache_nsa_t, proj_s, win_state_t, tabw, rel_t, offs, *, n_top):
    nseq = sel_table.shape[0]
    past = n_pages * PAGE_SIZE
    n_blk = G_B * n_top
    n_past_blk = past // L_SEL
    halves = PAGE_SIZE // L_SEL
    n_ctx = win_state_t.shape[-1]
    qb = offs['b_q'] // W_B
    gb = offs['b_g'] // LANES
    nb = offs['nsa_s'] // LANES
    wb = offs['win_s'] // LANES

    def blk_map(b, sp, n):
        bi = jnp.clip(sp[b, n_pages + n], 0, n_past_blk - 1)
        return (layer, sp[b, bi // halves], 1, 0, 0, 0)

    blk_specs = [pl.BlockSpec((None, None, 2, G_B, HEAD_DIM, PAGE_SIZE), functools.partial(blk_map, n=n))
                 for n in range(n_blk)]
    row = lambda cblk: pl.BlockSpec((None, 1, LANES), lambda b, sp: (b, 0, cblk))
    full = lambda a: pl.BlockSpec(a.shape, lambda b, sp: (0,) * a.ndim)
    grid_spec = pltpu.PrefetchScalarGridSpec(
        num_scalar_prefetch=1,
        grid=(nseq,),
        in_specs=blk_specs + [pl.BlockSpec((None, 1, W_B), lambda b, sp: (b, 0, qb)),
                              row(nb + 2), row(nb + 3),
                              pl.BlockSpec((None, None, 2, G_B, HEAD_DIM, n_ctx),
                                           lambda b, sp: (layer, b, 0, 0, 0, 0)),
                              row(wb), row(wb + 1), full(tabw), full(rel_t), row(gb)],
        out_specs=[pl.BlockSpec((None, 1, W_B), lambda b, sp: (b, 0, 0)),
                   pl.BlockSpec((None, 1, W_B), lambda b, sp: (b, 0, 0))],
        scratch_shapes=[pltpu.VMEM((KV_B, n_blk * PAGE_SIZE), BF16), pltpu.VMEM((KV_B, n_blk * PAGE_SIZE), BF16)],
    )
    return pl.pallas_call(
        functools.partial(_dec_selwin_kernel, n_top=n_top, past=past, n_pages=n_pages),
        grid_spec=grid_spec,
        out_shape=[jax.ShapeDtypeStruct((nseq, 1, W_B), F32)] * 2,
        compiler_params=_cparams(1),
        name="dec_selwin",
    )(sel_table, *([cache_nsa_t] * n_blk), proj_s, proj_s, proj_s, win_state_t, proj_s, proj_s,
      tabw, rel_t, proj_s)


def _dec_sb_kernel(pt_ref, *refs, n_pages_step):
    del pt_ref
    pages = refs[:n_pages_step]
    q_ref, o_ref, qcol_ref, carry_ref, acc_ref = refs[n_pages_step:]
    c = pl.program_id(1)
    ps = PAGE_SIZE

    @pl.when(c == 0)
    def _():
        qt = jnp.broadcast_to(q_ref[...], (LANES, W_C)).T
        qcol_ref[...] = qt.reshape(H_C, HEAD_DIM, LANES)
        carry_ref[...] = jnp.zeros_like(carry_ref)
        acc_ref[...] = jnp.zeros_like(acc_ref)

    tri = jnp.where(_row_iota((ps, ps)) > _lane_iota((ps, ps)), 1.0, 0.0).astype(BF16)
    hrow = _row_iota((H_C, ps))
    for k in range(n_pages_step):
        z = jnp.zeros((H_C, ps), F32)
        for h in range(H_C):
            zh = jnp.sum(qcol_ref[h] * pages[k][0, h], axis=0, keepdims=True)
            z = jnp.where(hrow == h, zh, z)
        t = jnp.log1p(jnp.exp(-jnp.abs(z)))
        log_1mb = -(jnp.maximum(z, 0.0) + t)
        log_b = jnp.minimum(z, 0.0) - t
        hi, lo = _split_bf16(log_1mb)
        after = _dot(hi, tri) + _dot(lo, tri) + carry_ref[...]
        a = jnp.exp(log_b + after)
        carry_ref[...] += jnp.sum(log_1mb, axis=-1, keepdims=True)
        for h in range(H_C):
            acc_ref[h] += a[h:h + 1, :] * pages[k][1, h]

    @pl.when(c == pl.num_programs(1) - 1)
    def _():
        acc = acc_ref[...].reshape(W_C, LANES)
        ones = jnp.ones((8, LANES), BF16)
        hi = acc.astype(BF16)
        r1 = acc - hi.astype(F32)
        mid = r1.astype(BF16)
        lo = (r1 - mid.astype(F32)).astype(BF16)
        tot = _dot_nt(ones, hi) + _dot_nt(ones, mid) + _dot_nt(ones, lo)
        o_ref[...] = tot[0:1]


def _dec_sb(layer, page_table, cache_sb_t, proj_s, offs, *, n_pages_step=8):
    nseq, n_pages = page_table.shape
    qb = offs['c_q'] // W_C
    page_specs = [pl.BlockSpec((None, None, 2, H_C, HEAD_DIM, PAGE_SIZE),
                               functools.partial(
                                   lambda b, c, pt, k: (layer, pt[b, n_pages - 1 - (c * n_pages_step + k)],
                                                        0, 0, 0, 0), k=k))
                  for k in range(n_pages_step)]
    grid_spec = pltpu.PrefetchScalarGridSpec(
        num_scalar_prefetch=1,
        grid=(nseq, n_pages // n_pages_step),
        in_specs=page_specs + [pl.BlockSpec((None, 1, W_C), lambda b, c, pt: (b, 0, qb))],
        out_specs=pl.BlockSpec((None, 1, W_C), lambda b, c, pt: (b, 0, 0)),
        scratch_shapes=[pltpu.VMEM((H_C, HEAD_DIM, LANES), F32), pltpu.VMEM((H_C, 1), F32),
                        pltpu.VMEM((H_C, HEAD_DIM, LANES), F32)],
    )
    return pl.pallas_call(
        functools.partial(_dec_sb_kernel, n_pages_step=n_pages_step),
        grid_spec=grid_spec,
        out_shape=jax.ShapeDtypeStruct((nseq, 1, W_C), F32),
        compiler_params=_cparams(2),
        name="dec_sb",
    )(page_table, *([cache_sb_t] * n_pages_step), proj_s)


def _overlap(ncp, n_lanes, reps):
    c0 = np.arange(ncp)[:, None] * D_CMP
    s0 = np.arange(n_lanes)[None, :] * L_SEL
    ov = ((c0 < s0 + L_SEL) & (c0 + L_CMP > s0)).astype(np.float32)
    return jnp.asarray(np.tile(ov, (1, reps)), BF16)


def _flat16_t(kt):
    b, _, t = kt.shape
    r = kt.reshape(b, G_B, HEAD_DIM, t // D_CMP, D_CMP).transpose(0, 1, 3, 4, 2)
    return r.reshape(b, G_B, t // D_CMP, D_CMP * HEAD_DIM)


def kernel(x_prompt, x_sample, cache_sb, cache_nsa, state_win, state_conv, page_table, c_prompt, c_sample,
           rel_bias, norm_pre, norm_post, w_ada, b_ada, w_ffn_gate, w_ffn_up, w_ffn_down, w_in, conv_w,
           cmp_pos, w_cmp1, w_cmp2, w_branch, w_out):
    batch, seq, d = x_prompt.shape
    nseq = x_sample.shape[0]
    depth = w_in.shape[0]
    n_pages = page_table.shape[1]
    past = n_pages * PAGE_SIZE
    n_ctx = state_win.shape[2]
    assert x_sample.shape[1] == 1 and seq % Q_BLOCK == 0 and seq // L_SEL <= L_SEL
    assert cache_sb.shape[2] == PAGE_SIZE and n_ctx == WINDOW
    offs, row_part, t_part, n_row, n_all = _layout(d)
    half = D_CMP * HEAD_DIM

    q_names = ('b_q', 'c_q')
    w_row = _gather_cols(w_in, row_part, q_names)
    w_kv = _gather_cols(w_in, t_part, q_names)
    tn = 1024
    n_s = -(-n_all // tn) * tn
    w_r = w_row.astype(BF16)
    w_t = w_kv.transpose(0, 2, 1).astype(BF16)
    w_s = jnp.concatenate([w_row, w_kv, jnp.zeros((depth, d, n_s - n_all), F32)], axis=-1).astype(BF16)
    wg = w_ffn_gate.astype(BF16)
    wu = w_ffn_up.astype(BF16)
    wd = w_ffn_down.astype(BF16)
    wba = w_branch[:, 0:W_A].astype(BF16)
    hb = lambda h: w_branch[:, W_A + h * HEAD_DIM:W_A + (h + 1) * HEAD_DIM]
    wbb = jnp.concatenate([x for p in range(R_B) for x in (hb(p), hb(p + R_B))], axis=1).astype(BF16)
    wbc = w_branch[:, W_A + W_B:].astype(BF16)
    wo = w_out.astype(BF16)
    w1 = w_cmp1.astype(BF16)
    z = jnp.zeros_like(w_cmp2)
    w2p = jnp.stack([jnp.concatenate([w_cmp2, z], axis=-1), jnp.concatenate([z, w_cmp2], axis=-1)],
                    axis=2).astype(BF16)
    pos = cmp_pos.reshape(depth, 2, 2, 1, half)

    rows_c = -(-(batch + nseq) // 16) * 16
    c_all = jnp.concatenate([c_prompt, c_sample, jnp.zeros((rows_c - batch - nseq, d), F32)], axis=0)
    ada_all = _ada_all(c_all, w_ada, b_ada)
    ada_p = ada_all[:, :batch].reshape(depth, batch, 3 * N_SUB, 1, d)
    ada_s = ada_all[:, batch:batch + nseq].reshape(depth, nseq, 3 * N_SUB, d).transpose(0, 2, 1, 3)[:, None]

    tq = Q_BLOCK
    ncp_p = seq // D_CMP
    tab_sel = _bias_table(rel_bias, 4, 2 * tq, tq, base0=0, tile_step=tq, row_step=-1, col_step=-1,
                          heads_on_lanes=True)
    tab_win = _bias_table(rel_bias, 6, 2 * tq, tq, base0=0, tile_step=tq, row_step=-1, col_step=-1,
                          max_dist=WINDOW, heads_on_lanes=True)
    tab_cmp = _bias_table(rel_bias, seq // tq, tq, ncp_p, base0=-(L_CMP - 1), tile_step=tq, row_step=1,
                          col_step=D_CMP)
    ncp_s = past // D_CMP
    tab_cmp_s = _bias_table(rel_bias, 1, 1, ncp_s, base0=past - (L_CMP - 1), tile_step=0, row_step=0,
                            col_step=D_CMP)[0]
    tab_win_s = _bias_table(rel_bias, 1, 1, n_ctx, base0=n_ctx, tile_step=0, row_step=0, col_step=1,
                            max_dist=WINDOW)[0]
    rel_t = rel_bias.T
    ovl_p = _overlap(ncp_p, L_SEL, G_B)
    nsl_s = -(-(past // L_SEL + 1) // LANES) * LANES
    ovl_s = _overlap(ncp_s, nsl_s, 1)
    n_top_p = min(N_SEL, -(-seq // L_SEL))
    n_top_s = min(N_SEL, -(-(past + 1) // L_SEL))

    cache_sb_t = cache_sb.transpose(0, 1, 3, 4, 5, 2)
    cache_nsa_t = cache_nsa.transpose(0, 1, 3, 4, 5, 2)
    win_state_t = state_win.transpose(0, 1, 3, 4, 5, 2)

    tm_p = 512 if seq % 512 == 0 else Q_BLOCK
    tf = w_ffn_gate.shape[-1] // 2

    y_p = x_prompt
    y_s = x_sample.reshape(1, nseq, d)
    outs = {k: [] for k in ('sb_p', 'sb_s', 'nsa_p', 'nsa_s', 'win_p', 'win_s', 'conv_p', 'conv_s')}
    for l in range(depth):
        gpre = [norm_pre[l, i][None] for i in range(N_SUB)]
        gpost = [norm_post[l, i][None] for i in range(N_SUB)]
        y_p = _ffn(y_p, ada_p[l], gpre[0], gpost[0], wg[l, 0], wu[l, 0], wd[l, 0], sub=0, tm=tm_p, tf=tf)
        projr = _proj(y_p, ada_p[l], gpre[1], w_r[l], tm=tm_p, tn=tn)
        sbt, nsat, wint = _proj_t(y_p, ada_p[l], gpre[1], w_t[l], tm=tm_p)
        hrows = jnp.stack([_flat16_t(nsat[:, 0:KV_B]), _flat16_t(nsat[:, KV_B:2 * KV_B])], axis=1)
        kvc = _compress_prompt(hrows, pos[l], w1[l], w2p[l])
        y_cmp, pen = _cmp_prompt(projr, kvc, tab_cmp, ovl_p, offs, n_top=n_top_p)
        y_sel = _nsa_flash_prompt(projr, nsat, tab_sel, pen, offs, mode='sel')
        y_win = _nsa_flash_prompt(projr, wint, tab_win, pen, offs, mode='win')
        y_c = _sb_prompt(projr, sbt, offs)
        y_p, u_tail = _merge(y_p, ada_p[l], gpost[1], projr, None, (y_cmp, y_sel, y_win), y_c, conv_w[l],
                             wba[l], wbb[l], wbc[l], wo[l], offs, tm=min(tm_p, 256), decode=False)
        y_p = _ffn(y_p, ada_p[l], gpre[2], gpost[2], wg[l, 1], wu[l, 1], wd[l, 1], sub=2, tm=tm_p, tf=tf)
        n_keep = min(WINDOW, seq)
        to_state = lambda a, k, g: a.reshape(batch, k, g, HEAD_DIM, a.shape[-1]).transpose(0, 4, 1, 2, 3)
        outs['sb_p'].append(to_state(sbt, 2, H_C))
        outs['nsa_p'].append(to_state(nsat, 4, G_B))
        outs['win_p'].append(to_state(wint[:, :, seq - n_keep:], 2, G_B))
        outs['conv_p'].append(u_tail[:, 8 - (CONV_W - 1):])
        y_s = _ffn(y_s, ada_s[l], gpre[0], gpost[0], wg[l, 0], wu[l, 0], wd[l, 0], sub=0, tm=nseq, tf=tf)
        proj_s = _proj(y_s, ada_s[l], gpre[1], w_s[l], tm=nseq, tn=tn)
        proj_r = proj_s.reshape(nseq, 1, n_s)
        s_cmp, top_idx = _dec_cmp(l, page_table, cache_nsa_t, proj_r, pos[l], w1[l], w2p[l], tab_cmp_s, ovl_s,
                                  offs, n_top=n_top_s)
        sel_table = jnp.concatenate([page_table, top_idx[:, 0, 0:n_top_s], top_idx[:, 1, 0:n_top_s]], axis=1)
        sel_table = jnp.pad(sel_table, ((0, 0), (0, -sel_table.shape[1] % LANES)))
        s_sel, s_win = _dec_selwin(l, sel_table, n_pages, cache_nsa_t, proj_r, win_state_t, tab_win_s,
                                   rel_t, offs, n_top=n_top_s)
        s_c = _dec_sb(l, page_table, cache_sb_t, proj_r, offs)
        to_rows = lambda a: a.reshape(1, nseq, a.shape[-1])
        ctx = (state_conv[l, :, 0][None], state_conv[l, :, 1][None])
        y_s, u_s = _merge(y_s, ada_s[l], gpost[1], proj_s, ctx, (to_rows(s_cmp), to_rows(s_sel), to_rows(s_win)),
                          to_rows(s_c), conv_w[l], wba[l], wbb[l], wbc[l], wo[l], offs, tm=nseq, decode=True)
        y_s = _ffn(y_s, ada_s[l], gpre[2], gpost[2], wg[l, 1], wu[l, 1], wd[l, 1], sub=2, tm=nseq, tf=tf)
        ps = proj_s[0]
        o_s, o_n, o_w = offs['sb_s'], offs['nsa_s'], offs['win_s']
        outs['sb_s'].append(ps[:, o_s:o_s + KT_SB].reshape(nseq, 1, 2, H_C, HEAD_DIM))
        outs['nsa_s'].append(ps[:, o_n:o_n + KT_NSA].reshape(nseq, 1, 4, G_B, HEAD_DIM))
        win_new = ps[:, o_w:o_w + KT_WIN].reshape(nseq, 1, 2, G_B, HEAD_DIM)
        outs['win_s'].append(jnp.concatenate([state_win[l][:, 1:], win_new], axis=1))
        outs['conv_s'].append(jnp.stack([state_conv[l, :, 1], u_s[0]], axis=1))
    st = {k: jnp.stack(v) for k, v in outs.items()}
    return (y_p, y_s.reshape(nseq, 1, d), st['sb_p'], st['sb_s'], st['nsa_p'], st['nsa_s'],
            st['win_p'], st['win_s'], st['conv_p'], st['conv_s'])
```

```python
import functools
import math

import numpy as np
import jax
import jax.numpy as jnp
from jax import lax
from jax.experimental import pallas as pl
from jax.experimental.pallas import tpu as pltpu

F32 = jnp.float32
BF16 = jnp.bfloat16

HEAD_DIM = 64
CONV_CH = 512
CONV_W = 3
H_B = 8
G_B = 2
R_B = H_B // G_B
L_CMP = 32
D_CMP = 16
L_SEL = 64
N_SEL = 16
WINDOW = 512
H_C = 8
N_BUCKETS = 32
MAX_DIST = 128
N_SUB = 3
Q_BLOCK = 128
PAGE_SIZE = 128
EPS = 1e-6
NEG = -1e30
MASKED_BELOW = -1e29
SB_UNDERFLOW = -105.0
FORCE_BONUS = 1e4

W_A = CONV_CH
W_B = H_B * HEAD_DIM
W_C = H_C * HEAD_DIM
KV_B = G_B * HEAD_DIM
LANES = 128
VMEM_LIMIT = 56 * 1024 * 1024

KT_SB = 2 * W_C
KT_NSA = 4 * KV_B
KT_WIN = 2 * KV_B


def _layout(d_model):
    sizes = (W_A, W_A, W_A, W_B, KV_B, KV_B, KV_B, KV_B, KV_B, KV_B, 3 * H_B, W_C, W_C, W_C, 3 * d_model)
    starts = np.concatenate([[0], np.cumsum(sizes)]).astype(np.int64)
    names = ('a_v', 'a_b', 'a_c', 'b_q', 'b_kc', 'b_vc', 'b_ks', 'b_vs', 'b_kw', 'b_vw', 'b_g',
             'c_q', 'c_k', 'c_v', 'g_merge')
    seg = {n: (int(starts[i]), int(starts[i + 1])) for i, n in enumerate(names)}
    q0 = seg['b_q'][0]
    head = lambda h: (q0 + h * HEAD_DIM, q0 + (h + 1) * HEAD_DIM)
    bq_pairs = []
    for p in range(R_B):
        bq_pairs += [head(p), head(p + R_B)]
    row_part = [('g_merge', [seg['g_merge']], 0),
                ('conv', [seg['a_v'], seg['a_b'], seg['a_c']], 0),
                ('b_q', bq_pairs, 0),
                ('c_q', [seg['c_q']], 0),
                ('b_g', [seg['b_g']], LANES - 3 * H_B),
                ('k_rows', [seg['b_ks'], seg['b_kw']], LANES)]
    t_part = [('sb', [seg['c_k'], seg['c_v']], 0),
              ('nsa', [seg['b_kc'], seg['b_vc'], seg['b_ks'], seg['b_vs']], 0),
              ('win', [seg['b_kw'], seg['b_vw']], 0)]
    offs, o = {}, 0
    for n, rngs, padn in row_part:
        offs[n] = o
        o += sum(b - a for a, b in rngs) + padn
    n_row = o
    for n, rngs, padn in t_part:
        offs[n + '_s'] = o
        o += sum(b - a for a, b in rngs) + padn
    return offs, row_part, t_part, n_row, o


def _gather_cols(w, parts, scale_names):
    cols = []
    for n, rngs, padn in parts:
        for a, b in rngs:
            c = w[:, :, a:b]
            cols.append(c * (HEAD_DIM ** -0.5) if n in scale_names else c)
        if padn:
            cols.append(jnp.zeros(w.shape[:2] + (padn,), w.dtype))
    return jnp.concatenate(cols, axis=-1)


def _t5_thresholds():
    max_exact = N_BUCKETS // 2
    def bucket(n):
        if n < max_exact:
            return n
        b = max_exact + int(math.log(max(n, 1) / max_exact) / math.log(MAX_DIST / max_exact)
                            * (N_BUCKETS - max_exact))
        return min(b, N_BUCKETS - 1)
    thr = []
    for k in range(1, N_BUCKETS):
        n = 0
        while bucket(n) < k:
            n += 1
        thr.append(n)
    return tuple(thr)


_T5_THR = _t5_thresholds()


def _cparams(n_grid, vmem=VMEM_LIMIT):
    return pltpu.CompilerParams(dimension_semantics=("arbitrary",) * n_grid, vmem_limit_bytes=vmem)


def _dot(a, b):
    return jnp.dot(a, b, preferred_element_type=F32)


def _dot_nt(a, b):
    return lax.dot_general(a, b, (((1,), (1,)), ((), ())), preferred_element_type=F32)


def _rms(x):
    return x * lax.rsqrt(jnp.mean(x * x, axis=-1, keepdims=True) + EPS)


def _silu(x):
    return x * jax.nn.sigmoid(x)


def _lane_iota(shape):
    return lax.broadcasted_iota(jnp.int32, shape, len(shape) - 1)


def _row_iota(shape):
    return lax.broadcasted_iota(jnp.int32, shape, len(shape) - 2)


def _split_bf16(x):
    hi = x.astype(BF16)
    lo = (x - hi.astype(F32)).astype(BF16)
    return hi, lo


def _ada_kernel(c_ref, w_ref, b_ref, o_ref):
    h = _silu(c_ref[...]).astype(BF16)
    o_ref[...] = _dot(h, w_ref[...].astype(BF16)) + b_ref[...]


def _ada_all(c_all, w_ada, b_ada, tn=1536):
    depth, d, n = w_ada.shape
    rows = c_all.shape[0]
    return pl.pallas_call(
        _ada_kernel,
        grid=(depth, n // tn),
        in_specs=[pl.BlockSpec((rows, d), lambda l, j: (0, 0)),
                  pl.BlockSpec((None, d, tn), lambda l, j: (l, 0, j)),
                  pl.BlockSpec((None, 1, tn), lambda l, j: (l, 0, j))],
        out_specs=pl.BlockSpec((None, rows, tn), lambda l, j: (l, 0, j)),
        out_shape=jax.ShapeDtypeStruct((depth, rows, n), F32),
        compiler_params=_cparams(2),
        name="ada",
    )(c_all, w_ada, b_ada.reshape(depth, 1, n))


def _table_kernel(rel_ref, o_ref, *, base0, tile_step, row_step, col_step, max_dist):
    t = pl.program_id(0)
    h = pl.program_id(1)
    shape = o_ref.shape
    dist = (base0 + t * tile_step + _row_iota(shape) * row_step - _lane_iota(shape) * col_step)
    b = jnp.full(shape, rel_ref[0, h], F32)
    for k, thr in enumerate(_T5_THR):
        b = jnp.where(dist >= thr, rel_ref[k + 1, h], b)
    ok = dist >= 0
    if max_dist is not None:
        ok = ok & (dist <= max_dist)
    o_ref[...] = jnp.where(ok, b, NEG)


def _bias_table(rel_bias, n_tiles, rows, cols, *, base0, tile_step, row_step, col_step, max_dist=None,
                heads_on_lanes=False):
    n_heads = rel_bias.shape[1]
    if heads_on_lanes:
        return pl.pallas_call(
            functools.partial(_table_kernel, base0=base0, tile_step=tile_step, row_step=row_step,
                              col_step=col_step, max_dist=max_dist),
            grid=(n_tiles, n_heads),
            in_specs=[pl.BlockSpec(memory_space=pltpu.SMEM)],
            out_specs=pl.BlockSpec((None, rows, cols), lambda t, h: (t, 0, h)),
            out_shape=jax.ShapeDtypeStruct((n_tiles, rows, n_heads * cols), F32),
            compiler_params=_cparams(2),
            name="bias_table",
        )(rel_bias)
    out = pl.pallas_call(
        functools.partial(_table_kernel, base0=base0, tile_step=tile_step, row_step=row_step,
                          col_step=col_step, max_dist=max_dist),
        grid=(n_tiles, n_heads),
        in_specs=[pl.BlockSpec(memory_space=pltpu.SMEM)],
        out_specs=pl.BlockSpec((None, None, rows, cols), lambda t, h: (t, h, 0, 0)),
        out_shape=jax.ShapeDtypeStruct((n_tiles, n_heads, rows, cols), F32),
        compiler_params=_cparams(2),
        name="bias_table",
    )(rel_bias)
    return out.reshape(n_tiles, n_heads * rows, cols)


def _ffn_kernel(x_ref, ada_ref, gpre_ref, gpost_ref, wg_ref, wu_ref, wd_ref, o_ref, h_ref, acc_ref, *, sub):
    f = pl.program_id(2)

    @pl.when(f == 0)
    def _():
        h = _rms(x_ref[...]) * gpre_ref[...]
        h = h * (1.0 + ada_ref[3 * sub + 1]) + ada_ref[3 * sub]
        h_ref[...] = h.astype(BF16)
        acc_ref[...] = jnp.zeros_like(acc_ref)

    h = h_ref[...]
    a = _silu(_dot(h, wg_ref[...])) * _dot(h, wu_ref[...])
    acc_ref[...] += _dot(a.astype(BF16), wd_ref[...])

    @pl.when(f == pl.num_programs(2) - 1)
    def _():
        y = _rms(acc_ref[...]) * gpost_ref[...]
        o_ref[...] = x_ref[...] + 0.5 * (ada_ref[3 * sub + 2] * y)


def _ffn(x, ada, gpre, gpost, wg, wu, wd, *, sub, tm, tf):
    bg, t, d = x.shape
    f = wg.shape[1]
    mrows = ada.shape[2]
    return pl.pallas_call(
        functools.partial(_ffn_kernel, sub=sub),
        grid=(bg, t // tm, f // tf),
        in_specs=[pl.BlockSpec((None, tm, d), lambda b, i, j: (b, i, 0)),
                  pl.BlockSpec((None, 3 * N_SUB, mrows, d), lambda b, i, j: (b, 0, 0, 0)),
                  pl.BlockSpec((1, d), lambda b, i, j: (0, 0)),
                  pl.BlockSpec((1, d), lambda b, i, j: (0, 0)),
                  pl.BlockSpec((d, tf), lambda b, i, j: (0, j)),
                  pl.BlockSpec((d, tf), lambda b, i, j: (0, j)),
                  pl.BlockSpec((tf, d), lambda b, i, j: (j, 0))],
        out_specs=pl.BlockSpec((None, tm, d), lambda b, i, j: (b, i, 0)),
        out_shape=jax.ShapeDtypeStruct(x.shape, F32),
        scratch_shapes=[pltpu.VMEM((tm, d), BF16), pltpu.VMEM((tm, d), F32)],
        compiler_params=_cparams(3),
        name="ffn",
    )(x, ada, gpre, gpost, wg, wu, wd)


def _mod1(x_ref, ada_ref, gpre_ref):
    h = _rms(x_ref[...]) * gpre_ref[...]
    return (h * (1.0 + ada_ref[4]) + ada_ref[3]).astype(BF16)


def _proj_kernel(x_ref, ada_ref, gpre_ref, w_ref, o_ref, h_ref):
    @pl.when(pl.program_id(2) == 0)
    def _():
        h_ref[...] = _mod1(x_ref, ada_ref, gpre_ref)

    o_ref[...] = _dot(h_ref[...], w_ref[...])


def _proj(x, ada, gpre, w, *, tm, tn):
    bg, t, d = x.shape
    n = w.shape[1]
    mrows = ada.shape[2]
    return pl.pallas_call(
        _proj_kernel,
        grid=(bg, t // tm, n // tn),
        in_specs=[pl.BlockSpec((None, tm, d), lambda b, i, j: (b, i, 0)),
                  pl.BlockSpec((None, 3 * N_SUB, mrows, d), lambda b, i, j: (b, 0, 0, 0)),
                  pl.BlockSpec((1, d), lambda b, i, j: (0, 0)),
                  pl.BlockSpec((d, tn), lambda b, i, j: (0, j))],
        out_specs=pl.BlockSpec((None, tm, tn), lambda b, i, j: (b, i, j)),
        out_shape=jax.ShapeDtypeStruct((bg, t, n), F32),
        scratch_shapes=[pltpu.VMEM((tm, d), BF16)],
        compiler_params=_cparams(3),
        name="proj",
    )(x, ada, gpre, w)


def _proj_t_kernel(x_ref, ada_ref, gpre_ref, w_ref, sb_ref, nsa_ref, win_ref):
    h = _mod1(x_ref, ada_ref, gpre_ref)
    kt = _dot_nt(w_ref[...], h)
    sb_ref[...] = kt[0:KT_SB]
    nsa_ref[...] = kt[KT_SB:KT_SB + KT_NSA]
    win_ref[...] = kt[KT_SB + KT_NSA:KT_SB + KT_NSA + KT_WIN]


def _proj_t(x, ada, gpre, w_t, *, tm):
    bg, t, d = x.shape
    nf = w_t.shape[0]
    out = lambda rows: (pl.BlockSpec((None, rows, tm), lambda b, i: (b, 0, i)),
                        jax.ShapeDtypeStruct((bg, rows, t), F32))
    specs, shapes = zip(out(KT_SB), out(KT_NSA), out(KT_WIN))
    return pl.pallas_call(
        _proj_t_kernel,
        grid=(bg, t // tm),
        in_specs=[pl.BlockSpec((None, tm, d), lambda b, i: (b, i, 0)),
                  pl.BlockSpec((None, 3 * N_SUB, 1, d), lambda b, i: (b, 0, 0, 0)),
                  pl.BlockSpec((1, d), lambda b, i: (0, 0)),
                  pl.BlockSpec((nf, d), lambda b, i: (0, 0))],
        out_specs=list(specs),
        out_shape=list(shapes),
        compiler_params=_cparams(2),
        name="proj_t",
    )(x, ada, gpre, w_t)


def _compress_kernel(h_ref, pos_ref, w1_ref, w2_ref, o_ref):
    ncp = h_ref.shape[1]
    half = h_ref.shape[2]
    out = None
    for g in range(G_B):
        hg = h_ref[g]
        a = _dot((hg + pos_ref[0]).astype(BF16), w1_ref[0:half, :])
        bm = _dot((hg + pos_ref[1]).astype(BF16), w1_ref[half:2 * half, :])
        pre = a + pltpu.roll(bm, ncp - 1, 0)
        o = _dot(_silu(pre).astype(BF16), w2_ref[g])
        out = o if out is None else out + o
    o_ref[...] = out


def _compress_prompt(hrows, pos, w1, w2p):
    b, _, g, ncp, half = hrows.shape
    hid = w1.shape[2]
    return pl.pallas_call(
        _compress_kernel,
        grid=(b, 2),
        in_specs=[pl.BlockSpec((None, None, g, ncp, half), lambda i, k: (i, k, 0, 0, 0)),
                  pl.BlockSpec((None, 2, 1, half), lambda i, k: (k, 0, 0, 0)),
                  pl.BlockSpec((None, 2 * half, hid), lambda i, k: (k, 0, 0)),
                  pl.BlockSpec((None, g, hid, LANES), lambda i, k: (k, 0, 0, 0))],
        out_specs=pl.BlockSpec((None, None, ncp, LANES), lambda i, k: (i, k, 0, 0)),
        out_shape=jax.ShapeDtypeStruct((b, 2, ncp, LANES), F32),
        compiler_params=_cparams(2),
        name="compress",
    )(hrows, pos, w1, w2p)


def _stack_heads(blk_fn, tq):
    lane = _lane_iota((tq, LANES))
    lo = [jnp.where(lane < HEAD_DIM, blk_fn(p), 0.0) for p in range(R_B)]
    hi = [jnp.where(lane >= HEAD_DIM, blk_fn(p), 0.0) for p in range(R_B)]
    return jnp.concatenate(lo + hi, axis=0)


def _unstack_gated(o, gsig, branch, tq):
    lane = _lane_iota((tq, LANES))
    out = []
    for p in range(R_B):
        c_lo = 3 * p + branch
        c_hi = 3 * (p + R_B) + branch
        blk = jnp.where(lane < HEAD_DIM, o[p * tq:(p + 1) * tq], o[(p + R_B) * tq:(p + R_B + 1) * tq])
        gate = jnp.where(lane < HEAD_DIM, gsig[:, c_lo:c_lo + 1], gsig[:, c_hi:c_hi + 1])
        out.append(blk * gate)
    return out


def _topk_penalty(score_t, n_top):
    nb = score_t.shape[0]
    jrow = _row_iota(score_t.shape).astype(F32)
    x = score_t
    sel = jnp.zeros(score_t.shape, F32)
    for _ in range(n_top):
        mx = jnp.max(x, axis=0, keepdims=True)
        first = jnp.min(jnp.where(x == mx, jrow, float(nb)), axis=0, keepdims=True)
        hit = jrow == first
        sel = jnp.where(hit, 1.0, sel)
        x = jnp.where(hit, -jnp.inf, x)
    return jnp.where(sel > 0.5, 0.0, NEG)


def _cmp_kernel(q_ref, kc_ref, vc_ref, tab_ref, ovl_ref, bg_ref, o_ref, pen_ref, *, n_top):
    tq = q_ref.shape[0]
    i = pl.program_id(1)
    q8 = _stack_heads(lambda p: q_ref[:, p * LANES:(p + 1) * LANES], tq).astype(BF16)
    s = _dot_nt(q8, kc_ref[...].astype(BF16)) + tab_ref[...]
    ok = s > MASKED_BELOW
    e = jnp.where(ok, jnp.exp(s - jnp.max(s, axis=-1, keepdims=True)), 0.0)
    p = (e / jnp.maximum(jnp.sum(e, axis=-1, keepdims=True), 1e-30)).astype(BF16)
    o = _dot(p, vc_ref[...].astype(BF16))
    gsig = jax.nn.sigmoid(bg_ref[...])
    for pi, blk in enumerate(_unstack_gated(o, gsig, 0, tq)):
        o_ref[:, pi * LANES:(pi + 1) * LANES] = blk
    imp8 = _dot(p, ovl_ref[...])
    lane = _lane_iota((tq, LANES))
    g0 = imp8[0:tq] + imp8[tq:2 * tq] + imp8[2 * tq:3 * tq] + imp8[3 * tq:4 * tq]
    g1 = imp8[4 * tq:5 * tq] + imp8[5 * tq:6 * tq] + imp8[6 * tq:7 * tq] + imp8[7 * tq:8 * tq]
    imp = jnp.where(lane < L_SEL, g0, g1)
    qpos = i * tq + _row_iota((tq, LANES))
    j = lane % L_SEL
    cur = qpos // L_SEL
    forced = (j == 0) | (j == cur) | (j == cur - 1)
    score = jnp.where(j * L_SEL <= qpos, imp + FORCE_BONUS * forced.astype(F32), NEG)
    st = score.T
    pen_t = jnp.concatenate([_topk_penalty(st[g * L_SEL:(g + 1) * L_SEL], n_top) for g in range(G_B)], axis=0)
    pen_ref[...] = pen_t


def _cmp_prompt(projr, kvc, tab, ovl, offs, *, n_top, tq=Q_BLOCK):
    b, t, _ = projr.shape
    ncp = kvc.shape[2]
    qb = offs['b_q'] // W_B
    gb = offs['b_g'] // LANES
    return pl.pallas_call(
        functools.partial(_cmp_kernel, n_top=n_top),
        grid=(b, t // tq),
        in_specs=[pl.BlockSpec((None, tq, W_B), lambda bi, i: (bi, i, qb)),
                  pl.BlockSpec((None, None, ncp, LANES), lambda bi, i: (bi, 0, 0, 0)),
                  pl.BlockSpec((None, None, ncp, LANES), lambda bi, i: (bi, 1, 0, 0)),
                  pl.BlockSpec((None, H_B * tq, ncp), lambda bi, i: (i, 0, 0)),
                  pl.BlockSpec((ncp, LANES), lambda bi, i: (0, 0)),
                  pl.BlockSpec((None, tq, LANES), lambda bi, i: (bi, i, gb))],
        out_specs=[pl.BlockSpec((None, tq, W_B), lambda bi, i: (bi, i, 0)),
                   pl.BlockSpec((None, LANES, tq), lambda bi, i: (bi, 0, i))],
        out_shape=[jax.ShapeDtypeStruct((b, t, W_B), F32), jax.ShapeDtypeStruct((b, LANES, t), F32)],
        compiler_params=_cparams(2),
        name="nsa_cmp",
    )(projr, kvc, kvc, tab, ovl, projr)


def _nsa_flash_kernel(q_ref, k_ref, v_ref, tab_ref, pen_ref, bg_ref, o_ref, qa_ref, m_ref, l_ref, acc_ref, *,
                      mode, branch):
    tq = q_ref.shape[0]
    tk = 2 * tq
    i = pl.program_id(1)
    jd = i // 2
    par = i % 2

    row = _row_iota((LANES, tq))
    for p in range(R_B):
        qt = q_ref[:, p * LANES:(p + 1) * LANES].T
        qa_ref[0:LANES, p * tq:(p + 1) * tq] = jnp.where(row < HEAD_DIM, qt, 0.0).astype(BF16)
        qa_ref[0:LANES, (p + R_B) * tq:(p + R_B + 1) * tq] = jnp.where(row >= HEAD_DIM, qt, 0.0).astype(BF16)
    if mode == 'sel':
        pen = pen_ref[...]
        lo = jnp.where(row < L_SEL, pen, 0.0).astype(BF16)
        hi = jnp.where(row >= L_SEL, pen, 0.0).astype(BF16)
        qa_ref[LANES:2 * LANES, :] = jnp.concatenate([lo] * R_B + [hi] * R_B, axis=1)
    m_ref[...] = jnp.full_like(m_ref, NEG)
    l_ref[...] = jnp.zeros_like(l_ref)
    acc_ref[...] = jnp.zeros_like(acc_ref)

    def step(j, bias):
        k0 = pl.multiple_of(j * tk, tk)
        kt = k_ref[pl.ds(k0, tk), :].astype(BF16)
        if mode == 'sel':
            blk = (tk // L_SEL) * j + _row_iota((tk, LANES)) // L_SEL
            et = jnp.where(blk == _lane_iota((tk, LANES)) % L_SEL, 1.0, 0.0).astype(BF16)
            kt = jnp.concatenate([kt, et], axis=1)
        s = _dot(kt, qa_ref[...]) + bias
        m_old = m_ref[...]
        m_new = jnp.maximum(m_old, jnp.max(s, axis=0, keepdims=True))
        alpha = jnp.exp(m_old - m_new)
        p = jnp.where(s > MASKED_BELOW, jnp.exp(s - m_new), 0.0)
        l_ref[...] = alpha * l_ref[...] + jnp.sum(p, axis=0, keepdims=True)
        acc_ref[...] = alpha * acc_ref[...] + _dot(v_ref[:, pl.ds(k0, tk)].astype(BF16), p.astype(BF16))
        m_ref[...] = m_new

    n_near = tab_ref.shape[0] // 2
    if mode == 'sel':
        far_bias = tab_ref[3, 0:1, :]

        def far_body(j, c):
            step(j, far_bias)
            return c

        lax.fori_loop(0, jnp.maximum(jd - (n_near - 1), 0), far_body, 0)
    for dj in range(n_near - 1, 0, -1):
        @pl.when(jd >= dj)
        def _(dj=dj):
            step(jd - dj, tab_ref[par + 2 * dj])

    step(jd, tab_ref[par])

    ot = acc_ref[...] / jnp.maximum(l_ref[...], 1e-30)
    o = jnp.concatenate([ot[:, h * tq:(h + 1) * tq].T for h in range(H_B)], axis=0)
    gsig = jax.nn.sigmoid(bg_ref[...])
    for pi, blk in enumerate(_unstack_gated(o, gsig, branch, tq)):
        o_ref[:, pi * LANES:(pi + 1) * LANES] = blk


def _nsa_flash_prompt(projr, vt, tab, pen_t, offs, *, mode, tq=Q_BLOCK):
    b, t, _ = projr.shape
    assert t % (2 * tq) == 0 and WINDOW == 4 * tq and tq == LANES
    qb = offs['b_q'] // W_B
    gb = offs['b_g'] // LANES
    kb = offs['k_rows'] // LANES
    if mode == 'sel':
        vb, branch, qa_rows = 3, 1, 2 * LANES
    else:
        kb, vb, branch, qa_rows = kb + 1, 1, 2, LANES
    return pl.pallas_call(
        functools.partial(_nsa_flash_kernel, mode=mode, branch=branch),
        grid=(b, t // tq),
        in_specs=[pl.BlockSpec((None, tq, W_B), lambda bi, i: (bi, i, qb)),
                  pl.BlockSpec((None, t, KV_B), lambda bi, i: (bi, 0, kb)),
                  pl.BlockSpec((None, KV_B, t), lambda bi, i: (bi, vb, 0)),
                  pl.BlockSpec(tab.shape, lambda bi, i: (0, 0, 0)),
                  pl.BlockSpec((None, LANES, tq), lambda bi, i: (bi, 0, i)),
                  pl.BlockSpec((None, tq, LANES), lambda bi, i: (bi, i, gb))],
        out_specs=pl.BlockSpec((None, tq, W_B), lambda bi, i: (bi, i, 0)),
        out_shape=jax.ShapeDtypeStruct((b, t, W_B), F32),
        scratch_shapes=[pltpu.VMEM((qa_rows, H_B * tq), BF16),
                        pltpu.VMEM((1, H_B * tq), F32),
                        pltpu.VMEM((1, H_B * tq), F32),
                        pltpu.VMEM((KV_B, H_B * tq), F32)],
        compiler_params=_cparams(2),
        name="nsa_" + mode,
    )(projr, projr, vt, tab, pen_t, projr)


def _sb_kernel(q_ref, k_ref, v_ref, o_ref, qs_ref, carry_ref, acc_ref):
    tq = q_ref.shape[0]
    tk = tq
    i = pl.program_id(2)
    lane = _lane_iota((tq, LANES))
    q = q_ref[...]
    qs_ref[...] = jnp.concatenate([jnp.where(lane < HEAD_DIM, q, 0.0),
                                   jnp.where(lane >= HEAD_DIM, q, 0.0)], axis=0).astype(BF16)
    carry_ref[...] = jnp.zeros_like(carry_ref)
    acc_ref[...] = jnp.zeros_like(acc_ref)
    jj = _row_iota((2 * tk, tk))
    tri2 = jnp.where(jnp.where(jj >= tk, jj - tk, jj) > _lane_iota((2 * tk, tk)), 1.0, 0.0).astype(BF16)

    def step(j, diagonal):
        k0 = pl.multiple_of(j * tk, tk)
        z = _dot(qs_ref[...], k_ref[:, pl.ds(k0, tk)].astype(BF16))
        t = jnp.log(1.0 + jnp.exp(-jnp.abs(z)))
        log_1mb = -(jnp.maximum(z, 0.0) + t)
        log_b = z + log_1mb
        if diagonal:
            rr = _row_iota((2 * tq, tk))
            causal = _lane_iota((2 * tq, tk)) < jnp.where(rr >= tq, rr - tq, rr)
            log_1mb = jnp.where(causal, log_1mb, 0.0)
        hi, lo = _split_bf16(log_1mb)
        after = _dot(jnp.concatenate([hi, lo], axis=1), tri2) + carry_ref[...]
        a = jnp.exp(log_b + after)
        if diagonal:
            a = jnp.where(causal, a, 0.0)
        carry_ref[...] += jnp.sum(log_1mb, axis=-1, keepdims=True)
        acc_ref[...] += _dot_nt(a.astype(BF16), v_ref[:, pl.ds(k0, tk)].astype(BF16))

    step(i, True)

    def still_visible():
        return jnp.max(carry_ref[...]) > SB_UNDERFLOW

    def body(c):
        j, _ = c
        step(j, False)
        return j - 1, still_visible()

    lax.while_loop(lambda c: jnp.logical_and(c[0] >= 0, c[1]), body, (i - 1, still_visible()))
    acc = acc_ref[...]
    o_ref[...] = jnp.where(lane < HEAD_DIM, acc[0:tq], acc[tq:2 * tq])


def _sb_prompt(projr, sbt, offs, *, tq=2 * Q_BLOCK):
    b, t, _ = projr.shape
    n_pairs = W_C // LANES
    qb = offs['c_q'] // LANES
    return pl.pallas_call(
        _sb_kernel,
        grid=(b, n_pairs, t // tq),
        in_specs=[pl.BlockSpec((None, tq, LANES), lambda bi, p, i: (bi, i, qb + p)),
                  pl.BlockSpec((None, LANES, t), lambda bi, p, i: (bi, p, 0)),
                  pl.BlockSpec((None, LANES, t), lambda bi, p, i: (bi, n_pairs + p, 0))],
        out_specs=pl.BlockSpec((None, tq, LANES), lambda bi, p, i: (bi, i, p)),
        out_shape=jax.ShapeDtypeStruct((b, t, W_C), F32),
        scratch_shapes=[pltpu.VMEM((2 * tq, LANES), BF16),
                        pltpu.VMEM((2 * tq, 1), F32),
                        pltpu.VMEM((2 * tq, LANES), F32)],
        compiler_params=_cparams(3),
        name="sb",
    )(projr, sbt, sbt)


def _merge_kernel(*refs, decode):
    if decode:
        (x_ref, ada_ref, gpost_ref, conv_ref, s0_ref, s1_ref, gm0_ref, gm1_ref, gm2_ref, yb0_ref, yb1_ref,
         yb2_ref, yc_ref, cw_ref, wba_ref, wbb_ref, wbc_ref, wo_ref, o_ref, u_ref) = refs
    else:
        (x_ref, ada_ref, gpost_ref, conv_ref, halo_ref, gm0_ref, gm1_ref, gm2_ref, yb0_ref, yb1_ref, yb2_ref,
         yc_ref, cw_ref, wba_ref, wbb_ref, wbc_ref, wo_ref, o_ref, u_ref) = refs
    tm = x_ref.shape[0]
    u = conv_ref[:, 2 * W_A:3 * W_A] * conv_ref[:, 0:W_A]
    if decode:
        um2 = s0_ref[...]
        um1 = s1_ref[...]
        u_ref[...] = u
    else:
        i = pl.program_id(1)
        keep = (i > 0).astype(F32)
        hu = (halo_ref[:, 2 * W_A:3 * W_A] * halo_ref[:, 0:W_A]) * keep
        row = _row_iota((tm, W_A))
        um1 = jnp.where(row >= 1, pltpu.roll(u, 1, 0), hu[7:8])
        um2 = jnp.where(row >= 2, pltpu.roll(u, 2, 0), jnp.where(row == 1, hu[7:8], hu[6:7]))

        @pl.when(i == pl.num_programs(1) - 1)
        def _():
            u_ref[...] = u[tm - 8:tm]

    y_a = conv_ref[:, W_A:2 * W_A] * (cw_ref[0:1] * um2 + cw_ref[1:2] * um1 + cw_ref[2:3] * u)
    y_b = yb0_ref[...] + yb1_ref[...] + yb2_ref[...]
    merged = (jax.nn.sigmoid(gm0_ref[...]) * _dot(y_a.astype(BF16), wba_ref[...])
              + jax.nn.sigmoid(gm1_ref[...]) * _dot(y_b.astype(BF16), wbb_ref[...])
              + jax.nn.sigmoid(gm2_ref[...]) * _dot(yc_ref[...].astype(BF16), wbc_ref[...]))
    out = _dot(merged.astype(BF16), wo_ref[...])
    o_ref[...] = x_ref[...] + ada_ref[5] * (_rms(out) * gpost_ref[...])


def _merge(x, ada, gpost, projr, conv_state, yb, yc, cw, wba, wbb, wbc, wo, offs, *, tm, decode):
    bg, t, d = x.shape
    mrows = ada.shape[2]
    cb = offs['conv'] // (3 * W_A)
    gmb = offs['g_merge'] // d
    row_spec = lambda w, cblk: pl.BlockSpec((None, tm, w), lambda b, i: (b, i, cblk))
    full = lambda a: pl.BlockSpec(a.shape, lambda b, i: (0,) * a.ndim)
    if decode:
        ctx_specs = [row_spec(W_A, 0), row_spec(W_A, 0)]
        ctx = list(conv_state)
        u_shape, u_spec = (bg, t, W_A), row_spec(W_A, 0)
    else:
        r8 = tm // 8
        ctx_specs = [pl.BlockSpec((None, 8, 3 * W_A), lambda b, i: (b, jnp.maximum(i * r8 - 1, 0), cb))]
        ctx = [projr]
        u_shape, u_spec = (bg, 8, W_A), pl.BlockSpec((None, 8, W_A), lambda b, i: (b, 0, 0))
    out, u = pl.pallas_call(
        functools.partial(_merge_kernel, decode=decode),
        grid=(bg, t // tm),
        in_specs=[row_spec(d, 0),
                  pl.BlockSpec((None, 3 * N_SUB, mrows, d), lambda b, i: (b, 0, 0, 0)),
                  pl.BlockSpec((1, d), lambda b, i: (0, 0)),
                  row_spec(3 * W_A, cb)] + ctx_specs + [
                  row_spec(d, gmb), row_spec(d, gmb + 1), row_spec(d, gmb + 2),
                  row_spec(W_B, 0), row_spec(W_B, 0), row_spec(W_B, 0), row_spec(W_C, 0),
                  full(cw), full(wba), full(wbb), full(wbc), full(wo)],
        out_specs=[row_spec(d, 0), u_spec],
        out_shape=[jax.ShapeDtypeStruct(x.shape, F32), jax.ShapeDtypeStruct(u_shape, F32)],
        compiler_params=_cparams(2),
        name="merge",
    )(x, ada, gpost, projr, *ctx, projr, projr, projr, yb[0], yb[1], yb[2], yc, cw, wba, wbb, wbc, wo)
    return out, u


def _dec_cmp_kernel(pt_ref, *refs, n_pages_step, n_top, past):
    del pt_ref
    pages = refs[:n_pages_step]
    (q_ref, pos_ref, w1_ref, w2_ref, tab_ref, ovl_ref, bg_ref, o_ref, idx_ref, h_ref) = refs[n_pages_step:]
    c = pl.program_id(1)
    lane = _lane_iota((8, LANES))
    rpp = PAGE_SIZE // D_CMP
    ii = _row_iota((PAGE_SIZE, PAGE_SIZE))
    perm = jnp.where(_lane_iota((PAGE_SIZE, PAGE_SIZE)) == D_CMP * (ii % rpp) + ii // rpp, 1.0, 0.0).astype(BF16)
    for k in range(n_pages_step):
        row0 = pl.multiple_of((c * n_pages_step + k) * rpp, rpp)
        xt = pages[k][...].reshape(2 * KV_B, PAGE_SIZE).astype(BF16)
        rows = _dot_nt(perm, xt)
        for m in range(D_CMP // 2):
            ev = rows[2 * m * rpp:(2 * m + 1) * rpp]
            od = rows[(2 * m + 1) * rpp:(2 * m + 2) * rpp]
            for kind in range(2):
                e = ev[:, kind * LANES:(kind + 1) * LANES]
                o = od[:, kind * LANES:(kind + 1) * LANES]
                h_ref[kind, 0, pl.ds(row0, rpp), m * LANES:(m + 1) * LANES] = (
                    jnp.where(lane < HEAD_DIM, e, pltpu.roll(o, HEAD_DIM, 1)))
                h_ref[kind, 1, pl.ds(row0, rpp), m * LANES:(m + 1) * LANES] = (
                    jnp.where(lane < HEAD_DIM, pltpu.roll(e, HEAD_DIM, 1), o))

    @pl.when(c == pl.num_programs(1) - 1)
    def _():
        ncp = h_ref.shape[2]
        half = h_ref.shape[3]
        kv = []
        for kind in range(2):
            out = None
            for g in range(G_B):
                hg = h_ref[kind, g]
                a = _dot((hg + pos_ref[kind, 0]).astype(BF16), w1_ref[kind, 0:half, :])
                bm = _dot((hg + pos_ref[kind, 1]).astype(BF16), w1_ref[kind, half:2 * half, :])
                pre = a + pltpu.roll(bm, ncp - 1, 0)
                o = _dot(_silu(pre).astype(BF16), w2_ref[kind, g])
                out = o if out is None else out + o
            kv.append(out.astype(BF16))
        q8 = _stack_heads(lambda p: q_ref[:, p * LANES:(p + 1) * LANES], 1).astype(BF16)
        s = _dot_nt(q8, kv[0]) + tab_ref[...]
        ok = s > MASKED_BELOW
        e = jnp.where(ok, jnp.exp(s - jnp.max(s, axis=-1, keepdims=True)), 0.0)
        p = (e / jnp.maximum(jnp.sum(e, axis=-1, keepdims=True), 1e-30)).astype(BF16)
        o = _dot(p, kv[1])
        gsig = jax.nn.sigmoid(bg_ref[...])
        for pi, blk in enumerate(_unstack_gated(o, gsig, 0, 1)):
            o_ref[:, pi * LANES:(pi + 1) * LANES] = blk
        imp8 = _dot(p, ovl_ref[...])
        imp = jnp.concatenate([imp8[0:1] + imp8[1:2] + imp8[2:3] + imp8[3:4],
                               imp8[4:5] + imp8[5:6] + imp8[6:7] + imp8[7:8]], axis=0)
        nbl = imp.shape[1]
        j = _lane_iota(imp.shape)
        cur = past // L_SEL
        forced = (j == 0) | (j == cur) | (j == cur - 1)
        x = jnp.where(j * L_SEL <= past, imp + FORCE_BONUS * forced.astype(F32), NEG)
        jf = j.astype(F32)
        slot = _lane_iota((G_B, LANES))
        idx = jnp.zeros((G_B, LANES), F32)
        for it in range(n_top):
            mx = jnp.max(x, axis=-1, keepdims=True)
            first = jnp.min(jnp.where(x == mx, jf, float(nbl)), axis=-1, keepdims=True)
            idx = jnp.where(slot == it, first, idx)
            x = jnp.where(jf == first, -jnp.inf, x)
        idx_ref[...] = jnp.zeros(idx_ref.shape, jnp.int32)
        idx_ref[0:G_B, :] = idx.astype(jnp.int32)


def _dec_cmp(layer, page_table, cache_nsa_t, proj_s, pos, w1, w2p, tab, ovl, offs, *, n_top, n_pages_step=8):
    nseq, n_pages = page_table.shape
    past = n_pages * PAGE_SIZE
    ncp = past // D_CMP
    half = D_CMP * HEAD_DIM
    qb = offs['b_q'] // W_B
    gb = offs['b_g'] // LANES
    page_specs = [pl.BlockSpec((None, None, 2, G_B, HEAD_DIM, PAGE_SIZE),
                               functools.partial(
                                   lambda b, c, pt, k: (layer, pt[b, c * n_pages_step + k], 0, 0, 0, 0), k=k))
                  for k in range(n_pages_step)]
    full = lambda a: pl.BlockSpec(a.shape, lambda b, c, pt: (0,) * a.ndim)
    grid_spec = pltpu.PrefetchScalarGridSpec(
        num_scalar_prefetch=1,
        grid=(nseq, n_pages // n_pages_step),
        in_specs=page_specs + [pl.BlockSpec((None, 1, W_B), lambda b, c, pt: (b, 0, qb)),
                               full(pos), full(w1), full(w2p), full(tab), full(ovl),
                               pl.BlockSpec((None, 1, LANES), lambda b, c, pt: (b, 0, gb))],
        out_specs=[pl.BlockSpec((None, 1, W_B), lambda b, c, pt: (b, 0, 0)),
                   pl.BlockSpec((None, 8, LANES), lambda b, c, pt: (b, 0, 0))],
        scratch_shapes=[pltpu.VMEM((2, G_B, ncp, half), F32)],
    )
    return pl.pallas_call(
        functools.partial(_dec_cmp_kernel, n_pages_step=n_pages_step, n_top=n_top, past=past),
        grid_spec=grid_spec,
        out_shape=[jax.ShapeDtypeStruct((nseq, 1, W_B), F32), jax.ShapeDtypeStruct((nseq, 8, LANES), jnp.int32)],
        compiler_params=_cparams(2),
        name="dec_cmp",
    )(page_table, *([cache_nsa_t] * n_pages_step), proj_s, pos, w1, w2p, tab, ovl, proj_s)


def _dec_selwin_kernel(sp_ref, *refs, n_top, past, n_pages):
    n_blk = G_B * n_top
    blocks = refs[:n_blk]
    (q_ref, ksn_ref, vsn_ref, wst_ref, kwn_ref, vwn_ref, tabw_ref, relt_ref, bg_ref,
     osel_ref, owin_ref, kc_ref, vc_ref) = refs[n_blk:]
    b = pl.program_id(0)
    n_past_blk = past // L_SEL
    halves = PAGE_SIZE // L_SEL
    q8f = _stack_heads(lambda p: q_ref[:, p * LANES:(p + 1) * LANES], 1)
    q8 = q8f.astype(BF16)
    q8r = q8.astype(F32)
    gsig = jax.nn.sigmoid(bg_ref[...])
    row1 = _row_iota((H_B, 1))

    for n in range(n_blk):
        kc_ref[:, n * PAGE_SIZE:(n + 1) * PAGE_SIZE] = blocks[n][0].reshape(KV_B, PAGE_SIZE).astype(BF16)
        vc_ref[:, n * PAGE_SIZE:(n + 1) * PAGE_SIZE] = blocks[n][1].reshape(KV_B, PAGE_SIZE).astype(BF16)
    nk = n_blk * PAGE_SIZE
    s = _dot(q8, kc_ref[...])
    lane = _lane_iota((H_B, nk))
    slot = lane // PAGE_SIZE
    within = lane % PAGE_SIZE
    blk_of = jnp.zeros((H_B, nk), jnp.int32)
    has_new = [jnp.int32(0)] * G_B
    for n in range(n_blk):
        bi = sp_ref[b, n_pages + n]
        blk_of = jnp.where(slot == n, bi, blk_of)
        has_new[n // n_top] = jnp.maximum(has_new[n // n_top], (bi == n_past_blk).astype(jnp.int32))
    kpos = (blk_of // halves) * PAGE_SIZE + within
    dist = past - kpos
    row = _row_iota((H_B, nk))
    ok = ((within // L_SEL == blk_of % halves) & (blk_of < n_past_blk)
          & (row // R_B == slot // n_top))
    bias = jnp.broadcast_to(relt_ref[:, 0:1], (H_B, nk))
    for k, thr in enumerate(_T5_THR):
        bias = jnp.where(dist >= thr, relt_ref[:, k + 1:k + 2], bias)
    s = jnp.where(ok, s + bias, NEG)
    new_ok = jnp.where(row1 < R_B, has_new[0], has_new[1]) > 0
    s_new = jnp.sum(q8r * ksn_ref[...].astype(BF16).astype(F32), axis=-1, keepdims=True)
    s_new = jnp.where(new_ok, s_new + relt_ref[:, 0:1], NEG)
    m = jnp.maximum(jnp.max(s, axis=-1, keepdims=True), s_new)
    e = jnp.where(ok, jnp.exp(s - m), 0.0)
    e_new = jnp.where(new_ok, jnp.exp(s_new - m), 0.0)
    den = jnp.maximum(jnp.sum(e, axis=-1, keepdims=True) + e_new, 1e-30)
    o = (_dot_nt(e.astype(BF16), vc_ref[...]) + e_new * vsn_ref[...].astype(BF16).astype(F32)) / den
    for pi, blk in enumerate(_unstack_gated(o, gsig, 1, 1)):
        osel_ref[:, pi * LANES:(pi + 1) * LANES] = blk

    n_ctx = wst_ref.shape[-1]
    sw = _dot(q8, wst_ref[0].reshape(KV_B, n_ctx).astype(BF16)) + tabw_ref[...]
    okw = sw > MASKED_BELOW
    sw_new = jnp.sum(q8r * kwn_ref[...].astype(BF16).astype(F32), axis=-1, keepdims=True) + relt_ref[:, 0:1]
    mw = jnp.maximum(jnp.max(sw, axis=-1, keepdims=True), sw_new)
    ew = jnp.where(okw, jnp.exp(sw - mw), 0.0)
    ew_new = jnp.exp(sw_new - mw)
    denw = jnp.maximum(jnp.sum(ew, axis=-1, keepdims=True) + ew_new, 1e-30)
    ow = (_dot_nt(ew.astype(BF16), wst_ref[1].reshape(KV_B, n_ctx).astype(BF16))
          + ew_new * vwn_ref[...].astype(BF16).astype(F32)) / denw
    for pi, blk in enumerate(_unstack_gated(ow, gsig, 2, 1)):
        owin_ref[:, pi * LANES:(pi + 1) * LANES] = blk


def _dec_selwin(layer, sel_table, n_pages, cache_nsa_t, proj_s, win_state_t, tabw, rel_t, offs, *, n_top):
    nseq = sel_table.shape[0]
    past = n_pages * PAGE_SIZE
    n_blk = G_B * n_top
    n_past_blk = past // L_SEL
    halves = PAGE_SIZE // L_SEL
    n_ctx = win_state_t.shape[-1]
    qb = offs['b_q'] // W_B
    gb = offs['b_g'] // LANES
    nb = offs['nsa_s'] // LANES
    wb = offs['win_s'] // LANES

    def blk_map(b, sp, n):
        bi = jnp.clip(sp[b, n_pages + n], 0, n_past_blk - 1)
        return (layer, sp[b, bi // halves], 1, 0, 0, 0)

    blk_specs = [pl.BlockSpec((None, None, 2, G_B, HEAD_DIM, PAGE_SIZE), functools.partial(blk_map, n=n))
                 for n in range(n_blk)]
    row = lambda cblk: pl.BlockSpec((None, 1, LANES), lambda b, sp: (b, 0, cblk))
    full = lambda a: pl.BlockSpec(a.shape, lambda b, sp: (0,) * a.ndim)
    grid_spec = pltpu.PrefetchScalarGridSpec(
        num_scalar_prefetch=1,
        grid=(nseq,),
        in_specs=blk_specs + [pl.BlockSpec((None, 1, W_B), lambda b, sp: (b, 0, qb)),
                              row(nb + 2), row(nb + 3),
                              pl.BlockSpec((None, None, 2, G_B, HEAD_DIM, n_ctx),
                                           lambda b, sp: (layer, b, 0, 0, 0, 0)),
                              row(wb), row(wb + 1), full(tabw), full(rel_t), row(gb)],
        out_specs=[pl.BlockSpec((None, 1, W_B), lambda b, sp: (b, 0, 0)),
                   pl.BlockSpec((None, 1, W_B), lambda b, sp: (b, 0, 0))],
        scratch_shapes=[pltpu.VMEM((KV_B, n_blk * PAGE_SIZE), BF16), pltpu.VMEM((KV_B, n_blk * PAGE_SIZE), BF16)],
    )
    return pl.pallas_call(
        functools.partial(_dec_selwin_kernel, n_top=n_top, past=past, n_pages=n_pages),
        grid_spec=grid_spec,
        out_shape=[jax.ShapeDtypeStruct((nseq, 1, W_B), F32)] * 2,
        compiler_params=_cparams(1),
        name="dec_selwin",
    )(sel_table, *([cache_nsa_t] * n_blk), proj_s, proj_s, proj_s, win_state_t, proj_s, proj_s,
      tabw, rel_t, proj_s)


def _dec_sb_kernel(pt_ref, *refs, n_pages_step):
    del pt_ref
    pages = refs[:n_pages_step]
    q_ref, o_ref, qcol_ref, carry_ref, acc_ref = refs[n_pages_step:]
    c = pl.program_id(1)
    ps = PAGE_SIZE

    @pl.when(c == 0)
    def _():
        qt = jnp.broadcast_to(q_ref[...], (LANES, W_C)).T
        qcol_ref[...] = qt.reshape(H_C, HEAD_DIM, LANES)
        carry_ref[...] = jnp.zeros_like(carry_ref)
        acc_ref[...] = jnp.zeros_like(acc_ref)

    tri = jnp.where(_row_iota((ps, ps)) > _lane_iota((ps, ps)), 1.0, 0.0).astype(BF16)
    hrow = _row_iota((H_C, ps))
    for k in range(n_pages_step):
        z = jnp.zeros((H_C, ps), F32)
        for h in range(H_C):
            zh = jnp.sum(qcol_ref[h] * pages[k][0, h], axis=0, keepdims=True)
            z = jnp.where(hrow == h, zh, z)
        t = jnp.log1p(jnp.exp(-jnp.abs(z)))
        log_1mb = -(jnp.maximum(z, 0.0) + t)
        log_b = jnp.minimum(z, 0.0) - t
        hi, lo = _split_bf16(log_1mb)
        after = _dot(hi, tri) + _dot(lo, tri) + carry_ref[...]
        a = jnp.exp(log_b + after)
        carry_ref[...] += jnp.sum(log_1mb, axis=-1, keepdims=True)
        for h in range(H_C):
            acc_ref[h] += a[h:h + 1, :] * pages[k][1, h]

    @pl.when(c == pl.num_programs(1) - 1)
    def _():
        acc = acc_ref[...].reshape(W_C, LANES)
        ones = jnp.ones((8, LANES), BF16)
        hi = acc.astype(BF16)
        r1 = acc - hi.astype(F32)
        mid = r1.astype(BF16)
        lo = (r1 - mid.astype(F32)).astype(BF16)
        tot = _dot_nt(ones, hi) + _dot_nt(ones, mid) + _dot_nt(ones, lo)
        o_ref[...] = tot[0:1]


def _dec_sb(layer, page_table, cache_sb_t, proj_s, offs, *, n_pages_step=8):
    nseq, n_pages = page_table.shape
    qb = offs['c_q'] // W_C
    page_specs = [pl.BlockSpec((None, None, 2, H_C, HEAD_DIM, PAGE_SIZE),
                               functools.partial(
                                   lambda b, c, pt, k: (layer, pt[b, n_pages - 1 - (c * n_pages_step + k)],
                                                        0, 0, 0, 0), k=k))
                  for k in range(n_pages_step)]
    grid_spec = pltpu.PrefetchScalarGridSpec(
        num_scalar_prefetch=1,
        grid=(nseq, n_pages // n_pages_step),
        in_specs=page_specs + [pl.BlockSpec((None, 1, W_C), lambda b, c, pt: (b, 0, qb))],
        out_specs=pl.BlockSpec((None, 1, W_C), lambda b, c, pt: (b, 0, 0)),
        scratch_shapes=[pltpu.VMEM((H_C, HEAD_DIM, LANES), F32), pltpu.VMEM((H_C, 1), F32),
                        pltpu.VMEM((H_C, HEAD_DIM, LANES), F32)],
    )
    return pl.pallas_call(
        functools.partial(_dec_sb_kernel, n_pages_step=n_pages_step),
        grid_spec=grid_spec,
        out_shape=jax.ShapeDtypeStruct((nseq, 1, W_C), F32),
        compiler_params=_cparams(2),
        name="dec_sb",
    )(page_table, *([cache_sb_t] * n_pages_step), proj_s)


def _overlap(ncp, n_lanes, reps):
    c0 = np.arange(ncp)[:, None] * D_CMP
    s0 = np.arange(n_lanes)[None, :] * L_SEL
    ov = ((c0 < s0 + L_SEL) & (c0 + L_CMP > s0)).astype(np.float32)
    return jnp.asarray(np.tile(ov, (1, reps)), BF16)


def _flat16_t(kt):
    b, _, t = kt.shape
    r = kt.reshape(b, G_B, HEAD_DIM, t // D_CMP, D_CMP).transpose(0, 1, 3, 4, 2)
    return r.reshape(b, G_B, t // D_CMP, D_CMP * HEAD_DIM)


def kernel(x_prompt, x_sample, cache_sb, cache_nsa, state_win, state_conv, page_table, c_prompt, c_sample,
           rel_bias, norm_pre, norm_post, w_ada, b_ada, w_ffn_gate, w_ffn_up, w_ffn_down, w_in, conv_w,
           cmp_pos, w_cmp1, w_cmp2, w_branch, w_out):
    batch, seq, d = x_prompt.shape
    nseq = x_sample.shape[0]
    depth = w_in.shape[0]
    n_pages = page_table.shape[1]
    past = n_pages * PAGE_SIZE
    n_ctx = state_win.shape[2]
    assert x_sample.shape[1] == 1 and seq % Q_BLOCK == 0 and seq // L_SEL <= L_SEL
    assert cache_sb.shape[2] == PAGE_SIZE and n_ctx == WINDOW
    offs, row_part, t_part, n_row, n_all = _layout(d)
    half = D_CMP * HEAD_DIM

    q_names = ('b_q', 'c_q')
    w_row = _gather_cols(w_in, row_part, q_names)
    w_kv = _gather_cols(w_in, t_part, q_names)
    tn = n_row // 2
    tn_s = 2048
    assert tn % LANES == 0
    n_s = -(-n_all // tn_s) * tn_s
    w_r = w_row.astype(BF16)
    w_t = w_kv.transpose(0, 2, 1).astype(BF16)
    w_s = jnp.concatenate([w_row, w_kv, jnp.zeros((depth, d, n_s - n_all), F32)], axis=-1).astype(BF16)
    wg = w_ffn_gate.astype(BF16)
    wu = w_ffn_up.astype(BF16)
    wd = w_ffn_down.astype(BF16)
    wba = w_branch[:, 0:W_A].astype(BF16)
    hb = lambda h: w_branch[:, W_A + h * HEAD_DIM:W_A + (h + 1) * HEAD_DIM]
    wbb = jnp.concatenate([x for p in range(R_B) for x in (hb(p), hb(p + R_B))], axis=1).astype(BF16)
    wbc = w_branch[:, W_A + W_B:].astype(BF16)
    wo = w_out.astype(BF16)
    w1 = w_cmp1.astype(BF16)
    z = jnp.zeros_like(w_cmp2)
    w2p = jnp.stack([jnp.concatenate([w_cmp2, z], axis=-1), jnp.concatenate([z, w_cmp2], axis=-1)],
                    axis=2).astype(BF16)
    pos = cmp_pos.reshape(depth, 2, 2, 1, half)

    rows_c = -(-(batch + nseq) // 16) * 16
    c_all = jnp.concatenate([c_prompt, c_sample, jnp.zeros((rows_c - batch - nseq, d), F32)], axis=0)
    ada_all = _ada_all(c_all, w_ada, b_ada)
    ada_p = ada_all[:, :batch].reshape(depth, batch, 3 * N_SUB, 1, d)
    ada_s = ada_all[:, batch:batch + nseq].reshape(depth, nseq, 3 * N_SUB, d).transpose(0, 2, 1, 3)[:, None]

    tq = Q_BLOCK
    ncp_p = seq // D_CMP
    tab_sel = _bias_table(rel_bias, 4, 2 * tq, tq, base0=0, tile_step=tq, row_step=-1, col_step=-1,
                          heads_on_lanes=True)
    tab_win = _bias_table(rel_bias, 6, 2 * tq, tq, base0=0, tile_step=tq, row_step=-1, col_step=-1,
                          max_dist=WINDOW, heads_on_lanes=True)
    tab_cmp = _bias_table(rel_bias, seq // tq, tq, ncp_p, base0=-(L_CMP - 1), tile_step=tq, row_step=1,
                          col_step=D_CMP)
    ncp_s = past // D_CMP
    tab_cmp_s = _bias_table(rel_bias, 1, 1, ncp_s, base0=past - (L_CMP - 1), tile_step=0, row_step=0,
                            col_step=D_CMP)[0]
    tab_win_s = _bias_table(rel_bias, 1, 1, n_ctx, base0=n_ctx, tile_step=0, row_step=0, col_step=1,
                            max_dist=WINDOW)[0]
    rel_t = rel_bias.T
    ovl_p = _overlap(ncp_p, L_SEL, G_B)
    nsl_s = -(-(past // L_SEL + 1) // LANES) * LANES
    ovl_s = _overlap(ncp_s, nsl_s, 1)
    n_top_p = min(N_SEL, -(-seq // L_SEL))
    n_top_s = min(N_SEL, -(-(past + 1) // L_SEL))

    cache_sb_t = cache_sb.transpose(0, 1, 3, 4, 5, 2)
    cache_nsa_t = cache_nsa.transpose(0, 1, 3, 4, 5, 2)
    win_state_t = state_win.transpose(0, 1, 3, 4, 5, 2)

    tm_p = 512 if seq % 512 == 0 else Q_BLOCK
    tf = w_ffn_gate.shape[-1] // 2

    y_p = x_prompt
    y_s = x_sample.reshape(1, nseq, d)
    outs = {k: [] for k in ('sb_p', 'sb_s', 'nsa_p', 'nsa_s', 'win_p', 'win_s', 'conv_p', 'conv_s')}
    for l in range(depth):
        gpre = [norm_pre[l, i][None] for i in range(N_SUB)]
        gpost = [norm_post[l, i][None] for i in range(N_SUB)]
        y_p = _ffn(y_p, ada_p[l], gpre[0], gpost[0], wg[l, 0], wu[l, 0], wd[l, 0], sub=0, tm=tm_p, tf=tf)
        projr = _proj(y_p, ada_p[l], gpre[1], w_r[l], tm=tm_p, tn=tn)
        sbt, nsat, wint = _proj_t(y_p, ada_p[l], gpre[1], w_t[l], tm=tm_p)
        hrows = jnp.stack([_flat16_t(nsat[:, 0:KV_B]), _flat16_t(nsat[:, KV_B:2 * KV_B])], axis=1)
        kvc = _compress_prompt(hrows, pos[l], w1[l], w2p[l])
        y_cmp, pen = _cmp_prompt(projr, kvc, tab_cmp, ovl_p, offs, n_top=n_top_p)
        y_sel = _nsa_flash_prompt(projr, nsat, tab_sel, pen, offs, mode='sel')
        y_win = _nsa_flash_prompt(projr, wint, tab_win, pen, offs, mode='win')
        y_c = _sb_prompt(projr, sbt, offs)
        y_p, u_tail = _merge(y_p, ada_p[l], gpost[1], projr, None, (y_cmp, y_sel, y_win), y_c, conv_w[l],
                             wba[l], wbb[l], wbc[l], wo[l], offs, tm=min(tm_p, 256), decode=False)
        y_p = _ffn(y_p, ada_p[l], gpre[2], gpost[2], wg[l, 1], wu[l, 1], wd[l, 1], sub=2, tm=tm_p, tf=tf)
        n_keep = min(WINDOW, seq)
        to_state = lambda a, k, g: a.reshape(batch, k, g, HEAD_DIM, a.shape[-1]).transpose(0, 4, 1, 2, 3)
        outs['sb_p'].append(to_state(sbt, 2, H_C))
        outs['nsa_p'].append(to_state(nsat, 4, G_B))
        outs['win_p'].append(to_state(wint[:, :, seq - n_keep:], 2, G_B))
        outs['conv_p'].append(u_tail[:, 8 - (CONV_W - 1):])
        y_s = _ffn(y_s, ada_s[l], gpre[0], gpost[0], wg[l, 0], wu[l, 0], wd[l, 0], sub=0, tm=nseq, tf=tf)
        proj_s = _proj(y_s, ada_s[l], gpre[1], w_s[l], tm=nseq, tn=tn_s)
        proj_r = proj_s.reshape(nseq, 1, n_s)
        s_cmp, top_idx = _dec_cmp(l, page_table, cache_nsa_t, proj_r, pos[l], w1[l], w2p[l], tab_cmp_s, ovl_s,
                                  offs, n_top=n_top_s)
        sel_table = jnp.concatenate([page_table, top_idx[:, 0, 0:n_top_s], top_idx[:, 1, 0:n_top_s]], axis=1)
        sel_table = jnp.pad(sel_table, ((0, 0), (0, -sel_table.shape[1] % LANES)))
        s_sel, s_win = _dec_selwin(l, sel_table, n_pages, cache_nsa_t, proj_r, win_state_t, tab_win_s,
                                   rel_t, offs, n_top=n_top_s)
        s_c = _dec_sb(l, page_table, cache_sb_t, proj_r, offs)
        to_rows = lambda a: a.reshape(1, nseq, a.shape[-1])
        ctx = (state_conv[l, :, 0][None], state_conv[l, :, 1][None])
        y_s, u_s = _merge(y_s, ada_s[l], gpost[1], proj_s, ctx, (to_rows(s_cmp), to_rows(s_sel), to_rows(s_win)),
                          to_rows(s_c), conv_w[l], wba[l], wbb[l], wbc[l], wo[l], offs, tm=nseq, decode=True)
        y_s = _ffn(y_s, ada_s[l], gpre[2], gpost[2], wg[l, 1], wu[l, 1], wd[l, 1], sub=2, tm=nseq, tf=tf)
        ps = proj_s[0]
        o_s, o_n, o_w = offs['sb_s'], offs['nsa_s'], offs['win_s']
        outs['sb_s'].append(ps[:, o_s:o_s + KT_SB].reshape(nseq, 1, 2, H_C, HEAD_DIM))
        outs['nsa_s'].append(ps[:, o_n:o_n + KT_NSA].reshape(nseq, 1, 4, G_B, HEAD_DIM))
        win_new = ps[:, o_w:o_w + KT_WIN].reshape(nseq, 1, 2, G_B, HEAD_DIM)
        outs['win_s'].append(jnp.concatenate([state_win[l][:, 1:], win_new], axis=1))
        outs['conv_s'].append(jnp.stack([state_conv[l, :, 1], u_s[0]], axis=1))
    st = {k: jnp.stack(v) for k, v in outs.items()}
    return (y_p, y_s.reshape(nseq, 1, d), st['sb_p'], st['sb_s'], st['nsa_p'], st['nsa_s'],
            st['win_p'], st['win_s'], st['conv_p'], st['conv_s'])
```

```python
import functools
import math

import numpy as np
import jax
import jax.numpy as jnp
from jax import lax
from jax.experimental import pallas as pl
from jax.experimental.pallas import tpu as pltpu

F32 = jnp.float32
BF16 = jnp.bfloat16

HEAD_DIM = 64
CONV_CH = 512
CONV_W = 3
H_B = 8
G_B = 2
R_B = H_B // G_B
L_CMP = 32
D_CMP = 16
L_SEL = 64
N_SEL = 16
WINDOW = 512
H_C = 8
N_BUCKETS = 32
MAX_DIST = 128
N_SUB = 3
Q_BLOCK = 128
PAGE_SIZE = 128
EPS = 1e-6
NEG = -1e30
MASKED_BELOW = -1e29
SB_UNDERFLOW = -105.0
FORCE_BONUS = 1e4

W_A = CONV_CH
W_B = H_B * HEAD_DIM
W_C = H_C * HEAD_DIM
KV_B = G_B * HEAD_DIM
LANES = 128
VMEM_LIMIT = 56 * 1024 * 1024

KT_SB = 2 * W_C
KT_NSA = 4 * KV_B
KT_WIN = 2 * KV_B


def _layout(d_model):
    sizes = (W_A, W_A, W_A, W_B, KV_B, KV_B, KV_B, KV_B, KV_B, KV_B, 3 * H_B, W_C, W_C, W_C, 3 * d_model)
    starts = np.concatenate([[0], np.cumsum(sizes)]).astype(np.int64)
    names = ('a_v', 'a_b', 'a_c', 'b_q', 'b_kc', 'b_vc', 'b_ks', 'b_vs', 'b_kw', 'b_vw', 'b_g',
             'c_q', 'c_k', 'c_v', 'g_merge')
    seg = {n: (int(starts[i]), int(starts[i + 1])) for i, n in enumerate(names)}
    q0 = seg['b_q'][0]
    head = lambda h: (q0 + h * HEAD_DIM, q0 + (h + 1) * HEAD_DIM)
    bq_pairs = []
    for p in range(R_B):
        bq_pairs += [head(p), head(p + R_B)]
    row_part = [('g_merge', [seg['g_merge']], 0),
                ('conv', [seg['a_v'], seg['a_b'], seg['a_c']], 0),
                ('b_q', bq_pairs, 0),
                ('c_q', [seg['c_q']], 0),
                ('b_g', [seg['b_g']], LANES - 3 * H_B),
                ('k_rows', [seg['b_ks'], seg['b_kw']], LANES)]
    t_part = [('sb', [seg['c_k'], seg['c_v']], 0),
              ('nsa', [seg['b_kc'], seg['b_vc'], seg['b_ks'], seg['b_vs']], 0),
              ('win', [seg['b_kw'], seg['b_vw']], 0)]
    offs, o = {}, 0
    for n, rngs, padn in row_part:
        offs[n] = o
        o += sum(b - a for a, b in rngs) + padn
    n_row = o
    for n, rngs, padn in t_part:
        offs[n + '_s'] = o
        o += sum(b - a for a, b in rngs) + padn
    return offs, row_part, t_part, n_row, o


def _gather_cols(w, parts, scale_names):
    cols = []
    for n, rngs, padn in parts:
        for a, b in rngs:
            c = w[:, :, a:b]
            cols.append(c * (HEAD_DIM ** -0.5) if n in scale_names else c)
        if padn:
            cols.append(jnp.zeros(w.shape[:2] + (padn,), w.dtype))
    return jnp.concatenate(cols, axis=-1)


def _t5_thresholds():
    max_exact = N_BUCKETS // 2
    def bucket(n):
        if n < max_exact:
            return n
        b = max_exact + int(math.log(max(n, 1) / max_exact) / math.log(MAX_DIST / max_exact)
                            * (N_BUCKETS - max_exact))
        return min(b, N_BUCKETS - 1)
    thr = []
    for k in range(1, N_BUCKETS):
        n = 0
        while bucket(n) < k:
            n += 1
        thr.append(n)
    return tuple(thr)


_T5_THR = _t5_thresholds()


def _cparams(n_grid, vmem=VMEM_LIMIT):
    return pltpu.CompilerParams(dimension_semantics=("arbitrary",) * n_grid, vmem_limit_bytes=vmem)


def _dot(a, b):
    return jnp.dot(a, b, preferred_element_type=F32)


def _dot_nt(a, b):
    return lax.dot_general(a, b, (((1,), (1,)), ((), ())), preferred_element_type=F32)


def _rms(x):
    return x * lax.rsqrt(jnp.mean(x * x, axis=-1, keepdims=True) + EPS)


def _silu(x):
    return x * jax.nn.sigmoid(x)


def _lane_iota(shape):
    return lax.broadcasted_iota(jnp.int32, shape, len(shape) - 1)


def _row_iota(shape):
    return lax.broadcasted_iota(jnp.int32, shape, len(shape) - 2)


def _split_bf16(x):
    hi = x.astype(BF16)
    lo = (x - hi.astype(F32)).astype(BF16)
    return hi, lo


def _ada_kernel(c_ref, w_ref, b_ref, o_ref):
    h = _silu(c_ref[...]).astype(BF16)
    o_ref[...] = _dot(h, w_ref[...].astype(BF16)) + b_ref[...]


def _ada_all(c_all, w_ada, b_ada, tn=1536):
    depth, d, n = w_ada.shape
    rows = c_all.shape[0]
    return pl.pallas_call(
        _ada_kernel,
        grid=(depth, n // tn),
        in_specs=[pl.BlockSpec((rows, d), lambda l, j: (0, 0)),
                  pl.BlockSpec((None, d, tn), lambda l, j: (l, 0, j)),
                  pl.BlockSpec((None, 1, tn), lambda l, j: (l, 0, j))],
        out_specs=pl.BlockSpec((None, rows, tn), lambda l, j: (l, 0, j)),
        out_shape=jax.ShapeDtypeStruct((depth, rows, n), F32),
        compiler_params=_cparams(2),
        name="ada",
    )(c_all, w_ada, b_ada.reshape(depth, 1, n))


def _table_kernel(rel_ref, o_ref, *, base0, tile_step, row_step, col_step, max_dist):
    t = pl.program_id(0)
    h = pl.program_id(1)
    shape = o_ref.shape
    dist = (base0 + t * tile_step + _row_iota(shape) * row_step - _lane_iota(shape) * col_step)
    b = jnp.full(shape, rel_ref[0, h], F32)
    for k, thr in enumerate(_T5_THR):
        b = jnp.where(dist >= thr, rel_ref[k + 1, h], b)
    ok = dist >= 0
    if max_dist is not None:
        ok = ok & (dist <= max_dist)
    o_ref[...] = jnp.where(ok, b, NEG)


def _bias_table(rel_bias, n_tiles, rows, cols, *, base0, tile_step, row_step, col_step, max_dist=None,
                heads_on_lanes=False):
    n_heads = rel_bias.shape[1]
    if heads_on_lanes:
        return pl.pallas_call(
            functools.partial(_table_kernel, base0=base0, tile_step=tile_step, row_step=row_step,
                              col_step=col_step, max_dist=max_dist),
            grid=(n_tiles, n_heads),
            in_specs=[pl.BlockSpec(memory_space=pltpu.SMEM)],
            out_specs=pl.BlockSpec((None, rows, cols), lambda t, h: (t, 0, h)),
            out_shape=jax.ShapeDtypeStruct((n_tiles, rows, n_heads * cols), F32),
            compiler_params=_cparams(2),
            name="bias_table",
        )(rel_bias)
    out = pl.pallas_call(
        functools.partial(_table_kernel, base0=base0, tile_step=tile_step, row_step=row_step,
                          col_step=col_step, max_dist=max_dist),
        grid=(n_tiles, n_heads),
        in_specs=[pl.BlockSpec(memory_space=pltpu.SMEM)],
        out_specs=pl.BlockSpec((None, None, rows, cols), lambda t, h: (t, h, 0, 0)),
        out_shape=jax.ShapeDtypeStruct((n_tiles, n_heads, rows, cols), F32),
        compiler_params=_cparams(2),
        name="bias_table",
    )(rel_bias)
    return out.reshape(n_tiles, n_heads * rows, cols)


def _ffn_kernel(x_ref, ada_ref, gpre_ref, gpost_ref, wg_ref, wu_ref, wd_ref, o_ref, h_ref, acc_ref, *, sub):
    f = pl.program_id(2)

    @pl.when(f == 0)
    def _():
        h = _rms(x_ref[...]) * gpre_ref[...]
        h = h * (1.0 + ada_ref[3 * sub + 1]) + ada_ref[3 * sub]
        h_ref[...] = h.astype(BF16)
        acc_ref[...] = jnp.zeros_like(acc_ref)

    h = h_ref[...]
    a = _silu(_dot(h, wg_ref[...])) * _dot(h, wu_ref[...])
    acc_ref[...] += _dot(a.astype(BF16), wd_ref[...])

    @pl.when(f == pl.num_programs(2) - 1)
    def _():
        y = _rms(acc_ref[...]) * gpost_ref[...]
        o_ref[...] = x_ref[...] + 0.5 * (ada_ref[3 * sub + 2] * y)


def _ffn(x, ada, gpre, gpost, wg, wu, wd, *, sub, tm, tf):
    bg, t, d = x.shape
    f = wg.shape[1]
    mrows = ada.shape[2]
    return pl.pallas_call(
        functools.partial(_ffn_kernel, sub=sub),
        grid=(bg, t // tm, f // tf),
        in_specs=[pl.BlockSpec((None, tm, d), lambda b, i, j: (b, i, 0)),
                  pl.BlockSpec((None, 3 * N_SUB, mrows, d), lambda b, i, j: (b, 0, 0, 0)),
                  pl.BlockSpec((1, d), lambda b, i, j: (0, 0)),
                  pl.BlockSpec((1, d), lambda b, i, j: (0, 0)),
                  pl.BlockSpec((d, tf), lambda b, i, j: (0, j)),
                  pl.BlockSpec((d, tf), lambda b, i, j: (0, j)),
                  pl.BlockSpec((tf, d), lambda b, i, j: (j, 0))],
        out_specs=pl.BlockSpec((None, tm, d), lambda b, i, j: (b, i, 0)),
        out_shape=jax.ShapeDtypeStruct(x.shape, F32),
        scratch_shapes=[pltpu.VMEM((tm, d), BF16), pltpu.VMEM((tm, d), F32)],
        compiler_params=_cparams(3),
        name="ffn",
    )(x, ada, gpre, gpost, wg, wu, wd)


def _mod1(x_ref, ada_ref, gpre_ref):
    h = _rms(x_ref[...]) * gpre_ref[...]
    return (h * (1.0 + ada_ref[4]) + ada_ref[3]).astype(BF16)


def _proj_kernel(x_ref, ada_ref, gpre_ref, w_ref, o_ref, h_ref):
    @pl.when(pl.program_id(2) == 0)
    def _():
        h_ref[...] = _mod1(x_ref, ada_ref, gpre_ref)

    o_ref[...] = _dot(h_ref[...], w_ref[...])


def _proj(x, ada, gpre, w, *, tm, tn):
    bg, t, d = x.shape
    n = w.shape[1]
    mrows = ada.shape[2]
    return pl.pallas_call(
        _proj_kernel,
        grid=(bg, t // tm, n // tn),
        in_specs=[pl.BlockSpec((None, tm, d), lambda b, i, j: (b, i, 0)),
                  pl.BlockSpec((None, 3 * N_SUB, mrows, d), lambda b, i, j: (b, 0, 0, 0)),
                  pl.BlockSpec((1, d), lambda b, i, j: (0, 0)),
                  pl.BlockSpec((d, tn), lambda b, i, j: (0, j))],
        out_specs=pl.BlockSpec((None, tm, tn), lambda b, i, j: (b, i, j)),
        out_shape=jax.ShapeDtypeStruct((bg, t, n), F32),
        scratch_shapes=[pltpu.VMEM((tm, d), BF16)],
        compiler_params=_cparams(3),
        name="proj",
    )(x, ada, gpre, w)


def _proj_t_kernel(x_ref, ada_ref, gpre_ref, w_ref, sb_ref, nsa_ref, win_ref):
    h = _mod1(x_ref, ada_ref, gpre_ref)
    kt = _dot_nt(w_ref[...], h)
    sb_ref[...] = kt[0:KT_SB]
    nsa_ref[...] = kt[KT_SB:KT_SB + KT_NSA]
    win_ref[...] = kt[KT_SB + KT_NSA:KT_SB + KT_NSA + KT_WIN]


def _proj_t(x, ada, gpre, w_t, *, tm):
    bg, t, d = x.shape
    nf = w_t.shape[0]
    out = lambda rows: (pl.BlockSpec((None, rows, tm), lambda b, i: (b, 0, i)),
                        jax.ShapeDtypeStruct((bg, rows, t), F32))
    specs, shapes = zip(out(KT_SB), out(KT_NSA), out(KT_WIN))
    return pl.pallas_call(
        _proj_t_kernel,
        grid=(bg, t // tm),
        in_specs=[pl.BlockSpec((None, tm, d), lambda b, i: (b, i, 0)),
                  pl.BlockSpec((None, 3 * N_SUB, 1, d), lambda b, i: (b, 0, 0, 0)),
                  pl.BlockSpec((1, d), lambda b, i: (0, 0)),
                  pl.BlockSpec((nf, d), lambda b, i: (0, 0))],
        out_specs=list(specs),
        out_shape=list(shapes),
        compiler_params=_cparams(2),
        name="proj_t",
    )(x, ada, gpre, w_t)


def _compress_kernel(h_ref, pos_ref, w1_ref, w2_ref, o_ref):
    ncp = h_ref.shape[1]
    half = h_ref.shape[2]
    out = None
    for g in range(G_B):
        hg = h_ref[g]
        a = _dot((hg + pos_ref[0]).astype(BF16), w1_ref[0:half, :])
        bm = _dot((hg + pos_ref[1]).astype(BF16), w1_ref[half:2 * half, :])
        pre = a + pltpu.roll(bm, ncp - 1, 0)
        o = _dot(_silu(pre).astype(BF16), w2_ref[g])
        out = o if out is None else out + o
    o_ref[...] = out


def _compress_prompt(hrows, pos, w1, w2p):
    b, _, g, ncp, half = hrows.shape
    hid = w1.shape[2]
    return pl.pallas_call(
        _compress_kernel,
        grid=(b, 2),
        in_specs=[pl.BlockSpec((None, None, g, ncp, half), lambda i, k: (i, k, 0, 0, 0)),
                  pl.BlockSpec((None, 2, 1, half), lambda i, k: (k, 0, 0, 0)),
                  pl.BlockSpec((None, 2 * half, hid), lambda i, k: (k, 0, 0)),
                  pl.BlockSpec((None, g, hid, LANES), lambda i, k: (k, 0, 0, 0))],
        out_specs=pl.BlockSpec((None, None, ncp, LANES), lambda i, k: (i, k, 0, 0)),
        out_shape=jax.ShapeDtypeStruct((b, 2, ncp, LANES), F32),
        compiler_params=_cparams(2),
        name="compress",
    )(hrows, pos, w1, w2p)


def _stack_heads(blk_fn, tq):
    lane = _lane_iota((tq, LANES))
    lo = [jnp.where(lane < HEAD_DIM, blk_fn(p), 0.0) for p in range(R_B)]
    hi = [jnp.where(lane >= HEAD_DIM, blk_fn(p), 0.0) for p in range(R_B)]
    return jnp.concatenate(lo + hi, axis=0)


def _unstack_gated(o, gsig, branch, tq):
    lane = _lane_iota((tq, LANES))
    out = []
    for p in range(R_B):
        c_lo = 3 * p + branch
        c_hi = 3 * (p + R_B) + branch
        blk = jnp.where(lane < HEAD_DIM, o[p * tq:(p + 1) * tq], o[(p + R_B) * tq:(p + R_B + 1) * tq])
        gate = jnp.where(lane < HEAD_DIM, gsig[:, c_lo:c_lo + 1], gsig[:, c_hi:c_hi + 1])
        out.append(blk * gate)
    return out


def _topk_penalty(score_t, n_top):
    nb = score_t.shape[0]
    jrow = _row_iota(score_t.shape).astype(F32)
    x = score_t
    sel = jnp.zeros(score_t.shape, F32)
    for _ in range(n_top):
        mx = jnp.max(x, axis=0, keepdims=True)
        first = jnp.min(jnp.where(x == mx, jrow, float(nb)), axis=0, keepdims=True)
        hit = jrow == first
        sel = jnp.where(hit, 1.0, sel)
        x = jnp.where(hit, -jnp.inf, x)
    return jnp.where(sel > 0.5, 0.0, NEG)


def _cmp_kernel(q_ref, kc_ref, vc_ref, tab_ref, ovl_ref, bg_ref, o_ref, pen_ref, *, n_top):
    tq = q_ref.shape[0]
    i = pl.program_id(1)
    q8 = _stack_heads(lambda p: q_ref[:, p * LANES:(p + 1) * LANES], tq).astype(BF16)
    s = _dot_nt(q8, kc_ref[...].astype(BF16)) + tab_ref[...]
    ok = s > MASKED_BELOW
    e = jnp.where(ok, jnp.exp(s - jnp.max(s, axis=-1, keepdims=True)), 0.0)
    p = (e / jnp.maximum(jnp.sum(e, axis=-1, keepdims=True), 1e-30)).astype(BF16)
    o = _dot(p, vc_ref[...].astype(BF16))
    gsig = jax.nn.sigmoid(bg_ref[...])
    for pi, blk in enumerate(_unstack_gated(o, gsig, 0, tq)):
        o_ref[:, pi * LANES:(pi + 1) * LANES] = blk
    imp8 = _dot(p, ovl_ref[...])
    lane = _lane_iota((tq, LANES))
    g0 = imp8[0:tq] + imp8[tq:2 * tq] + imp8[2 * tq:3 * tq] + imp8[3 * tq:4 * tq]
    g1 = imp8[4 * tq:5 * tq] + imp8[5 * tq:6 * tq] + imp8[6 * tq:7 * tq] + imp8[7 * tq:8 * tq]
    imp = jnp.where(lane < L_SEL, g0, g1)
    qpos = i * tq + _row_iota((tq, LANES))
    j = lane % L_SEL
    cur = qpos // L_SEL
    forced = (j == 0) | (j == cur) | (j == cur - 1)
    score = jnp.where(j * L_SEL <= qpos, imp + FORCE_BONUS * forced.astype(F32), NEG)
    st = score.T
    pen_t = jnp.concatenate([_topk_penalty(st[g * L_SEL:(g + 1) * L_SEL], n_top) for g in range(G_B)], axis=0)
    pen_ref[...] = pen_t


def _cmp_prompt(projr, kvc, tab, ovl, offs, *, n_top, tq=Q_BLOCK):
    b, t, _ = projr.shape
    ncp = kvc.shape[2]
    qb = offs['b_q'] // W_B
    gb = offs['b_g'] // LANES
    return pl.pallas_call(
        functools.partial(_cmp_kernel, n_top=n_top),
        grid=(b, t // tq),
        in_specs=[pl.BlockSpec((None, tq, W_B), lambda bi, i: (bi, i, qb)),
                  pl.BlockSpec((None, None, ncp, LANES), lambda bi, i: (bi, 0, 0, 0)),
                  pl.BlockSpec((None, None, ncp, LANES), lambda bi, i: (bi, 1, 0, 0)),
                  pl.BlockSpec((None, H_B * tq, ncp), lambda bi, i: (i, 0, 0)),
                  pl.BlockSpec((ncp, LANES), lambda bi, i: (0, 0)),
                  pl.BlockSpec((None, tq, LANES), lambda bi, i: (bi, i, gb))],
        out_specs=[pl.BlockSpec((None, tq, W_B), lambda bi, i: (bi, i, 0)),
                   pl.BlockSpec((None, LANES, tq), lambda bi, i: (bi, 0, i))],
        out_shape=[jax.ShapeDtypeStruct((b, t, W_B), F32), jax.ShapeDtypeStruct((b, LANES, t), F32)],
        compiler_params=_cparams(2),
        name="nsa_cmp",
    )(projr, kvc, kvc, tab, ovl, projr)


def _nsa_flash_kernel(q_ref, k_ref, v_ref, tab_ref, pen_ref, bg_ref, o_ref, qa_ref, m_ref, l_ref, acc_ref, *,
                      mode, branch):
    tq = q_ref.shape[0]
    tk = 2 * tq
    i = pl.program_id(1)
    jd = i // 2
    par = i % 2

    row = _row_iota((LANES, tq))
    for p in range(R_B):
        qt = q_ref[:, p * LANES:(p + 1) * LANES].T
        qa_ref[0:LANES, p * tq:(p + 1) * tq] = jnp.where(row < HEAD_DIM, qt, 0.0).astype(BF16)
        qa_ref[0:LANES, (p + R_B) * tq:(p + R_B + 1) * tq] = jnp.where(row >= HEAD_DIM, qt, 0.0).astype(BF16)
    if mode == 'sel':
        pen = pen_ref[...]
        lo = jnp.where(row < L_SEL, pen, 0.0).astype(BF16)
        hi = jnp.where(row >= L_SEL, pen, 0.0).astype(BF16)
        qa_ref[LANES:2 * LANES, :] = jnp.concatenate([lo] * R_B + [hi] * R_B, axis=1)
    m_ref[...] = jnp.full_like(m_ref, NEG)
    l_ref[...] = jnp.zeros_like(l_ref)
    acc_ref[...] = jnp.zeros_like(acc_ref)

    def step(j, bias, uniform_bias=False):
        k0 = pl.multiple_of(j * tk, tk)
        kt = k_ref[pl.ds(k0, tk), :].astype(BF16)
        if mode == 'sel':
            blk = (tk // L_SEL) * j + _row_iota((tk, LANES)) // L_SEL
            et = jnp.where(blk == _lane_iota((tk, LANES)) % L_SEL, 1.0, 0.0).astype(BF16)
            kt = jnp.concatenate([kt, et], axis=1)
        s = _dot(kt, qa_ref[...])
        m_old = m_ref[...]
        if uniform_bias:
            m_new = jnp.maximum(m_old, jnp.max(s, axis=0, keepdims=True) + bias)
            p = jnp.exp(s - (m_new - bias))
        else:
            s = s + bias
            m_new = jnp.maximum(m_old, jnp.max(s, axis=0, keepdims=True))
            p = jnp.exp(s - m_new)
        alpha = jnp.exp(m_old - m_new)
        l_ref[...] = alpha * l_ref[...] + jnp.sum(p, axis=0, keepdims=True)
        acc_ref[...] = alpha * acc_ref[...] + _dot(v_ref[:, pl.ds(k0, tk)].astype(BF16), p.astype(BF16))
        m_ref[...] = m_new

    step(jd, tab_ref[par])
    n_near = tab_ref.shape[0] // 2
    for dj in range(1, n_near):
        @pl.when(jd >= dj)
        def _(dj=dj):
            step(jd - dj, tab_ref[par + 2 * dj])

    if mode == 'sel':
        far_bias = tab_ref[3, 0:1, :]

        def far_body(j, c):
            step(j, far_bias, uniform_bias=True)
            return c

        lax.fori_loop(0, jnp.maximum(jd - (n_near - 1), 0), far_body, 0)

    ot = acc_ref[...] / jnp.maximum(l_ref[...], 1e-30)
    o = jnp.concatenate([ot[:, h * tq:(h + 1) * tq].T for h in range(H_B)], axis=0)
    gsig = jax.nn.sigmoid(bg_ref[...])
    for pi, blk in enumerate(_unstack_gated(o, gsig, branch, tq)):
        o_ref[:, pi * LANES:(pi + 1) * LANES] = blk


def _nsa_flash_prompt(projr, vt, tab, pen_t, offs, *, mode, tq=Q_BLOCK):
    b, t, _ = projr.shape
    assert t % (2 * tq) == 0 and WINDOW == 4 * tq and tq == LANES
    qb = offs['b_q'] // W_B
    gb = offs['b_g'] // LANES
    kb = offs['k_rows'] // LANES
    if mode == 'sel':
        vb, branch, qa_rows = 3, 1, 2 * LANES
    else:
        kb, vb, branch, qa_rows = kb + 1, 1, 2, LANES
    return pl.pallas_call(
        functools.partial(_nsa_flash_kernel, mode=mode, branch=branch),
        grid=(b, t // tq),
        in_specs=[pl.BlockSpec((None, tq, W_B), lambda bi, i: (bi, i, qb)),
                  pl.BlockSpec((None, t, KV_B), lambda bi, i: (bi, 0, kb)),
                  pl.BlockSpec((None, KV_B, t), lambda bi, i: (bi, vb, 0)),
                  pl.BlockSpec(tab.shape, lambda bi, i: (0, 0, 0)),
                  pl.BlockSpec((None, LANES, tq), lambda bi, i: (bi, 0, i)),
                  pl.BlockSpec((None, tq, LANES), lambda bi, i: (bi, i, gb))],
        out_specs=pl.BlockSpec((None, tq, W_B), lambda bi, i: (bi, i, 0)),
        out_shape=jax.ShapeDtypeStruct((b, t, W_B), F32),
        scratch_shapes=[pltpu.VMEM((qa_rows, H_B * tq), BF16),
                        pltpu.VMEM((1, H_B * tq), F32),
                        pltpu.VMEM((1, H_B * tq), F32),
                        pltpu.VMEM((KV_B, H_B * tq), F32)],
        compiler_params=_cparams(2),
        name="nsa_" + mode,
    )(projr, projr, vt, tab, pen_t, projr)


def _sb_kernel(q_ref, k_ref, v_ref, o_ref, qs_ref, carry_ref, acc_ref):
    tq = q_ref.shape[0]
    tk = tq
    i = pl.program_id(2)
    lane = _lane_iota((tq, LANES))
    q = q_ref[...]
    qs_ref[...] = jnp.concatenate([jnp.where(lane < HEAD_DIM, q, 0.0),
                                   jnp.where(lane >= HEAD_DIM, q, 0.0)], axis=0).astype(BF16)
    carry_ref[...] = jnp.zeros_like(carry_ref)
    acc_ref[...] = jnp.zeros_like(acc_ref)
    jj = _row_iota((2 * tk, tk))
    tri2 = jnp.where(jnp.where(jj >= tk, jj - tk, jj) > _lane_iota((2 * tk, tk)), 1.0, 0.0).astype(BF16)

    def step(j, diagonal):
        k0 = pl.multiple_of(j * tk, tk)
        z = _dot(qs_ref[...], k_ref[:, pl.ds(k0, tk)].astype(BF16))
        t = jnp.log(1.0 + jnp.exp(-jnp.abs(z)))
        log_1mb = -(jnp.maximum(z, 0.0) + t)
        log_b = z + log_1mb
        if diagonal:
            rr = _row_iota((2 * tq, tk))
            causal = _lane_iota((2 * tq, tk)) < jnp.where(rr >= tq, rr - tq, rr)
            log_1mb = jnp.where(causal, log_1mb, 0.0)
        hi, lo = _split_bf16(log_1mb)
        after = _dot(jnp.concatenate([hi, lo], axis=1), tri2) + carry_ref[...]
        a = jnp.exp(log_b + after)
        if diagonal:
            a = jnp.where(causal, a, 0.0)
        carry_ref[...] += jnp.sum(log_1mb, axis=-1, keepdims=True)
        acc_ref[...] += _dot_nt(a.astype(BF16), v_ref[:, pl.ds(k0, tk)].astype(BF16))

    step(i, True)

    def still_visible():
        return jnp.max(carry_ref[...]) > SB_UNDERFLOW

    def body(c):
        j, _ = c
        step(j, False)
        return j - 1, still_visible()

    lax.while_loop(lambda c: jnp.logical_and(c[0] >= 0, c[1]), body, (i - 1, still_visible()))
    acc = acc_ref[...]
    o_ref[...] = jnp.where(lane < HEAD_DIM, acc[0:tq], acc[tq:2 * tq])


def _sb_prompt(projr, sbt, offs, *, tq=2 * Q_BLOCK):
    b, t, _ = projr.shape
    n_pairs = W_C // LANES
    qb = offs['c_q'] // LANES
    return pl.pallas_call(
        _sb_kernel,
        grid=(b, n_pairs, t // tq),
        in_specs=[pl.BlockSpec((None, tq, LANES), lambda bi, p, i: (bi, i, qb + p)),
                  pl.BlockSpec((None, LANES, t), lambda bi, p, i: (bi, p, 0)),
                  pl.BlockSpec((None, LANES, t), lambda bi, p, i: (bi, n_pairs + p, 0))],
        out_specs=pl.BlockSpec((None, tq, LANES), lambda bi, p, i: (bi, i, p)),
        out_shape=jax.ShapeDtypeStruct((b, t, W_C), F32),
        scratch_shapes=[pltpu.VMEM((2 * tq, LANES), BF16),
                        pltpu.VMEM((2 * tq, 1), F32),
                        pltpu.VMEM((2 * tq, LANES), F32)],
        compiler_params=_cparams(3),
        name="sb",
    )(projr, sbt, sbt)


def _merge_kernel(*refs, decode):
    if decode:
        (x_ref, ada_ref, gpost_ref, conv_ref, s0_ref, s1_ref, gm0_ref, gm1_ref, gm2_ref, yb0_ref, yb1_ref,
         yb2_ref, yc_ref, cw_ref, wba_ref, wbb_ref, wbc_ref, wo_ref, o_ref, u_ref) = refs
    else:
        (x_ref, ada_ref, gpost_ref, conv_ref, halo_ref, gm0_ref, gm1_ref, gm2_ref, yb0_ref, yb1_ref, yb2_ref,
         yc_ref, cw_ref, wba_ref, wbb_ref, wbc_ref, wo_ref, o_ref, u_ref) = refs
    tm = x_ref.shape[0]
    u = conv_ref[:, 2 * W_A:3 * W_A] * conv_ref[:, 0:W_A]
    if decode:
        um2 = s0_ref[...]
        um1 = s1_ref[...]
        u_ref[...] = u
    else:
        i = pl.program_id(1)
        keep = (i > 0).astype(F32)
        hu = (halo_ref[:, 2 * W_A:3 * W_A] * halo_ref[:, 0:W_A]) * keep
        row = _row_iota((tm, W_A))
        um1 = jnp.where(row >= 1, pltpu.roll(u, 1, 0), hu[7:8])
        um2 = jnp.where(row >= 2, pltpu.roll(u, 2, 0), jnp.where(row == 1, hu[7:8], hu[6:7]))

        @pl.when(i == pl.num_programs(1) - 1)
        def _():
            u_ref[...] = u[tm - 8:tm]

    y_a = conv_ref[:, W_A:2 * W_A] * (cw_ref[0:1] * um2 + cw_ref[1:2] * um1 + cw_ref[2:3] * u)
    y_b = yb0_ref[...] + yb1_ref[...] + yb2_ref[...]
    merged = (jax.nn.sigmoid(gm0_ref[...]) * _dot(y_a.astype(BF16), wba_ref[...])
              + jax.nn.sigmoid(gm1_ref[...]) * _dot(y_b.astype(BF16), wbb_ref[...])
              + jax.nn.sigmoid(gm2_ref[...]) * _dot(yc_ref[...].astype(BF16), wbc_ref[...]))
    out = _dot(merged.astype(BF16), wo_ref[...])
    o_ref[...] = x_ref[...] + ada_ref[5] * (_rms(out) * gpost_ref[...])


def _merge(x, ada, gpost, projr, conv_state, yb, yc, cw, wba, wbb, wbc, wo, offs, *, tm, decode):
    bg, t, d = x.shape
    mrows = ada.shape[2]
    cb = offs['conv'] // (3 * W_A)
    gmb = offs['g_merge'] // d
    row_spec = lambda w, cblk: pl.BlockSpec((None, tm, w), lambda b, i: (b, i, cblk))
    full = lambda a: pl.BlockSpec(a.shape, lambda b, i: (0,) * a.ndim)
    if decode:
        ctx_specs = [row_spec(W_A, 0), row_spec(W_A, 0)]
        ctx = list(conv_state)
        u_shape, u_spec = (bg, t, W_A), row_spec(W_A, 0)
    else:
        r8 = tm // 8
        ctx_specs = [pl.BlockSpec((None, 8, 3 * W_A), lambda b, i: (b, jnp.maximum(i * r8 - 1, 0), cb))]
        ctx = [projr]
        u_shape, u_spec = (bg, 8, W_A), pl.BlockSpec((None, 8, W_A), lambda b, i: (b, 0, 0))
    out, u = pl.pallas_call(
        functools.partial(_merge_kernel, decode=decode),
        grid=(bg, t // tm),
        in_specs=[row_spec(d, 0),
                  pl.BlockSpec((None, 3 * N_SUB, mrows, d), lambda b, i: (b, 0, 0, 0)),
                  pl.BlockSpec((1, d), lambda b, i: (0, 0)),
                  row_spec(3 * W_A, cb)] + ctx_specs + [
                  row_spec(d, gmb), row_spec(d, gmb + 1), row_spec(d, gmb + 2),
                  row_spec(W_B, 0), row_spec(W_B, 0), row_spec(W_B, 0), row_spec(W_C, 0),
                  full(cw), full(wba), full(wbb), full(wbc), full(wo)],
        out_specs=[row_spec(d, 0), u_spec],
        out_shape=[jax.ShapeDtypeStruct(x.shape, F32), jax.ShapeDtypeStruct(u_shape, F32)],
        compiler_params=_cparams(2),
        name="merge",
    )(x, ada, gpost, projr, *ctx, projr, projr, projr, yb[0], yb[1], yb[2], yc, cw, wba, wbb, wbc, wo)
    return out, u


def _dec_cmp_kernel(pt_ref, *refs, n_pages_step, n_top, past):
    del pt_ref
    pages = refs[:n_pages_step]
    (q_ref, pos_ref, w1_ref, w2_ref, tab_ref, ovl_ref, bg_ref, o_ref, idx_ref, h_ref) = refs[n_pages_step:]
    c = pl.program_id(1)
    lane = _lane_iota((8, LANES))
    rpp = PAGE_SIZE // D_CMP
    ii = _row_iota((PAGE_SIZE, PAGE_SIZE))
    perm = jnp.where(_lane_iota((PAGE_SIZE, PAGE_SIZE)) == D_CMP * (ii % rpp) + ii // rpp, 1.0, 0.0).astype(BF16)
    for k in range(n_pages_step):
        row0 = pl.multiple_of((c * n_pages_step + k) * rpp, rpp)
        xt = pages[k][...].reshape(2 * KV_B, PAGE_SIZE).astype(BF16)
        rows = _dot_nt(perm, xt)
        for m in range(D_CMP // 2):
            ev = rows[2 * m * rpp:(2 * m + 1) * rpp]
            od = rows[(2 * m + 1) * rpp:(2 * m + 2) * rpp]
            for kind in range(2):
                e = ev[:, kind * LANES:(kind + 1) * LANES]
                o = od[:, kind * LANES:(kind + 1) * LANES]
                h_ref[kind, 0, pl.ds(row0, rpp), m * LANES:(m + 1) * LANES] = (
                    jnp.where(lane < HEAD_DIM, e, pltpu.roll(o, HEAD_DIM, 1)))
                h_ref[kind, 1, pl.ds(row0, rpp), m * LANES:(m + 1) * LANES] = (
                    jnp.where(lane < HEAD_DIM, pltpu.roll(e, HEAD_DIM, 1), o))

    @pl.when(c == pl.num_programs(1) - 1)
    def _():
        ncp = h_ref.shape[2]
        half = h_ref.shape[3]
        kv = []
        for kind in range(2):
            out = None
            for g in range(G_B):
                hg = h_ref[kind, g]
                a = _dot((hg + pos_ref[kind, 0]).astype(BF16), w1_ref[kind, 0:half, :])
                bm = _dot((hg + pos_ref[kind, 1]).astype(BF16), w1_ref[kind, half:2 * half, :])
                pre = a + pltpu.roll(bm, ncp - 1, 0)
                o = _dot(_silu(pre).astype(BF16), w2_ref[kind, g])
                out = o if out is None else out + o
            kv.append(out.astype(BF16))
        q8 = _stack_heads(lambda p: q_ref[:, p * LANES:(p + 1) * LANES], 1).astype(BF16)
        s = _dot_nt(q8, kv[0]) + tab_ref[...]
        ok = s > MASKED_BELOW
        e = jnp.where(ok, jnp.exp(s - jnp.max(s, axis=-1, keepdims=True)), 0.0)
        p = (e / jnp.maximum(jnp.sum(e, axis=-1, keepdims=True), 1e-30)).astype(BF16)
        o = _dot(p, kv[1])
        gsig = jax.nn.sigmoid(bg_ref[...])
        for pi, blk in enumerate(_unstack_gated(o, gsig, 0, 1)):
            o_ref[:, pi * LANES:(pi + 1) * LANES] = blk
        imp8 = _dot(p, ovl_ref[...])
        imp = jnp.concatenate([imp8[0:1] + imp8[1:2] + imp8[2:3] + imp8[3:4],
                               imp8[4:5] + imp8[5:6] + imp8[6:7] + imp8[7:8]], axis=0)
        nbl = imp.shape[1]
        j = _lane_iota(imp.shape)
        cur = past // L_SEL
        forced = (j == 0) | (j == cur) | (j == cur - 1)
        x = jnp.where(j * L_SEL <= past, imp + FORCE_BONUS * forced.astype(F32), NEG)
        jf = j.astype(F32)
        slot = _lane_iota((G_B, LANES))
        idx = jnp.zeros((G_B, LANES), F32)
        for it in range(n_top):
            mx = jnp.max(x, axis=-1, keepdims=True)
            first = jnp.min(jnp.where(x == mx, jf, float(nbl)), axis=-1, keepdims=True)
            idx = jnp.where(slot == it, first, idx)
            x = jnp.where(jf == first, -jnp.inf, x)
        idx_ref[...] = jnp.zeros(idx_ref.shape, jnp.int32)
        idx_ref[0:G_B, :] = idx.astype(jnp.int32)


def _dec_cmp(layer, page_table, cache_nsa_t, proj_s, pos, w1, w2p, tab, ovl, offs, *, n_top, n_pages_step=8):
    nseq, n_pages = page_table.shape
    past = n_pages * PAGE_SIZE
    ncp = past // D_CMP
    half = D_CMP * HEAD_DIM
    qb = offs['b_q'] // W_B
    gb = offs['b_g'] // LANES
    page_specs = [pl.BlockSpec((None, None, 2, G_B, HEAD_DIM, PAGE_SIZE),
                               functools.partial(
                                   lambda b, c, pt, k: (layer, pt[b, c * n_pages_step + k], 0, 0, 0, 0), k=k))
                  for k in range(n_pages_step)]
    full = lambda a: pl.BlockSpec(a.shape, lambda b, c, pt: (0,) * a.ndim)
    grid_spec = pltpu.PrefetchScalarGridSpec(
        num_scalar_prefetch=1,
        grid=(nseq, n_pages // n_pages_step),
        in_specs=page_specs + [pl.BlockSpec((None, 1, W_B), lambda b, c, pt: (b, 0, qb)),
                               full(pos), full(w1), full(w2p), full(tab), full(ovl),
                               pl.BlockSpec((None, 1, LANES), lambda b, c, pt: (b, 0, gb))],
        out_specs=[pl.BlockSpec((None, 1, W_B), lambda b, c, pt: (b, 0, 0)),
                   pl.BlockSpec((None, 8, LANES), lambda b, c, pt: (b, 0, 0))],
        scratch_shapes=[pltpu.VMEM((2, G_B, ncp, half), F32)],
    )
    return pl.pallas_call(
        functools.partial(_dec_cmp_kernel, n_pages_step=n_pages_step, n_top=n_top, past=past),
        grid_spec=grid_spec,
        out_shape=[jax.ShapeDtypeStruct((nseq, 1, W_B), F32), jax.ShapeDtypeStruct((nseq, 8, LANES), jnp.int32)],
        compiler_params=_cparams(2),
        name="dec_cmp",
    )(page_table, *([cache_nsa_t] * n_pages_step), proj_s, pos, w1, w2p, tab, ovl, proj_s)


def _dec_selwin_kernel(sp_ref, *refs, n_top, past, n_pages):
    n_blk = G_B * n_top
    blocks = refs[:n_blk]
    (q_ref, ksn_ref, vsn_ref, wst_ref, kwn_ref, vwn_ref, tabw_ref, relt_ref, bg_ref,
     osel_ref, owin_ref, kc_ref, vc_ref) = refs[n_blk:]
    b = pl.program_id(0)
    n_past_blk = past // L_SEL
    halves = PAGE_SIZE // L_SEL
    q8f = _stack_heads(lambda p: q_ref[:, p * LANES:(p + 1) * LANES], 1)
    q8 = q8f.astype(BF16)
    q8r = q8.astype(F32)
    gsig = jax.nn.sigmoid(bg_ref[...])
    row1 = _row_iota((H_B, 1))

    for n in range(n_blk):
        kc_ref[:, n * PAGE_SIZE:(n + 1) * PAGE_SIZE] = blocks[n][0].reshape(KV_B, PAGE_SIZE).astype(BF16)
        vc_ref[:, n * PAGE_SIZE:(n + 1) * PAGE_SIZE] = blocks[n][1].reshape(KV_B, PAGE_SIZE).astype(BF16)
    nk = n_blk * PAGE_SIZE
    s = _dot(q8, kc_ref[...])
    lane = _lane_iota((H_B, nk))
    slot = lane // PAGE_SIZE
    within = lane % PAGE_SIZE
    blk_of = jnp.zeros((H_B, nk), jnp.int32)
    has_new = [jnp.int32(0)] * G_B
    for n in range(n_blk):
        bi = sp_ref[b, n_pages + n]
        blk_of = jnp.where(slot == n, bi, blk_of)
        has_new[n // n_top] = jnp.maximum(has_new[n // n_top], (bi == n_past_blk).astype(jnp.int32))
    kpos = (blk_of // halves) * PAGE_SIZE + within
    dist = past - kpos
    row = _row_iota((H_B, nk))
    ok = ((within // L_SEL == blk_of % halves) & (blk_of < n_past_blk)
          & (row // R_B == slot // n_top))
    bias = jnp.broadcast_to(relt_ref[:, 0:1], (H_B, nk))
    for k, thr in enumerate(_T5_THR):
        bias = jnp.where(dist >= thr, relt_ref[:, k + 1:k + 2], bias)
    s = jnp.where(ok, s + bias, NEG)
    new_ok = jnp.where(row1 < R_B, has_new[0], has_new[1]) > 0
    s_new = jnp.sum(q8r * ksn_ref[...].astype(BF16).astype(F32), axis=-1, keepdims=True)
    s_new = jnp.where(new_ok, s_new + relt_ref[:, 0:1], NEG)
    m = jnp.maximum(jnp.max(s, axis=-1, keepdims=True), s_new)
    e = jnp.where(ok, jnp.exp(s - m), 0.0)
    e_new = jnp.where(new_ok, jnp.exp(s_new - m), 0.0)
    den = jnp.maximum(jnp.sum(e, axis=-1, keepdims=True) + e_new, 1e-30)
    o = (_dot_nt(e.astype(BF16), vc_ref[...]) + e_new * vsn_ref[...].astype(BF16).astype(F32)) / den
    for pi, blk in enumerate(_unstack_gated(o, gsig, 1, 1)):
        osel_ref[:, pi * LANES:(pi + 1) * LANES] = blk

    n_ctx = wst_ref.shape[-1]
    sw = _dot(q8, wst_ref[0].reshape(KV_B, n_ctx).astype(BF16)) + tabw_ref[...]
    okw = sw > MASKED_BELOW
    sw_new = jnp.sum(q8r * kwn_ref[...].astype(BF16).astype(F32), axis=-1, keepdims=True) + relt_ref[:, 0:1]
    mw = jnp.maximum(jnp.max(sw, axis=-1, keepdims=True), sw_new)
    ew = jnp.where(okw, jnp.exp(sw - mw), 0.0)
    ew_new = jnp.exp(sw_new - mw)
    denw = jnp.maximum(jnp.sum(ew, axis=-1, keepdims=True) + ew_new, 1e-30)
    ow = (_dot_nt(ew.astype(BF16), wst_ref[1].reshape(KV_B, n_ctx).astype(BF16))
          + ew_new * vwn_ref[...].astype(BF16).astype(F32)) / denw
    for pi, blk in enumerate(_unstack_gated(ow, gsig, 2, 1)):
        owin_ref[:, pi * LANES:(pi + 1) * LANES] = blk


def _dec_selwin(layer, sel_table, n_pages, cache_nsa_t, proj_s, win_state_t, tabw, rel_t, offs, *, n_top):
    nseq = sel_table.shape[0]
    past = n_pages * PAGE_SIZE
    n_blk = G_B * n_top
    n_past_blk = past // L_SEL
    halves = PAGE_SIZE // L_SEL
    n_ctx = win_state_t.shape[-1]
    qb = offs['b_q'] // W_B
    gb = offs['b_g'] // LANES
    nb = offs['nsa_s'] // LANES
    wb = offs['win_s'] // LANES

    def blk_map(b, sp, n):
        bi = jnp.clip(sp[b, n_pages + n], 0, n_past_blk - 1)
        return (layer, sp[b, bi // halves], 1, 0, 0, 0)

    blk_specs = [pl.BlockSpec((None, None, 2, G_B, HEAD_DIM, PAGE_SIZE), functools.partial(blk_map, n=n))
                 for n in range(n_blk)]
    row = lambda cblk: pl.BlockSpec((None, 1, LANES), lambda b, sp: (b, 0, cblk))
    full = lambda a: pl.BlockSpec(a.shape, lambda b, sp: (0,) * a.ndim)
    grid_spec = pltpu.PrefetchScalarGridSpec(
        num_scalar_prefetch=1,
        grid=(nseq,),
        in_specs=blk_specs + [pl.BlockSpec((None, 1, W_B), lambda b, sp: (b, 0, qb)),
                              row(nb + 2), row(nb + 3),
                              pl.BlockSpec((None, None, 2, G_B, HEAD_DIM, n_ctx),
                                           lambda b, sp: (layer, b, 0, 0, 0, 0)),
                              row(wb), row(wb + 1), full(tabw), full(rel_t), row(gb)],
        out_specs=[pl.BlockSpec((None, 1, W_B), lambda b, sp: (b, 0, 0)),
                   pl.BlockSpec((None, 1, W_B), lambda b, sp: (b, 0, 0))],
        scratch_shapes=[pltpu.VMEM((KV_B, n_blk * PAGE_SIZE), BF16), pltpu.VMEM((KV_B, n_blk * PAGE_SIZE), BF16)],
    )
    return pl.pallas_call(
        functools.partial(_dec_selwin_kernel, n_top=n_top, past=past, n_pages=n_pages),
        grid_spec=grid_spec,
        out_shape=[jax.ShapeDtypeStruct((nseq, 1, W_B), F32)] * 2,
        compiler_params=_cparams(1),
        name="dec_selwin",
    )(sel_table, *([cache_nsa_t] * n_blk), proj_s, proj_s, proj_s, win_state_t, proj_s, proj_s,
      tabw, rel_t, proj_s)


def _dec_sb_kernel(sp_ref, *refs, n_pages_step, tail, n_pages):
    pages = refs[:n_pages_step]
    if tail:
        q_ref, acc_in_ref, carry_in_ref, o_ref, qcol_ref, carry_ref, acc_ref = refs[n_pages_step:]
    else:
        q_ref, acc_out_ref, carry_out_ref, qcol_ref, carry_ref, acc_ref = refs[n_pages_step:]
    b = pl.program_id(0)
    c = pl.program_id(1)
    ps = PAGE_SIZE

    @pl.when(c == 0)
    def _():
        qt = jnp.broadcast_to(q_ref[...], (LANES, W_C)).T
        qcol_ref[...] = qt.reshape(H_C, HEAD_DIM, LANES)
        if tail:
            carry_ref[...] = carry_in_ref[:, 0:1]
            acc_ref[...] = acc_in_ref[...]
        else:
            carry_ref[...] = jnp.zeros_like(carry_ref)
            acc_ref[...] = jnp.zeros_like(acc_ref)

    def sweep():
        tri = jnp.where(_row_iota((ps, ps)) > _lane_iota((ps, ps)), 1.0, 0.0).astype(BF16)
        hrow = _row_iota((H_C, ps))
        for k in range(n_pages_step):
            z = jnp.zeros((H_C, ps), F32)
            for h in range(H_C):
                zh = jnp.sum(qcol_ref[h] * pages[k][0, h], axis=0, keepdims=True)
                z = jnp.where(hrow == h, zh, z)
            t = jnp.log1p(jnp.exp(-jnp.abs(z)))
            log_1mb = -(jnp.maximum(z, 0.0) + t)
            log_b = jnp.minimum(z, 0.0) - t
            hi, lo = _split_bf16(log_1mb)
            after = _dot(hi, tri) + _dot(lo, tri) + carry_ref[...]
            a = jnp.exp(log_b + after)
            carry_ref[...] += jnp.sum(log_1mb, axis=-1, keepdims=True)
            for h in range(H_C):
                acc_ref[h] += a[h:h + 1, :] * pages[k][1, h]

    if tail:
        pl.when(sp_ref[b, n_pages] > 0)(sweep)

        @pl.when(c == pl.num_programs(1) - 1)
        def _():
            acc = acc_ref[...].reshape(W_C, LANES)
            ones = jnp.ones((8, LANES), BF16)
            hi = acc.astype(BF16)
            r1 = acc - hi.astype(F32)
            mid = r1.astype(BF16)
            lo = (r1 - mid.astype(F32)).astype(BF16)
            tot = _dot_nt(ones, hi) + _dot_nt(ones, mid) + _dot_nt(ones, lo)
            o_ref[...] = tot[0:1]
    else:
        sweep()

        @pl.when(c == pl.num_programs(1) - 1)
        def _():
            acc_out_ref[...] = acc_ref[...]
            carry_out_ref[...] = jnp.broadcast_to(carry_ref[...], (H_C, LANES))


def _dec_sb(layer, page_table, cache_sb_t, proj_s, offs):
    nseq, n_pages = page_table.shape
    assert n_pages >= 2
    step = min(8, n_pages // 2)
    assert n_pages % step == 0
    n_head = step
    qb = offs['c_q'] // W_C
    page_block = (None, None, 2, H_C, HEAD_DIM, PAGE_SIZE)
    scratch = [pltpu.VMEM((H_C, HEAD_DIM, LANES), F32), pltpu.VMEM((H_C, 1), F32),
               pltpu.VMEM((H_C, HEAD_DIM, LANES), F32)]
    q_spec = pl.BlockSpec((None, 1, W_C), lambda b, c, sp: (b, 0, qb))
    acc_spec = pl.BlockSpec((None, H_C, HEAD_DIM, LANES), lambda b, c, sp: (b, 0, 0, 0))
    carry_spec = pl.BlockSpec((None, H_C, LANES), lambda b, c, sp: (b, 0, 0))

    head_specs = [pl.BlockSpec(page_block, functools.partial(
        lambda b, c, sp, k: (layer, sp[b, n_pages - 1 - (c * step + k)], 0, 0, 0, 0), k=k)) for k in range(step)]
    acc, carry = pl.pallas_call(
        functools.partial(_dec_sb_kernel, n_pages_step=step, tail=False, n_pages=n_pages),
        grid_spec=pltpu.PrefetchScalarGridSpec(
            num_scalar_prefetch=1, grid=(nseq, n_head // step),
            in_specs=head_specs + [q_spec], out_specs=[acc_spec, carry_spec], scratch_shapes=scratch),
        out_shape=[jax.ShapeDtypeStruct((nseq, H_C, HEAD_DIM, LANES), F32),
                   jax.ShapeDtypeStruct((nseq, H_C, LANES), F32)],
        compiler_params=_cparams(2),
        name="dec_sb_head",
    )(page_table, *([cache_sb_t] * step), proj_s)

    flag = (jnp.max(carry[:, :, 0], axis=1) > SB_UNDERFLOW).astype(jnp.int32)
    table = jnp.concatenate([page_table, flag[:, None]], axis=1)
    table = jnp.pad(table, ((0, 0), (0, -table.shape[1] % LANES)))

    def tail_map(b, c, sp, k):
        page = sp[b, n_pages - 1 - (n_head + c * step + k)]
        return (layer, jnp.where(sp[b, n_pages] > 0, page, 0), 0, 0, 0, 0)

    tail_specs = [pl.BlockSpec(page_block, functools.partial(tail_map, k=k)) for k in range(step)]
    return pl.pallas_call(
        functools.partial(_dec_sb_kernel, n_pages_step=step, tail=True, n_pages=n_pages),
        grid_spec=pltpu.PrefetchScalarGridSpec(
            num_scalar_prefetch=1, grid=(nseq, (n_pages - n_head) // step),
            in_specs=tail_specs + [q_spec, acc_spec, carry_spec],
            out_specs=pl.BlockSpec((None, 1, W_C), lambda b, c, sp: (b, 0, 0)), scratch_shapes=scratch),
        out_shape=jax.ShapeDtypeStruct((nseq, 1, W_C), F32),
        compiler_params=_cparams(2),
        name="dec_sb_tail",
    )(table, *([cache_sb_t] * step), proj_s, acc, carry)


def _overlap(ncp, n_lanes, reps):
    c0 = np.arange(ncp)[:, None] * D_CMP
    s0 = np.arange(n_lanes)[None, :] * L_SEL
    ov = ((c0 < s0 + L_SEL) & (c0 + L_CMP > s0)).astype(np.float32)
    return jnp.asarray(np.tile(ov, (1, reps)), BF16)


def _flat16_t(kt):
    b, _, t = kt.shape
    r = kt.reshape(b, G_B, HEAD_DIM, t // D_CMP, D_CMP).transpose(0, 1, 3, 4, 2)
    return r.reshape(b, G_B, t // D_CMP, D_CMP * HEAD_DIM)


def kernel(x_prompt, x_sample, cache_sb, cache_nsa, state_win, state_conv, page_table, c_prompt, c_sample,
           rel_bias, norm_pre, norm_post, w_ada, b_ada, w_ffn_gate, w_ffn_up, w_ffn_down, w_in, conv_w,
           cmp_pos, w_cmp1, w_cmp2, w_branch, w_out):
    batch, seq, d = x_prompt.shape
    nseq = x_sample.shape[0]
    depth = w_in.shape[0]
    n_pages = page_table.shape[1]
    past = n_pages * PAGE_SIZE
    n_ctx = state_win.shape[2]
    assert x_sample.shape[1] == 1 and seq % Q_BLOCK == 0 and seq // L_SEL <= L_SEL
    assert cache_sb.shape[2] == PAGE_SIZE and n_ctx == WINDOW
    offs, row_part, t_part, n_row, n_all = _layout(d)
    half = D_CMP * HEAD_DIM

    q_names = ('b_q', 'c_q')
    w_row = _gather_cols(w_in, row_part, q_names)
    w_kv = _gather_cols(w_in, t_part, q_names)
    tn = n_row // 2
    tn_s = 2048
    assert tn % LANES == 0
    n_s = -(-n_all // tn_s) * tn_s
    w_r = w_row.astype(BF16)
    w_t = w_kv.transpose(0, 2, 1).astype(BF16)
    w_s = jnp.concatenate([w_row, w_kv, jnp.zeros((depth, d, n_s - n_all), F32)], axis=-1).astype(BF16)
    wg = w_ffn_gate.astype(BF16)
    wu = w_ffn_up.astype(BF16)
    wd = w_ffn_down.astype(BF16)
    wba = w_branch[:, 0:W_A].astype(BF16)
    hb = lambda h: w_branch[:, W_A + h * HEAD_DIM:W_A + (h + 1) * HEAD_DIM]
    wbb = jnp.concatenate([x for p in range(R_B) for x in (hb(p), hb(p + R_B))], axis=1).astype(BF16)
    wbc = w_branch[:, W_A + W_B:].astype(BF16)
    wo = w_out.astype(BF16)
    w1 = w_cmp1.astype(BF16)
    z = jnp.zeros_like(w_cmp2)
    w2p = jnp.stack([jnp.concatenate([w_cmp2, z], axis=-1), jnp.concatenate([z, w_cmp2], axis=-1)],
                    axis=2).astype(BF16)
    pos = cmp_pos.reshape(depth, 2, 2, 1, half)

    rows_c = -(-(batch + nseq) // 16) * 16
    c_all = jnp.concatenate([c_prompt, c_sample, jnp.zeros((rows_c - batch - nseq, d), F32)], axis=0)
    ada_all = _ada_all(c_all, w_ada, b_ada)
    ada_p = ada_all[:, :batch].reshape(depth, batch, 3 * N_SUB, 1, d)
    ada_s = ada_all[:, batch:batch + nseq].reshape(depth, nseq, 3 * N_SUB, d).transpose(0, 2, 1, 3)[:, None]

    tq = Q_BLOCK
    ncp_p = seq // D_CMP
    tab_sel = _bias_table(rel_bias, 4, 2 * tq, tq, base0=0, tile_step=tq, row_step=-1, col_step=-1,
                          heads_on_lanes=True)
    tab_win = _bias_table(rel_bias, 6, 2 * tq, tq, base0=0, tile_step=tq, row_step=-1, col_step=-1,
                          max_dist=WINDOW, heads_on_lanes=True)
    tab_cmp = _bias_table(rel_bias, seq // tq, tq, ncp_p, base0=-(L_CMP - 1), tile_step=tq, row_step=1,
                          col_step=D_CMP)
    ncp_s = past // D_CMP
    tab_cmp_s = _bias_table(rel_bias, 1, 1, ncp_s, base0=past - (L_CMP - 1), tile_step=0, row_step=0,
                            col_step=D_CMP)[0]
    tab_win_s = _bias_table(rel_bias, 1, 1, n_ctx, base0=n_ctx, tile_step=0, row_step=0, col_step=1,
                            max_dist=WINDOW)[0]
    rel_t = rel_bias.T
    ovl_p = _overlap(ncp_p, L_SEL, G_B)
    nsl_s = -(-(past // L_SEL + 1) // LANES) * LANES
    ovl_s = _overlap(ncp_s, nsl_s, 1)
    n_top_p = min(N_SEL, -(-seq // L_SEL))
    n_top_s = min(N_SEL, -(-(past + 1) // L_SEL))

    cache_sb_t = cache_sb.transpose(0, 1, 3, 4, 5, 2)
    cache_nsa_t = cache_nsa.transpose(0, 1, 3, 4, 5, 2)
    win_state_t = state_win.transpose(0, 1, 3, 4, 5, 2)

    tm_p = 512 if seq % 512 == 0 else Q_BLOCK
    tf = w_ffn_gate.shape[-1] // 2

    y_p = x_prompt
    y_s = x_sample.reshape(1, nseq, d)
    outs = {k: [] for k in ('sb_p', 'sb_s', 'nsa_p', 'nsa_s', 'win_p', 'win_s', 'conv_p', 'conv_s')}
    for l in range(depth):
        gpre = [norm_pre[l, i][None] for i in range(N_SUB)]
        gpost = [norm_post[l, i][None] for i in range(N_SUB)]
        y_p = _ffn(y_p, ada_p[l], gpre[0], gpost[0], wg[l, 0], wu[l, 0], wd[l, 0], sub=0, tm=tm_p, tf=tf)
        projr = _proj(y_p, ada_p[l], gpre[1], w_r[l], tm=tm_p, tn=tn)
        sbt, nsat, wint = _proj_t(y_p, ada_p[l], gpre[1], w_t[l], tm=tm_p)
        hrows = jnp.stack([_flat16_t(nsat[:, 0:KV_B]), _flat16_t(nsat[:, KV_B:2 * KV_B])], axis=1)
        kvc = _compress_prompt(hrows, pos[l], w1[l], w2p[l])
        y_cmp, pen = _cmp_prompt(projr, kvc, tab_cmp, ovl_p, offs, n_top=n_top_p)
        y_sel = _nsa_flash_prompt(projr, nsat, tab_sel, pen, offs, mode='sel')
        y_win = _nsa_flash_prompt(projr, wint, tab_win, pen, offs, mode='win')
        y_c = _sb_prompt(projr, sbt, offs)
        y_p, u_tail = _merge(y_p, ada_p[l], gpost[1], projr, None, (y_cmp, y_sel, y_win), y_c, conv_w[l],
                             wba[l], wbb[l], wbc[l], wo[l], offs, tm=min(tm_p, 256), decode=False)
        y_p = _ffn(y_p, ada_p[l], gpre[2], gpost[2], wg[l, 1], wu[l, 1], wd[l, 1], sub=2, tm=tm_p, tf=tf)
        n_keep = min(WINDOW, seq)
        to_state = lambda a, k, g: a.reshape(batch, k, g, HEAD_DIM, a.shape[-1]).transpose(0, 4, 1, 2, 3)
        outs['sb_p'].append(to_state(sbt, 2, H_C))
        outs['nsa_p'].append(to_state(nsat, 4, G_B))
        outs['win_p'].append(to_state(wint[:, :, seq - n_keep:], 2, G_B))
        outs['conv_p'].append(u_tail[:, 8 - (CONV_W - 1):])
        y_s = _ffn(y_s, ada_s[l], gpre[0], gpost[0], wg[l, 0], wu[l, 0], wd[l, 0], sub=0, tm=nseq, tf=tf)
        proj_s = _proj(y_s, ada_s[l], gpre[1], w_s[l], tm=nseq, tn=tn_s)
        proj_r = proj_s.reshape(nseq, 1, n_s)
        s_cmp, top_idx = _dec_cmp(l, page_table, cache_nsa_t, proj_r, pos[l], w1[l], w2p[l], tab_cmp_s, ovl_s,
                                  offs, n_top=n_top_s)
        sel_table = jnp.concatenate([page_table, top_idx[:, 0, 0:n_top_s], top_idx[:, 1, 0:n_top_s]], axis=1)
        sel_table = jnp.pad(sel_table, ((0, 0), (0, -sel_table.shape[1] % LANES)))
        s_sel, s_win = _dec_selwin(l, sel_table, n_pages, cache_nsa_t, proj_r, win_state_t, tab_win_s,
                                   rel_t, offs, n_top=n_top_s)
        s_c = _dec_sb(l, page_table, cache_sb_t, proj_r, offs)
        to_rows = lambda a: a.reshape(1, nseq, a.shape[-1])
        ctx = (state_conv[l, :, 0][None], state_conv[l, :, 1][None])
        y_s, u_s = _merge(y_s, ada_s[l], gpost[1], proj_s, ctx, (to_rows(s_cmp), to_rows(s_sel), to_rows(s_win)),
                          to_rows(s_c), conv_w[l], wba[l], wbb[l], wbc[l], wo[l], offs, tm=nseq, decode=True)
        y_s = _ffn(y_s, ada_s[l], gpre[2], gpost[2], wg[l, 1], wu[l, 1], wd[l, 1], sub=2, tm=nseq, tf=tf)
        ps = proj_s[0]
        o_s, o_n, o_w = offs['sb_s'], offs['nsa_s'], offs['win_s']
        outs['sb_s'].append(ps[:, o_s:o_s + KT_SB].reshape(nseq, 1, 2, H_C, HEAD_DIM))
        outs['nsa_s'].append(ps[:, o_n:o_n + KT_NSA].reshape(nseq, 1, 4, G_B, HEAD_DIM))
        win_new = ps[:, o_w:o_w + KT_WIN].reshape(nseq, 1, 2, G_B, HEAD_DIM)
        outs['win_s'].append(jnp.concatenate([state_win[l][:, 1:], win_new], axis=1))
        outs['conv_s'].append(jnp.stack([state_conv[l, :, 1], u_s[0]], axis=1))
    st = {k: jnp.stack(v) for k, v in outs.items()}
    return (y_p, y_s.reshape(nseq, 1, d), st['sb_p'], st['sb_s'], st['nsa_p'], st['nsa_s'],
            st['win_p'], st['win_s'], st['conv_p'], st['conv_s'])
```

```python
import functools
import math

import numpy as np
import jax
import jax.numpy as jnp
from jax import lax
from jax.experimental import pallas as pl
from jax.experimental.pallas import tpu as pltpu

F32 = jnp.float32
BF16 = jnp.bfloat16

HEAD_DIM = 64
CONV_CH = 512
CONV_W = 3
H_B = 8
G_B = 2
R_B = H_B // G_B
L_CMP = 32
D_CMP = 16
L_SEL = 64
N_SEL = 16
WINDOW = 512
H_C = 8
N_BUCKETS = 32
MAX_DIST = 128
N_SUB = 3
Q_BLOCK = 128
PAGE_SIZE = 128
EPS = 1e-6
NEG = -1e30
MASKED_BELOW = -1e29
SB_UNDERFLOW = -105.0
FORCE_BONUS = 1e4

W_A = CONV_CH
W_B = H_B * HEAD_DIM
W_C = H_C * HEAD_DIM
KV_B = G_B * HEAD_DIM
LANES = 128
VMEM_LIMIT = 56 * 1024 * 1024

KT_SB = 2 * W_C
KT_NSA = 4 * KV_B
KT_WIN = 2 * KV_B


def _layout(d_model):
    sizes = (W_A, W_A, W_A, W_B, KV_B, KV_B, KV_B, KV_B, KV_B, KV_B, 3 * H_B, W_C, W_C, W_C, 3 * d_model)
    starts = np.concatenate([[0], np.cumsum(sizes)]).astype(np.int64)
    names = ('a_v', 'a_b', 'a_c', 'b_q', 'b_kc', 'b_vc', 'b_ks', 'b_vs', 'b_kw', 'b_vw', 'b_g',
             'c_q', 'c_k', 'c_v', 'g_merge')
    seg = {n: (int(starts[i]), int(starts[i + 1])) for i, n in enumerate(names)}
    q0 = seg['b_q'][0]
    head = lambda h: (q0 + h * HEAD_DIM, q0 + (h + 1) * HEAD_DIM)
    bq_pairs = []
    for p in range(R_B):
        bq_pairs += [head(p), head(p + R_B)]
    row_part = [('g_merge', [seg['g_merge']], 0),
                ('conv', [seg['a_v'], seg['a_b'], seg['a_c']], 0),
                ('b_q', bq_pairs, 0),
                ('c_q', [seg['c_q']], 0),
                ('b_g', [seg['b_g']], LANES - 3 * H_B),
                ('k_rows', [seg['b_ks'], seg['b_kw']], LANES)]
    t_part = [('sb', [seg['c_k'], seg['c_v']], 0),
              ('nsa', [seg['b_kc'], seg['b_vc'], seg['b_ks'], seg['b_vs']], 0),
              ('win', [seg['b_kw'], seg['b_vw']], 0)]
    offs, o = {}, 0
    for n, rngs, padn in row_part:
        offs[n] = o
        o += sum(b - a for a, b in rngs) + padn
    n_row = o
    for n, rngs, padn in t_part:
        offs[n + '_s'] = o
        o += sum(b - a for a, b in rngs) + padn
    return offs, row_part, t_part, n_row, o


def _gather_cols(w, parts, scale_names):
    cols = []
    for n, rngs, padn in parts:
        for a, b in rngs:
            c = w[:, :, a:b]
            cols.append(c * (HEAD_DIM ** -0.5) if n in scale_names else c)
        if padn:
            cols.append(jnp.zeros(w.shape[:2] + (padn,), w.dtype))
    return jnp.concatenate(cols, axis=-1)


def _t5_thresholds():
    max_exact = N_BUCKETS // 2
    def bucket(n):
        if n < max_exact:
            return n
        b = max_exact + int(math.log(max(n, 1) / max_exact) / math.log(MAX_DIST / max_exact)
                            * (N_BUCKETS - max_exact))
        return min(b, N_BUCKETS - 1)
    thr = []
    for k in range(1, N_BUCKETS):
        n = 0
        while bucket(n) < k:
            n += 1
        thr.append(n)
    return tuple(thr)


_T5_THR = _t5_thresholds()


def _cparams(n_grid, vmem=VMEM_LIMIT):
    return pltpu.CompilerParams(dimension_semantics=("arbitrary",) * n_grid, vmem_limit_bytes=vmem)


def _dot(a, b):
    return jnp.dot(a, b, preferred_element_type=F32)


def _dot_nt(a, b):
    return lax.dot_general(a, b, (((1,), (1,)), ((), ())), preferred_element_type=F32)


def _rms(x):
    return x * lax.rsqrt(jnp.mean(x * x, axis=-1, keepdims=True) + EPS)


def _silu(x):
    return x * jax.nn.sigmoid(x)


def _lane_iota(shape):
    return lax.broadcasted_iota(jnp.int32, shape, len(shape) - 1)


def _row_iota(shape):
    return lax.broadcasted_iota(jnp.int32, shape, len(shape) - 2)


def _split_bf16(x):
    hi = x.astype(BF16)
    lo = (x - hi.astype(F32)).astype(BF16)
    return hi, lo


def _ada_kernel(c_ref, w_ref, b_ref, o_ref):
    h = _silu(c_ref[...]).astype(BF16)
    o_ref[...] = _dot(h, w_ref[...].astype(BF16)) + b_ref[...]


def _ada_all(c_all, w_ada, b_ada, tn=1536):
    depth, d, n = w_ada.shape
    rows = c_all.shape[0]
    return pl.pallas_call(
        _ada_kernel,
        grid=(depth, n // tn),
        in_specs=[pl.BlockSpec((rows, d), lambda l, j: (0, 0)),
                  pl.BlockSpec((None, d, tn), lambda l, j: (l, 0, j)),
                  pl.BlockSpec((None, 1, tn), lambda l, j: (l, 0, j))],
        out_specs=pl.BlockSpec((None, rows, tn), lambda l, j: (l, 0, j)),
        out_shape=jax.ShapeDtypeStruct((depth, rows, n), F32),
        compiler_params=_cparams(2),
        name="ada",
    )(c_all, w_ada, b_ada.reshape(depth, 1, n))


def _table_kernel(rel_ref, o_ref, *, base0, tile_step, row_step, col_step, max_dist):
    t = pl.program_id(0)
    h = pl.program_id(1)
    shape = o_ref.shape
    dist = (base0 + t * tile_step + _row_iota(shape) * row_step - _lane_iota(shape) * col_step)
    b = jnp.full(shape, rel_ref[0, h], F32)
    for k, thr in enumerate(_T5_THR):
        b = jnp.where(dist >= thr, rel_ref[k + 1, h], b)
    ok = dist >= 0
    if max_dist is not None:
        ok = ok & (dist <= max_dist)
    o_ref[...] = jnp.where(ok, b, NEG)


def _bias_table(rel_bias, n_tiles, rows, cols, *, base0, tile_step, row_step, col_step, max_dist=None,
                heads_on_lanes=False):
    n_heads = rel_bias.shape[1]
    if heads_on_lanes:
        return pl.pallas_call(
            functools.partial(_table_kernel, base0=base0, tile_step=tile_step, row_step=row_step,
                              col_step=col_step, max_dist=max_dist),
            grid=(n_tiles, n_heads),
            in_specs=[pl.BlockSpec(memory_space=pltpu.SMEM)],
            out_specs=pl.BlockSpec((None, rows, cols), lambda t, h: (t, 0, h)),
            out_shape=jax.ShapeDtypeStruct((n_tiles, rows, n_heads * cols), F32),
            compiler_params=_cparams(2),
            name="bias_table",
        )(rel_bias)
    out = pl.pallas_call(
        functools.partial(_table_kernel, base0=base0, tile_step=tile_step, row_step=row_step,
                          col_step=col_step, max_dist=max_dist),
        grid=(n_tiles, n_heads),
        in_specs=[pl.BlockSpec(memory_space=pltpu.SMEM)],
        out_specs=pl.BlockSpec((None, None, rows, cols), lambda t, h: (t, h, 0, 0)),
        out_shape=jax.ShapeDtypeStruct((n_tiles, n_heads, rows, cols), F32),
        compiler_params=_cparams(2),
        name="bias_table",
    )(rel_bias)
    return out.reshape(n_tiles, n_heads * rows, cols)


def _ffn_kernel(x_ref, ada_ref, gpre_ref, gpost_ref, wg_ref, wu_ref, wd_ref, o_ref, h_ref, acc_ref, *, sub):
    f = pl.program_id(2)

    @pl.when(f == 0)
    def _():
        h = _rms(x_ref[...]) * gpre_ref[...]
        h = h * (1.0 + ada_ref[3 * sub + 1]) + ada_ref[3 * sub]
        h_ref[...] = h.astype(BF16)
        acc_ref[...] = jnp.zeros_like(acc_ref)

    h = h_ref[...]
    a = _silu(_dot(h, wg_ref[...])) * _dot(h, wu_ref[...])
    acc_ref[...] += _dot(a.astype(BF16), wd_ref[...])

    @pl.when(f == pl.num_programs(2) - 1)
    def _():
        y = _rms(acc_ref[...]) * gpost_ref[...]
        o_ref[...] = x_ref[...] + 0.5 * (ada_ref[3 * sub + 2] * y)


def _ffn(x, ada, gpre, gpost, wg, wu, wd, *, sub, tm, tf):
    bg, t, d = x.shape
    f = wg.shape[1]
    mrows = ada.shape[2]
    return pl.pallas_call(
        functools.partial(_ffn_kernel, sub=sub),
        grid=(bg, t // tm, f // tf),
        in_specs=[pl.BlockSpec((None, tm, d), lambda b, i, j: (b, i, 0)),
                  pl.BlockSpec((None, 3 * N_SUB, mrows, d), lambda b, i, j: (b, 0, 0, 0)),
                  pl.BlockSpec((1, d), lambda b, i, j: (0, 0)),
                  pl.BlockSpec((1, d), lambda b, i, j: (0, 0)),
                  pl.BlockSpec((d, tf), lambda b, i, j: (0, j)),
                  pl.BlockSpec((d, tf), lambda b, i, j: (0, j)),
                  pl.BlockSpec((tf, d), lambda b, i, j: (j, 0))],
        out_specs=pl.BlockSpec((None, tm, d), lambda b, i, j: (b, i, 0)),
        out_shape=jax.ShapeDtypeStruct(x.shape, F32),
        scratch_shapes=[pltpu.VMEM((tm, d), BF16), pltpu.VMEM((tm, d), F32)],
        compiler_params=_cparams(3),
        name="ffn",
    )(x, ada, gpre, gpost, wg, wu, wd)


def _mod1(x_ref, ada_ref, gpre_ref):
    h = _rms(x_ref[...]) * gpre_ref[...]
    return (h * (1.0 + ada_ref[4]) + ada_ref[3]).astype(BF16)


def _proj_kernel(x_ref, ada_ref, gpre_ref, w_ref, o_ref, h_ref):
    @pl.when(pl.program_id(2) == 0)
    def _():
        h_ref[...] = _mod1(x_ref, ada_ref, gpre_ref)

    o_ref[...] = _dot(h_ref[...], w_ref[...])


def _proj(x, ada, gpre, w, *, tm, tn):
    bg, t, d = x.shape
    n = w.shape[1]
    mrows = ada.shape[2]
    return pl.pallas_call(
        _proj_kernel,
        grid=(bg, t // tm, n // tn),
        in_specs=[pl.BlockSpec((None, tm, d), lambda b, i, j: (b, i, 0)),
                  pl.BlockSpec((None, 3 * N_SUB, mrows, d), lambda b, i, j: (b, 0, 0, 0)),
                  pl.BlockSpec((1, d), lambda b, i, j: (0, 0)),
                  pl.BlockSpec((d, tn), lambda b, i, j: (0, j))],
        out_specs=pl.BlockSpec((None, tm, tn), lambda b, i, j: (b, i, j)),
        out_shape=jax.ShapeDtypeStruct((bg, t, n), F32),
        scratch_shapes=[pltpu.VMEM((tm, d), BF16)],
        compiler_params=_cparams(3),
        name="proj",
    )(x, ada, gpre, w)


def _proj_t_kernel(x_ref, ada_ref, gpre_ref, w_ref, sb_ref, nsa_ref, win_ref):
    h = _mod1(x_ref, ada_ref, gpre_ref)
    kt = _dot_nt(w_ref[...], h)
    sb_ref[...] = kt[0:KT_SB]
    nsa_ref[...] = kt[KT_SB:KT_SB + KT_NSA]
    win_ref[...] = kt[KT_SB + KT_NSA:KT_SB + KT_NSA + KT_WIN]


def _proj_t(x, ada, gpre, w_t, *, tm):
    bg, t, d = x.shape
    nf = w_t.shape[0]
    out = lambda rows: (pl.BlockSpec((None, rows, tm), lambda b, i: (b, 0, i)),
                        jax.ShapeDtypeStruct((bg, rows, t), F32))
    specs, shapes = zip(out(KT_SB), out(KT_NSA), out(KT_WIN))
    return pl.pallas_call(
        _proj_t_kernel,
        grid=(bg, t // tm),
        in_specs=[pl.BlockSpec((None, tm, d), lambda b, i: (b, i, 0)),
                  pl.BlockSpec((None, 3 * N_SUB, 1, d), lambda b, i: (b, 0, 0, 0)),
                  pl.BlockSpec((1, d), lambda b, i: (0, 0)),
                  pl.BlockSpec((nf, d), lambda b, i: (0, 0))],
        out_specs=list(specs),
        out_shape=list(shapes),
        compiler_params=_cparams(2),
        name="proj_t",
    )(x, ada, gpre, w_t)


def _compress_kernel(h_ref, pos_ref, w1_ref, w2_ref, o_ref):
    ncp = h_ref.shape[1]
    half = h_ref.shape[2]
    out = None
    for g in range(G_B):
        hg = h_ref[g]
        a = _dot((hg + pos_ref[0]).astype(BF16), w1_ref[0:half, :])
        bm = _dot((hg + pos_ref[1]).astype(BF16), w1_ref[half:2 * half, :])
        pre = a + pltpu.roll(bm, ncp - 1, 0)
        o = _dot(_silu(pre).astype(BF16), w2_ref[g])
        out = o if out is None else out + o
    o_ref[...] = out


def _compress_prompt(hrows, pos, w1, w2p):
    b, _, g, ncp, half = hrows.shape
    hid = w1.shape[2]
    return pl.pallas_call(
        _compress_kernel,
        grid=(b, 2),
        in_specs=[pl.BlockSpec((None, None, g, ncp, half), lambda i, k: (i, k, 0, 0, 0)),
                  pl.BlockSpec((None, 2, 1, half), lambda i, k: (k, 0, 0, 0)),
                  pl.BlockSpec((None, 2 * half, hid), lambda i, k: (k, 0, 0)),
                  pl.BlockSpec((None, g, hid, LANES), lambda i, k: (k, 0, 0, 0))],
        out_specs=pl.BlockSpec((None, None, ncp, LANES), lambda i, k: (i, k, 0, 0)),
        out_shape=jax.ShapeDtypeStruct((b, 2, ncp, LANES), F32),
        compiler_params=_cparams(2),
        name="compress",
    )(hrows, pos, w1, w2p)


def _stack_heads(blk_fn, tq):
    lane = _lane_iota((tq, LANES))
    lo = [jnp.where(lane < HEAD_DIM, blk_fn(p), 0.0) for p in range(R_B)]
    hi = [jnp.where(lane >= HEAD_DIM, blk_fn(p), 0.0) for p in range(R_B)]
    return jnp.concatenate(lo + hi, axis=0)


def _unstack_gated(o, gsig, branch, tq):
    lane = _lane_iota((tq, LANES))
    out = []
    for p in range(R_B):
        c_lo = 3 * p + branch
        c_hi = 3 * (p + R_B) + branch
        blk = jnp.where(lane < HEAD_DIM, o[p * tq:(p + 1) * tq], o[(p + R_B) * tq:(p + R_B + 1) * tq])
        gate = jnp.where(lane < HEAD_DIM, gsig[:, c_lo:c_lo + 1], gsig[:, c_hi:c_hi + 1])
        out.append(blk * gate)
    return out


def _topk_penalty(score_t, n_top):
    nb = score_t.shape[0]
    jrow = _row_iota(score_t.shape).astype(F32)
    x = score_t
    sel = jnp.zeros(score_t.shape, F32)
    for _ in range(n_top):
        mx = jnp.max(x, axis=0, keepdims=True)
        first = jnp.min(jnp.where(x == mx, jrow, float(nb)), axis=0, keepdims=True)
        hit = jrow == first
        sel = jnp.where(hit, 1.0, sel)
        x = jnp.where(hit, -jnp.inf, x)
    return jnp.where(sel > 0.5, 0.0, NEG)


def _cmp_kernel(q_ref, kc_ref, vc_ref, tab_ref, ovl_ref, bg_ref, o_ref, pen_ref, *, n_top):
    tq = q_ref.shape[0]
    i = pl.program_id(1)
    q8 = _stack_heads(lambda p: q_ref[:, p * LANES:(p + 1) * LANES], tq).astype(BF16)
    s = _dot_nt(q8, kc_ref[...].astype(BF16)) + tab_ref[...]
    ok = s > MASKED_BELOW
    e = jnp.where(ok, jnp.exp(s - jnp.max(s, axis=-1, keepdims=True)), 0.0)
    p = (e / jnp.maximum(jnp.sum(e, axis=-1, keepdims=True), 1e-30)).astype(BF16)
    o = _dot(p, vc_ref[...].astype(BF16))
    gsig = jax.nn.sigmoid(bg_ref[...])
    for pi, blk in enumerate(_unstack_gated(o, gsig, 0, tq)):
        o_ref[:, pi * LANES:(pi + 1) * LANES] = blk
    imp8 = _dot(p, ovl_ref[...])
    lane = _lane_iota((tq, LANES))
    g0 = imp8[0:tq] + imp8[tq:2 * tq] + imp8[2 * tq:3 * tq] + imp8[3 * tq:4 * tq]
    g1 = imp8[4 * tq:5 * tq] + imp8[5 * tq:6 * tq] + imp8[6 * tq:7 * tq] + imp8[7 * tq:8 * tq]
    imp = jnp.where(lane < L_SEL, g0, g1)
    qpos = i * tq + _row_iota((tq, LANES))
    j = lane % L_SEL
    cur = qpos // L_SEL
    forced = (j == 0) | (j == cur) | (j == cur - 1)
    score = jnp.where(j * L_SEL <= qpos, imp + FORCE_BONUS * forced.astype(F32), NEG)
    st = score.T
    pen_t = jnp.concatenate([_topk_penalty(st[g * L_SEL:(g + 1) * L_SEL], n_top) for g in range(G_B)], axis=0)
    pen_ref[...] = pen_t


def _cmp_prompt(projr, kvc, tab, ovl, offs, *, n_top, tq=Q_BLOCK):
    b, t, _ = projr.shape
    ncp = kvc.shape[2]
    qb = offs['b_q'] // W_B
    gb = offs['b_g'] // LANES
    return pl.pallas_call(
        functools.partial(_cmp_kernel, n_top=n_top),
        grid=(b, t // tq),
        in_specs=[pl.BlockSpec((None, tq, W_B), lambda bi, i: (bi, i, qb)),
                  pl.BlockSpec((None, None, ncp, LANES), lambda bi, i: (bi, 0, 0, 0)),
                  pl.BlockSpec((None, None, ncp, LANES), lambda bi, i: (bi, 1, 0, 0)),
                  pl.BlockSpec((None, H_B * tq, ncp), lambda bi, i: (i, 0, 0)),
                  pl.BlockSpec((ncp, LANES), lambda bi, i: (0, 0)),
                  pl.BlockSpec((None, tq, LANES), lambda bi, i: (bi, i, gb))],
        out_specs=[pl.BlockSpec((None, tq, W_B), lambda bi, i: (bi, i, 0)),
                   pl.BlockSpec((None, LANES, tq), lambda bi, i: (bi, 0, i))],
        out_shape=[jax.ShapeDtypeStruct((b, t, W_B), F32), jax.ShapeDtypeStruct((b, LANES, t), F32)],
        compiler_params=_cparams(2),
        name="nsa_cmp",
    )(projr, kvc, kvc, tab, ovl, projr)


def _nsa_flash_kernel(q_ref, k_ref, v_ref, tab_ref, pen_ref, bg_ref, o_ref, qa_ref, m_ref, l_ref, acc_ref, *,
                      mode, branch):
    tq = q_ref.shape[0]
    tk = 2 * tq
    i = pl.program_id(1)
    jd = i // 2
    par = i % 2

    row = _row_iota((LANES, tq))
    for p in range(R_B):
        qt = q_ref[:, p * LANES:(p + 1) * LANES].T
        qa_ref[0:LANES, p * tq:(p + 1) * tq] = jnp.where(row < HEAD_DIM, qt, 0.0).astype(BF16)
        qa_ref[0:LANES, (p + R_B) * tq:(p + R_B + 1) * tq] = jnp.where(row >= HEAD_DIM, qt, 0.0).astype(BF16)
    if mode == 'sel':
        pen = pen_ref[...]
        lo = jnp.where(row < L_SEL, pen, 0.0).astype(BF16)
        hi = jnp.where(row >= L_SEL, pen, 0.0).astype(BF16)
        qa_ref[LANES:2 * LANES, :] = jnp.concatenate([lo] * R_B + [hi] * R_B, axis=1)
    m_ref[...] = jnp.full_like(m_ref, NEG)
    l_ref[...] = jnp.zeros_like(l_ref)
    acc_ref[...] = jnp.zeros_like(acc_ref)

    def step(j, bias, uniform_bias=False):
        k0 = pl.multiple_of(j * tk, tk)
        kt = k_ref[pl.ds(k0, tk), :].astype(BF16)
        if mode == 'sel':
            blk = (tk // L_SEL) * j + _row_iota((tk, LANES)) // L_SEL
            et = jnp.where(blk == _lane_iota((tk, LANES)) % L_SEL, 1.0, 0.0).astype(BF16)
            kt = jnp.concatenate([kt, et], axis=1)
        s = _dot(kt, qa_ref[...])
        m_old = m_ref[...]
        if uniform_bias:
            m_new = jnp.maximum(m_old, jnp.max(s, axis=0, keepdims=True) + bias)
            p = jnp.exp(s - (m_new - bias))
        else:
            s = s + bias
            m_new = jnp.maximum(m_old, jnp.max(s, axis=0, keepdims=True))
            p = jnp.exp(s - m_new)
        alpha = jnp.exp(m_old - m_new)
        l_ref[...] = alpha * l_ref[...] + jnp.sum(p, axis=0, keepdims=True)
        acc_ref[...] = alpha * acc_ref[...] + _dot(v_ref[:, pl.ds(k0, tk)].astype(BF16), p.astype(BF16))
        m_ref[...] = m_new

    step(jd, tab_ref[par])
    n_near = tab_ref.shape[0] // 2
    for dj in range(1, n_near):
        @pl.when(jd >= dj)
        def _(dj=dj):
            step(jd - dj, tab_ref[par + 2 * dj])

    if mode == 'sel':
        far_bias = tab_ref[3, 0:1, :]

        def far_body(j, c):
            step(j, far_bias, uniform_bias=True)
            return c

        lax.fori_loop(0, jnp.maximum(jd - (n_near - 1), 0), far_body, 0)

    ot = acc_ref[...] / jnp.maximum(l_ref[...], 1e-30)
    o = jnp.concatenate([ot[:, h * tq:(h + 1) * tq].T for h in range(H_B)], axis=0)
    gsig = jax.nn.sigmoid(bg_ref[...])
    for pi, blk in enumerate(_unstack_gated(o, gsig, branch, tq)):
        o_ref[:, pi * LANES:(pi + 1) * LANES] = blk


def _nsa_flash_prompt(projr, vt, tab, pen_t, offs, *, mode, tq=Q_BLOCK):
    b, t, _ = projr.shape
    assert t % (2 * tq) == 0 and WINDOW == 4 * tq and tq == LANES
    qb = offs['b_q'] // W_B
    gb = offs['b_g'] // LANES
    kb = offs['k_rows'] // LANES
    if mode == 'sel':
        vb, branch, qa_rows = 3, 1, 2 * LANES
    else:
        kb, vb, branch, qa_rows = kb + 1, 1, 2, LANES
    return pl.pallas_call(
        functools.partial(_nsa_flash_kernel, mode=mode, branch=branch),
        grid=(b, t // tq),
        in_specs=[pl.BlockSpec((None, tq, W_B), lambda bi, i: (bi, i, qb)),
                  pl.BlockSpec((None, t, KV_B), lambda bi, i: (bi, 0, kb)),
                  pl.BlockSpec((None, KV_B, t), lambda bi, i: (bi, vb, 0)),
                  pl.BlockSpec(tab.shape, lambda bi, i: (0, 0, 0)),
                  pl.BlockSpec((None, LANES, tq), lambda bi, i: (bi, 0, i)),
                  pl.BlockSpec((None, tq, LANES), lambda bi, i: (bi, i, gb))],
        out_specs=pl.BlockSpec((None, tq, W_B), lambda bi, i: (bi, i, 0)),
        out_shape=jax.ShapeDtypeStruct((b, t, W_B), F32),
        scratch_shapes=[pltpu.VMEM((qa_rows, H_B * tq), BF16),
                        pltpu.VMEM((1, H_B * tq), F32),
                        pltpu.VMEM((1, H_B * tq), F32),
                        pltpu.VMEM((KV_B, H_B * tq), F32)],
        compiler_params=_cparams(2),
        name="nsa_" + mode,
    )(projr, projr, vt, tab, pen_t, projr)


def _sb_kernel(q_ref, k_ref, v_ref, o_ref, qs_ref, carry_ref, acc_ref):
    tq = q_ref.shape[0]
    tk = tq
    i = pl.program_id(2)
    lane = _lane_iota((tq, LANES))
    q = q_ref[...]
    qs_ref[...] = jnp.concatenate([jnp.where(lane < HEAD_DIM, q, 0.0),
                                   jnp.where(lane >= HEAD_DIM, q, 0.0)], axis=0).astype(BF16)
    carry_ref[...] = jnp.zeros_like(carry_ref)
    acc_ref[...] = jnp.zeros_like(acc_ref)
    jj = _row_iota((2 * tk, tk))
    tri2 = jnp.where(jnp.where(jj >= tk, jj - tk, jj) > _lane_iota((2 * tk, tk)), 1.0, 0.0).astype(BF16)

    def step(j, diagonal):
        k0 = pl.multiple_of(j * tk, tk)
        z = _dot(qs_ref[...], k_ref[:, pl.ds(k0, tk)].astype(BF16))
        t = jnp.log(1.0 + jnp.exp(-jnp.abs(z)))
        log_1mb = -(jnp.maximum(z, 0.0) + t)
        log_b = z + log_1mb
        if diagonal:
            rr = _row_iota((2 * tq, tk))
            causal = _lane_iota((2 * tq, tk)) < jnp.where(rr >= tq, rr - tq, rr)
            log_1mb = jnp.where(causal, log_1mb, 0.0)
        hi, lo = _split_bf16(log_1mb)
        after = _dot(jnp.concatenate([hi, lo], axis=1), tri2) + carry_ref[...]
        a = jnp.exp(log_b + after)
        if diagonal:
            a = jnp.where(causal, a, 0.0)
        carry_ref[...] += jnp.sum(log_1mb, axis=-1, keepdims=True)
        acc_ref[...] += _dot_nt(a.astype(BF16), v_ref[:, pl.ds(k0, tk)].astype(BF16))

    step(i, True)

    def still_visible():
        return jnp.max(carry_ref[...]) > SB_UNDERFLOW

    def body(c):
        j, _ = c
        step(j, False)
        return j - 1, still_visible()

    lax.while_loop(lambda c: jnp.logical_and(c[0] >= 0, c[1]), body, (i - 1, still_visible()))
    acc = acc_ref[...]
    o_ref[...] = jnp.where(lane < HEAD_DIM, acc[0:tq], acc[tq:2 * tq])


def _sb_prompt(projr, sbt, offs, *, tq=2 * Q_BLOCK):
    b, t, _ = projr.shape
    n_pairs = W_C // LANES
    qb = offs['c_q'] // LANES
    return pl.pallas_call(
        _sb_kernel,
        grid=(b, n_pairs, t // tq),
        in_specs=[pl.BlockSpec((None, tq, LANES), lambda bi, p, i: (bi, i, qb + p)),
                  pl.BlockSpec((None, LANES, t), lambda bi, p, i: (bi, p, 0)),
                  pl.BlockSpec((None, LANES, t), lambda bi, p, i: (bi, n_pairs + p, 0))],
        out_specs=pl.BlockSpec((None, tq, LANES), lambda bi, p, i: (bi, i, p)),
        out_shape=jax.ShapeDtypeStruct((b, t, W_C), F32),
        scratch_shapes=[pltpu.VMEM((2 * tq, LANES), BF16),
                        pltpu.VMEM((2 * tq, 1), F32),
                        pltpu.VMEM((2 * tq, LANES), F32)],
        compiler_params=_cparams(3),
        name="sb",
    )(projr, sbt, sbt)


def _merge_kernel(*refs, decode):
    if decode:
        (x_ref, ada_ref, gpost_ref, conv_ref, s0_ref, s1_ref, gm0_ref, gm1_ref, gm2_ref, yb0_ref, yb1_ref,
         yb2_ref, yc_ref, cw_ref, wba_ref, wbb_ref, wbc_ref, wo_ref, o_ref, u_ref) = refs
    else:
        (x_ref, ada_ref, gpost_ref, conv_ref, halo_ref, gm0_ref, gm1_ref, gm2_ref, yb0_ref, yb1_ref, yb2_ref,
         yc_ref, cw_ref, wba_ref, wbb_ref, wbc_ref, wo_ref, o_ref, u_ref) = refs
    tm = x_ref.shape[0]
    u = conv_ref[:, 2 * W_A:3 * W_A] * conv_ref[:, 0:W_A]
    if decode:
        um2 = s0_ref[...]
        um1 = s1_ref[...]
        u_ref[...] = u
    else:
        i = pl.program_id(1)
        keep = (i > 0).astype(F32)
        hu = (halo_ref[:, 2 * W_A:3 * W_A] * halo_ref[:, 0:W_A]) * keep
        row = _row_iota((tm, W_A))
        um1 = jnp.where(row >= 1, pltpu.roll(u, 1, 0), hu[7:8])
        um2 = jnp.where(row >= 2, pltpu.roll(u, 2, 0), jnp.where(row == 1, hu[7:8], hu[6:7]))

        @pl.when(i == pl.num_programs(1) - 1)
        def _():
            u_ref[...] = u[tm - 8:tm]

    y_a = conv_ref[:, W_A:2 * W_A] * (cw_ref[0:1] * um2 + cw_ref[1:2] * um1 + cw_ref[2:3] * u)
    y_b = yb0_ref[...] + yb1_ref[...] + yb2_ref[...]
    merged = (jax.nn.sigmoid(gm0_ref[...]) * _dot(y_a.astype(BF16), wba_ref[...])
              + jax.nn.sigmoid(gm1_ref[...]) * _dot(y_b.astype(BF16), wbb_ref[...])
              + jax.nn.sigmoid(gm2_ref[...]) * _dot(yc_ref[...].astype(BF16), wbc_ref[...]))
    out = _dot(merged.astype(BF16), wo_ref[...])
    o_ref[...] = x_ref[...] + ada_ref[5] * (_rms(out) * gpost_ref[...])


def _merge(x, ada, gpost, projr, conv_state, yb, yc, cw, wba, wbb, wbc, wo, offs, *, tm, decode):
    bg, t, d = x.shape
    mrows = ada.shape[2]
    cb = offs['conv'] // (3 * W_A)
    gmb = offs['g_merge'] // d
    row_spec = lambda w, cblk: pl.BlockSpec((None, tm, w), lambda b, i: (b, i, cblk))
    full = lambda a: pl.BlockSpec(a.shape, lambda b, i: (0,) * a.ndim)
    if decode:
        ctx_specs = [row_spec(W_A, 0), row_spec(W_A, 0)]
        ctx = list(conv_state)
        u_shape, u_spec = (bg, t, W_A), row_spec(W_A, 0)
    else:
        r8 = tm // 8
        ctx_specs = [pl.BlockSpec((None, 8, 3 * W_A), lambda b, i: (b, jnp.maximum(i * r8 - 1, 0), cb))]
        ctx = [projr]
        u_shape, u_spec = (bg, 8, W_A), pl.BlockSpec((None, 8, W_A), lambda b, i: (b, 0, 0))
    out, u = pl.pallas_call(
        functools.partial(_merge_kernel, decode=decode),
        grid=(bg, t // tm),
        in_specs=[row_spec(d, 0),
                  pl.BlockSpec((None, 3 * N_SUB, mrows, d), lambda b, i: (b, 0, 0, 0)),
                  pl.BlockSpec((1, d), lambda b, i: (0, 0)),
                  row_spec(3 * W_A, cb)] + ctx_specs + [
                  row_spec(d, gmb), row_spec(d, gmb + 1), row_spec(d, gmb + 2),
                  row_spec(W_B, 0), row_spec(W_B, 0), row_spec(W_B, 0), row_spec(W_C, 0),
                  full(cw), full(wba), full(wbb), full(wbc), full(wo)],
        out_specs=[row_spec(d, 0), u_spec],
        out_shape=[jax.ShapeDtypeStruct(x.shape, F32), jax.ShapeDtypeStruct(u_shape, F32)],
        compiler_params=_cparams(2),
        name="merge",
    )(x, ada, gpost, projr, *ctx, projr, projr, projr, yb[0], yb[1], yb[2], yc, cw, wba, wbb, wbc, wo)
    return out, u


def _dec_cmp_kernel(pt_ref, *refs, n_pages_step, n_top, past):
    del pt_ref
    pages = refs[:n_pages_step]
    (q_ref, pos_ref, w1_ref, w2_ref, tab_ref, ovl_ref, bg_ref, o_ref, idx_ref, h_ref) = refs[n_pages_step:]
    c = pl.program_id(1)
    lane = _lane_iota((8, LANES))
    rpp = PAGE_SIZE // D_CMP
    ii = _row_iota((PAGE_SIZE, PAGE_SIZE))
    perm = jnp.where(_lane_iota((PAGE_SIZE, PAGE_SIZE)) == D_CMP * (ii % rpp) + ii // rpp, 1.0, 0.0).astype(BF16)
    for k in range(n_pages_step):
        row0 = pl.multiple_of((c * n_pages_step + k) * rpp, rpp)
        xt = pages[k][...].reshape(2 * KV_B, PAGE_SIZE).astype(BF16)
        rows = _dot_nt(perm, xt)
        for m in range(D_CMP // 2):
            ev = rows[2 * m * rpp:(2 * m + 1) * rpp]
            od = rows[(2 * m + 1) * rpp:(2 * m + 2) * rpp]
            for kind in range(2):
                e = ev[:, kind * LANES:(kind + 1) * LANES]
                o = od[:, kind * LANES:(kind + 1) * LANES]
                h_ref[kind, 0, pl.ds(row0, rpp), m * LANES:(m + 1) * LANES] = (
                    jnp.where(lane < HEAD_DIM, e, pltpu.roll(o, HEAD_DIM, 1)))
                h_ref[kind, 1, pl.ds(row0, rpp), m * LANES:(m + 1) * LANES] = (
                    jnp.where(lane < HEAD_DIM, pltpu.roll(e, HEAD_DIM, 1), o))

    @pl.when(c == pl.num_programs(1) - 1)
    def _():
        ncp = h_ref.shape[2]
        half = h_ref.shape[3]
        kv = []
        for kind in range(2):
            out = None
            for g in range(G_B):
                hg = h_ref[kind, g]
                a = _dot((hg + pos_ref[kind, 0]).astype(BF16), w1_ref[kind, 0:half, :])
                bm = _dot((hg + pos_ref[kind, 1]).astype(BF16), w1_ref[kind, half:2 * half, :])
                pre = a + pltpu.roll(bm, ncp - 1, 0)
                o = _dot(_silu(pre).astype(BF16), w2_ref[kind, g])
                out = o if out is None else out + o
            kv.append(out.astype(BF16))
        q8 = _stack_heads(lambda p: q_ref[:, p * LANES:(p + 1) * LANES], 1).astype(BF16)
        s = _dot_nt(q8, kv[0]) + tab_ref[...]
        ok = s > MASKED_BELOW
        e = jnp.where(ok, jnp.exp(s - jnp.max(s, axis=-1, keepdims=True)), 0.0)
        p = (e / jnp.maximum(jnp.sum(e, axis=-1, keepdims=True), 1e-30)).astype(BF16)
        o = _dot(p, kv[1])
        gsig = jax.nn.sigmoid(bg_ref[...])
        for pi, blk in enumerate(_unstack_gated(o, gsig, 0, 1)):
            o_ref[:, pi * LANES:(pi + 1) * LANES] = blk
        imp8 = _dot(p, ovl_ref[...])
        imp = jnp.concatenate([imp8[0:1] + imp8[1:2] + imp8[2:3] + imp8[3:4],
                               imp8[4:5] + imp8[5:6] + imp8[6:7] + imp8[7:8]], axis=0)
        nbl = imp.shape[1]
        j = _lane_iota(imp.shape)
        cur = past // L_SEL
        forced = (j == 0) | (j == cur) | (j == cur - 1)
        x = jnp.where(j * L_SEL <= past, imp + FORCE_BONUS * forced.astype(F32), NEG)
        jf = j.astype(F32)
        slot = _lane_iota((G_B, LANES))
        idx = jnp.zeros((G_B, LANES), F32)
        for it in range(n_top):
            mx = jnp.max(x, axis=-1, keepdims=True)
            first = jnp.min(jnp.where(x == mx, jf, float(nbl)), axis=-1, keepdims=True)
            idx = jnp.where(slot == it, first, idx)
            x = jnp.where(jf == first, -jnp.inf, x)
        idx_ref[...] = jnp.zeros(idx_ref.shape, jnp.int32)
        idx_ref[0:G_B, :] = idx.astype(jnp.int32)


def _dec_cmp(layer, page_table, cache_nsa_t, proj_s, pos, w1, w2p, tab, ovl, offs, *, n_top, n_pages_step=16):
    nseq, n_pages = page_table.shape
    n_pages_step = min(n_pages_step, n_pages)
    assert n_pages % n_pages_step == 0
    past = n_pages * PAGE_SIZE
    ncp = past // D_CMP
    half = D_CMP * HEAD_DIM
    qb = offs['b_q'] // W_B
    gb = offs['b_g'] // LANES
    page_specs = [pl.BlockSpec((None, None, 2, G_B, HEAD_DIM, PAGE_SIZE),
                               functools.partial(
                                   lambda b, c, pt, k: (layer, pt[b, c * n_pages_step + k], 0, 0, 0, 0), k=k))
                  for k in range(n_pages_step)]
    full = lambda a: pl.BlockSpec(a.shape, lambda b, c, pt: (0,) * a.ndim)
    grid_spec = pltpu.PrefetchScalarGridSpec(
        num_scalar_prefetch=1,
        grid=(nseq, n_pages // n_pages_step),
        in_specs=page_specs + [pl.BlockSpec((None, 1, W_B), lambda b, c, pt: (b, 0, qb)),
                               full(pos), full(w1), full(w2p), full(tab), full(ovl),
                               pl.BlockSpec((None, 1, LANES), lambda b, c, pt: (b, 0, gb))],
        out_specs=[pl.BlockSpec((None, 1, W_B), lambda b, c, pt: (b, 0, 0)),
                   pl.BlockSpec((None, 8, LANES), lambda b, c, pt: (b, 0, 0))],
        scratch_shapes=[pltpu.VMEM((2, G_B, ncp, half), F32)],
    )
    return pl.pallas_call(
        functools.partial(_dec_cmp_kernel, n_pages_step=n_pages_step, n_top=n_top, past=past),
        grid_spec=grid_spec,
        out_shape=[jax.ShapeDtypeStruct((nseq, 1, W_B), F32), jax.ShapeDtypeStruct((nseq, 8, LANES), jnp.int32)],
        compiler_params=_cparams(2),
        name="dec_cmp",
    )(page_table, *([cache_nsa_t] * n_pages_step), proj_s, pos, w1, w2p, tab, ovl, proj_s)


def _dec_selwin_kernel(sp_ref, *refs, n_top, past, n_pages):
    n_blk = G_B * n_top
    blocks = refs[:n_blk]
    (q_ref, ksn_ref, vsn_ref, wst_ref, kwn_ref, vwn_ref, tabw_ref, relt_ref, bg_ref,
     osel_ref, owin_ref, kc_ref, vc_ref) = refs[n_blk:]
    b = pl.program_id(0)
    n_past_blk = past // L_SEL
    halves = PAGE_SIZE // L_SEL
    q8f = _stack_heads(lambda p: q_ref[:, p * LANES:(p + 1) * LANES], 1)
    q8 = q8f.astype(BF16)
    q8r = q8.astype(F32)
    gsig = jax.nn.sigmoid(bg_ref[...])
    row1 = _row_iota((H_B, 1))

    for n in range(n_blk):
        kc_ref[:, n * PAGE_SIZE:(n + 1) * PAGE_SIZE] = blocks[n][0].reshape(KV_B, PAGE_SIZE).astype(BF16)
        vc_ref[:, n * PAGE_SIZE:(n + 1) * PAGE_SIZE] = blocks[n][1].reshape(KV_B, PAGE_SIZE).astype(BF16)
    nk = n_blk * PAGE_SIZE
    s = _dot(q8, kc_ref[...])
    lane = _lane_iota((H_B, nk))
    slot = lane // PAGE_SIZE
    within = lane % PAGE_SIZE
    blk_of = jnp.zeros((H_B, nk), jnp.int32)
    has_new = [jnp.int32(0)] * G_B
    for n in range(n_blk):
        bi = sp_ref[b, n_pages + n]
        blk_of = jnp.where(slot == n, bi, blk_of)
        has_new[n // n_top] = jnp.maximum(has_new[n // n_top], (bi == n_past_blk).astype(jnp.int32))
    kpos = (blk_of // halves) * PAGE_SIZE + within
    dist = past - kpos
    row = _row_iota((H_B, nk))
    ok = ((within // L_SEL == blk_of % halves) & (blk_of < n_past_blk)
          & (row // R_B == slot // n_top))
    bias = jnp.broadcast_to(relt_ref[:, 0:1], (H_B, nk))
    for k, thr in enumerate(_T5_THR):
        bias = jnp.where(dist >= thr, relt_ref[:, k + 1:k + 2], bias)
    s = jnp.where(ok, s + bias, NEG)
    new_ok = jnp.where(row1 < R_B, has_new[0], has_new[1]) > 0
    s_new = jnp.sum(q8r * ksn_ref[...].astype(BF16).astype(F32), axis=-1, keepdims=True)
    s_new = jnp.where(new_ok, s_new + relt_ref[:, 0:1], NEG)
    m = jnp.maximum(jnp.max(s, axis=-1, keepdims=True), s_new)
    e = jnp.where(ok, jnp.exp(s - m), 0.0)
    e_new = jnp.where(new_ok, jnp.exp(s_new - m), 0.0)
    den = jnp.maximum(jnp.sum(e, axis=-1, keepdims=True) + e_new, 1e-30)
    o = (_dot_nt(e.astype(BF16), vc_ref[...]) + e_new * vsn_ref[...].astype(BF16).astype(F32)) / den
    for pi, blk in enumerate(_unstack_gated(o, gsig, 1, 1)):
        osel_ref[:, pi * LANES:(pi + 1) * LANES] = blk

    n_ctx = wst_ref.shape[-1]
    sw = _dot(q8, wst_ref[0].reshape(KV_B, n_ctx).astype(BF16)) + tabw_ref[...]
    okw = sw > MASKED_BELOW
    sw_new = jnp.sum(q8r * kwn_ref[...].astype(BF16).astype(F32), axis=-1, keepdims=True) + relt_ref[:, 0:1]
    mw = jnp.maximum(jnp.max(sw, axis=-1, keepdims=True), sw_new)
    ew = jnp.where(okw, jnp.exp(sw - mw), 0.0)
    ew_new = jnp.exp(sw_new - mw)
    denw = jnp.maximum(jnp.sum(ew, axis=-1, keepdims=True) + ew_new, 1e-30)
    ow = (_dot_nt(ew.astype(BF16), wst_ref[1].reshape(KV_B, n_ctx).astype(BF16))
          + ew_new * vwn_ref[...].astype(BF16).astype(F32)) / denw
    for pi, blk in enumerate(_unstack_gated(ow, gsig, 2, 1)):
        owin_ref[:, pi * LANES:(pi + 1) * LANES] = blk


def _dec_selwin(layer, sel_table, n_pages, cache_nsa_t, proj_s, win_state_t, tabw, rel_t, offs, *, n_top):
    nseq = sel_table.shape[0]
    past = n_pages * PAGE_SIZE
    n_blk = G_B * n_top
    n_past_blk = past // L_SEL
    halves = PAGE_SIZE // L_SEL
    n_ctx = win_state_t.shape[-1]
    qb = offs['b_q'] // W_B
    gb = offs['b_g'] // LANES
    nb = offs['nsa_s'] // LANES
    wb = offs['win_s'] // LANES

    def blk_map(b, sp, n):
        bi = jnp.clip(sp[b, n_pages + n], 0, n_past_blk - 1)
        return (layer, sp[b, bi // halves], 1, 0, 0, 0)

    blk_specs = [pl.BlockSpec((None, None, 2, G_B, HEAD_DIM, PAGE_SIZE), functools.partial(blk_map, n=n))
                 for n in range(n_blk)]
    row = lambda cblk: pl.BlockSpec((None, 1, LANES), lambda b, sp: (b, 0, cblk))
    full = lambda a: pl.BlockSpec(a.shape, lambda b, sp: (0,) * a.ndim)
    grid_spec = pltpu.PrefetchScalarGridSpec(
        num_scalar_prefetch=1,
        grid=(nseq,),
        in_specs=blk_specs + [pl.BlockSpec((None, 1, W_B), lambda b, sp: (b, 0, qb)),
                              row(nb + 2), row(nb + 3),
                              pl.BlockSpec((None, None, 2, G_B, HEAD_DIM, n_ctx),
                                           lambda b, sp: (layer, b, 0, 0, 0, 0)),
                              row(wb), row(wb + 1), full(tabw), full(rel_t), row(gb)],
        out_specs=[pl.BlockSpec((None, 1, W_B), lambda b, sp: (b, 0, 0)),
                   pl.BlockSpec((None, 1, W_B), lambda b, sp: (b, 0, 0))],
        scratch_shapes=[pltpu.VMEM((KV_B, n_blk * PAGE_SIZE), BF16), pltpu.VMEM((KV_B, n_blk * PAGE_SIZE), BF16)],
    )
    return pl.pallas_call(
        functools.partial(_dec_selwin_kernel, n_top=n_top, past=past, n_pages=n_pages),
        grid_spec=grid_spec,
        out_shape=[jax.ShapeDtypeStruct((nseq, 1, W_B), F32)] * 2,
        compiler_params=_cparams(1),
        name="dec_selwin",
    )(sel_table, *([cache_nsa_t] * n_blk), proj_s, proj_s, proj_s, win_state_t, proj_s, proj_s,
      tabw, rel_t, proj_s)


def _dec_sb_kernel(sp_ref, *refs, n_pages_step, tail, n_pages):
    pages = refs[:n_pages_step]
    if tail:
        q_ref, acc_in_ref, carry_in_ref, o_ref, qcol_ref, carry_ref, acc_ref = refs[n_pages_step:]
    else:
        q_ref, acc_out_ref, carry_out_ref, qcol_ref, carry_ref, acc_ref = refs[n_pages_step:]
    b = pl.program_id(0)
    c = pl.program_id(1)
    ps = PAGE_SIZE

    @pl.when(c == 0)
    def _():
        qt = jnp.broadcast_to(q_ref[...], (LANES, W_C)).T
        qcol_ref[...] = qt.reshape(H_C, HEAD_DIM, LANES)
        if tail:
            carry_ref[...] = carry_in_ref[:, 0:1]
            acc_ref[...] = acc_in_ref[...]
        else:
            carry_ref[...] = jnp.zeros_like(carry_ref)
            acc_ref[...] = jnp.zeros_like(acc_ref)

    def sweep():
        tri = jnp.where(_row_iota((ps, ps)) > _lane_iota((ps, ps)), 1.0, 0.0).astype(BF16)
        hrow = _row_iota((H_C, ps))
        for k in range(n_pages_step):
            z = jnp.zeros((H_C, ps), F32)
            for h in range(H_C):
                zh = jnp.sum(qcol_ref[h] * pages[k][0, h], axis=0, keepdims=True)
                z = jnp.where(hrow == h, zh, z)
            t = jnp.log1p(jnp.exp(-jnp.abs(z)))
            log_1mb = -(jnp.maximum(z, 0.0) + t)
            log_b = jnp.minimum(z, 0.0) - t
            hi, lo = _split_bf16(log_1mb)
            after = _dot(hi, tri) + _dot(lo, tri) + carry_ref[...]
            a = jnp.exp(log_b + after)
            carry_ref[...] += jnp.sum(log_1mb, axis=-1, keepdims=True)
            for h in range(H_C):
                acc_ref[h] += a[h:h + 1, :] * pages[k][1, h]

    if tail:
        pl.when(sp_ref[b, n_pages] > 0)(sweep)

        @pl.when(c == pl.num_programs(1) - 1)
        def _():
            acc = acc_ref[...].reshape(W_C, LANES)
            ones = jnp.ones((8, LANES), BF16)
            hi = acc.astype(BF16)
            r1 = acc - hi.astype(F32)
            mid = r1.astype(BF16)
            lo = (r1 - mid.astype(F32)).astype(BF16)
            tot = _dot_nt(ones, hi) + _dot_nt(ones, mid) + _dot_nt(ones, lo)
            o_ref[...] = tot[0:1]
    else:
        sweep()

        @pl.when(c == pl.num_programs(1) - 1)
        def _():
            acc_out_ref[...] = acc_ref[...]
            carry_out_ref[...] = jnp.broadcast_to(carry_ref[...], (H_C, LANES))


def _dec_sb(layer, page_table, cache_sb_t, proj_s, offs):
    nseq, n_pages = page_table.shape
    assert n_pages >= 2
    step = min(8, n_pages // 2)
    assert n_pages % step == 0
    n_head = step
    qb = offs['c_q'] // W_C
    page_block = (None, None, 2, H_C, HEAD_DIM, PAGE_SIZE)
    scratch = [pltpu.VMEM((H_C, HEAD_DIM, LANES), F32), pltpu.VMEM((H_C, 1), F32),
               pltpu.VMEM((H_C, HEAD_DIM, LANES), F32)]
    q_spec = pl.BlockSpec((None, 1, W_C), lambda b, c, sp: (b, 0, qb))
    acc_spec = pl.BlockSpec((None, H_C, HEAD_DIM, LANES), lambda b, c, sp: (b, 0, 0, 0))
    carry_spec = pl.BlockSpec((None, H_C, LANES), lambda b, c, sp: (b, 0, 0))

    head_specs = [pl.BlockSpec(page_block, functools.partial(
        lambda b, c, sp, k: (layer, sp[b, n_pages - 1 - (c * step + k)], 0, 0, 0, 0), k=k)) for k in range(step)]
    acc, carry = pl.pallas_call(
        functools.partial(_dec_sb_kernel, n_pages_step=step, tail=False, n_pages=n_pages),
        grid_spec=pltpu.PrefetchScalarGridSpec(
            num_scalar_prefetch=1, grid=(nseq, n_head // step),
            in_specs=head_specs + [q_spec], out_specs=[acc_spec, carry_spec], scratch_shapes=scratch),
        out_shape=[jax.ShapeDtypeStruct((nseq, H_C, HEAD_DIM, LANES), F32),
                   jax.ShapeDtypeStruct((nseq, H_C, LANES), F32)],
        compiler_params=_cparams(2),
        name="dec_sb_head",
    )(page_table, *([cache_sb_t] * step), proj_s)

    flag = (jnp.max(carry[:, :, 0], axis=1) > SB_UNDERFLOW).astype(jnp.int32)
    table = jnp.concatenate([page_table, flag[:, None]], axis=1)
    table = jnp.pad(table, ((0, 0), (0, -table.shape[1] % LANES)))

    def tail_map(b, c, sp, k):
        page = sp[b, n_pages - 1 - (n_head + c * step + k)]
        return (layer, jnp.where(sp[b, n_pages] > 0, page, 0), 0, 0, 0, 0)

    tail_specs = [pl.BlockSpec(page_block, functools.partial(tail_map, k=k)) for k in range(step)]
    return pl.pallas_call(
        functools.partial(_dec_sb_kernel, n_pages_step=step, tail=True, n_pages=n_pages),
        grid_spec=pltpu.PrefetchScalarGridSpec(
            num_scalar_prefetch=1, grid=(nseq, (n_pages - n_head) // step),
            in_specs=tail_specs + [q_spec, acc_spec, carry_spec],
            out_specs=pl.BlockSpec((None, 1, W_C), lambda b, c, sp: (b, 0, 0)), scratch_shapes=scratch),
        out_shape=jax.ShapeDtypeStruct((nseq, 1, W_C), F32),
        compiler_params=_cparams(2),
        name="dec_sb_tail",
    )(table, *([cache_sb_t] * step), proj_s, acc, carry)


def _overlap(ncp, n_lanes, reps):
    c0 = np.arange(ncp)[:, None] * D_CMP
    s0 = np.arange(n_lanes)[None, :] * L_SEL
    ov = ((c0 < s0 + L_SEL) & (c0 + L_CMP > s0)).astype(np.float32)
    return jnp.asarray(np.tile(ov, (1, reps)), BF16)


def _flat16_t(kt):
    b, _, t = kt.shape
    r = kt.reshape(b, G_B, HEAD_DIM, t // D_CMP, D_CMP).transpose(0, 1, 3, 4, 2)
    return r.reshape(b, G_B, t // D_CMP, D_CMP * HEAD_DIM)


def kernel(x_prompt, x_sample, cache_sb, cache_nsa, state_win, state_conv, page_table, c_prompt, c_sample,
           rel_bias, norm_pre, norm_post, w_ada, b_ada, w_ffn_gate, w_ffn_up, w_ffn_down, w_in, conv_w,
           cmp_pos, w_cmp1, w_cmp2, w_branch, w_out):
    batch, seq, d = x_prompt.shape
    nseq = x_sample.shape[0]
    depth = w_in.shape[0]
    n_pages = page_table.shape[1]
    past = n_pages * PAGE_SIZE
    n_ctx = state_win.shape[2]
    assert x_sample.shape[1] == 1 and seq % Q_BLOCK == 0 and seq // L_SEL <= L_SEL
    assert cache_sb.shape[2] == PAGE_SIZE and n_ctx == WINDOW
    offs, row_part, t_part, n_row, n_all = _layout(d)
    half = D_CMP * HEAD_DIM

    q_names = ('b_q', 'c_q')
    w_row = _gather_cols(w_in, row_part, q_names)
    w_kv = _gather_cols(w_in, t_part, q_names)
    tn = n_row // 2
    tn_s = 2048
    assert tn % LANES == 0
    n_s = -(-n_all // tn_s) * tn_s
    w_r = w_row.astype(BF16)
    w_t = w_kv.transpose(0, 2, 1).astype(BF16)
    w_s = jnp.concatenate([w_row, w_kv, jnp.zeros((depth, d, n_s - n_all), F32)], axis=-1).astype(BF16)
    wg = w_ffn_gate.astype(BF16)
    wu = w_ffn_up.astype(BF16)
    wd = w_ffn_down.astype(BF16)
    wba = w_branch[:, 0:W_A].astype(BF16)
    hb = lambda h: w_branch[:, W_A + h * HEAD_DIM:W_A + (h + 1) * HEAD_DIM]
    wbb = jnp.concatenate([x for p in range(R_B) for x in (hb(p), hb(p + R_B))], axis=1).astype(BF16)
    wbc = w_branch[:, W_A + W_B:].astype(BF16)
    wo = w_out.astype(BF16)
    w1 = w_cmp1.astype(BF16)
    z = jnp.zeros_like(w_cmp2)
    w2p = jnp.stack([jnp.concatenate([w_cmp2, z], axis=-1), jnp.concatenate([z, w_cmp2], axis=-1)],
                    axis=2).astype(BF16)
    pos = cmp_pos.reshape(depth, 2, 2, 1, half)

    rows_c = -(-(batch + nseq) // 16) * 16
    c_all = jnp.concatenate([c_prompt, c_sample, jnp.zeros((rows_c - batch - nseq, d), F32)], axis=0)
    ada_all = _ada_all(c_all, w_ada, b_ada)
    ada_p = ada_all[:, :batch].reshape(depth, batch, 3 * N_SUB, 1, d)
    ada_s = ada_all[:, batch:batch + nseq].reshape(depth, nseq, 3 * N_SUB, d).transpose(0, 2, 1, 3)[:, None]

    tq = Q_BLOCK
    ncp_p = seq // D_CMP
    tab_sel = _bias_table(rel_bias, 4, 2 * tq, tq, base0=0, tile_step=tq, row_step=-1, col_step=-1,
                          heads_on_lanes=True)
    tab_win = _bias_table(rel_bias, 6, 2 * tq, tq, base0=0, tile_step=tq, row_step=-1, col_step=-1,
                          max_dist=WINDOW, heads_on_lanes=True)
    tab_cmp = _bias_table(rel_bias, seq // tq, tq, ncp_p, base0=-(L_CMP - 1), tile_step=tq, row_step=1,
                          col_step=D_CMP)
    ncp_s = past // D_CMP
    tab_cmp_s = _bias_table(rel_bias, 1, 1, ncp_s, base0=past - (L_CMP - 1), tile_step=0, row_step=0,
                            col_step=D_CMP)[0]
    tab_win_s = _bias_table(rel_bias, 1, 1, n_ctx, base0=n_ctx, tile_step=0, row_step=0, col_step=1,
                            max_dist=WINDOW)[0]
    rel_t = rel_bias.T
    ovl_p = _overlap(ncp_p, L_SEL, G_B)
    nsl_s = -(-(past // L_SEL + 1) // LANES) * LANES
    ovl_s = _overlap(ncp_s, nsl_s, 1)
    n_top_p = min(N_SEL, -(-seq // L_SEL))
    n_top_s = min(N_SEL, -(-(past + 1) // L_SEL))

    cache_sb_t = cache_sb.transpose(0, 1, 3, 4, 5, 2)
    cache_nsa_t = cache_nsa.transpose(0, 1, 3, 4, 5, 2)
    win_state_t = state_win.transpose(0, 1, 3, 4, 5, 2)

    tm_p = 512 if seq % 512 == 0 else Q_BLOCK
    tf = w_ffn_gate.shape[-1] // 2

    y_p = x_prompt
    y_s = x_sample.reshape(1, nseq, d)
    outs = {k: [] for k in ('sb_p', 'sb_s', 'nsa_p', 'nsa_s', 'win_p', 'win_s', 'conv_p', 'conv_s')}
    for l in range(depth):
        gpre = [norm_pre[l, i][None] for i in range(N_SUB)]
        gpost = [norm_post[l, i][None] for i in range(N_SUB)]
        y_p = _ffn(y_p, ada_p[l], gpre[0], gpost[0], wg[l, 0], wu[l, 0], wd[l, 0], sub=0, tm=tm_p, tf=tf)
        projr = _proj(y_p, ada_p[l], gpre[1], w_r[l], tm=tm_p, tn=tn)
        sbt, nsat, wint = _proj_t(y_p, ada_p[l], gpre[1], w_t[l], tm=tm_p)
        hrows = jnp.stack([_flat16_t(nsat[:, 0:KV_B]), _flat16_t(nsat[:, KV_B:2 * KV_B])], axis=1)
        kvc = _compress_prompt(hrows, pos[l], w1[l], w2p[l])
        y_cmp, pen = _cmp_prompt(projr, kvc, tab_cmp, ovl_p, offs, n_top=n_top_p)
        y_sel = _nsa_flash_prompt(projr, nsat, tab_sel, pen, offs, mode='sel')
        y_win = _nsa_flash_prompt(projr, wint, tab_win, pen, offs, mode='win')
        y_c = _sb_prompt(projr, sbt, offs)
        y_p, u_tail = _merge(y_p, ada_p[l], gpost[1], projr, None, (y_cmp, y_sel, y_win), y_c, conv_w[l],
                             wba[l], wbb[l], wbc[l], wo[l], offs, tm=tm_p, decode=False)
        y_p = _ffn(y_p, ada_p[l], gpre[2], gpost[2], wg[l, 1], wu[l, 1], wd[l, 1], sub=2, tm=tm_p, tf=tf)
        n_keep = min(WINDOW, seq)
        to_state = lambda a, k, g: a.reshape(batch, k, g, HEAD_DIM, a.shape[-1]).transpose(0, 4, 1, 2, 3)
        outs['sb_p'].append(to_state(sbt, 2, H_C))
        outs['nsa_p'].append(to_state(nsat, 4, G_B))
        outs['win_p'].append(to_state(wint[:, :, seq - n_keep:], 2, G_B))
        outs['conv_p'].append(u_tail[:, 8 - (CONV_W - 1):])
        y_s = _ffn(y_s, ada_s[l], gpre[0], gpost[0], wg[l, 0], wu[l, 0], wd[l, 0], sub=0, tm=nseq, tf=tf)
        proj_s = _proj(y_s, ada_s[l], gpre[1], w_s[l], tm=nseq, tn=tn_s)
        proj_r = proj_s.reshape(nseq, 1, n_s)
        s_cmp, top_idx = _dec_cmp(l, page_table, cache_nsa_t, proj_r, pos[l], w1[l], w2p[l], tab_cmp_s, ovl_s,
                                  offs, n_top=n_top_s)
        sel_table = jnp.concatenate([page_table, top_idx[:, 0, 0:n_top_s], top_idx[:, 1, 0:n_top_s]], axis=1)
        sel_table = jnp.pad(sel_table, ((0, 0), (0, -sel_table.shape[1] % LANES)))
        s_sel, s_win = _dec_selwin(l, sel_table, n_pages, cache_nsa_t, proj_r, win_state_t, tab_win_s,
                                   rel_t, offs, n_top=n_top_s)
        s_c = _dec_sb(l, page_table, cache_sb_t, proj_r, offs)
        to_rows = lambda a: a.reshape(1, nseq, a.shape[-1])
        ctx = (state_conv[l, :, 0][None], state_conv[l, :, 1][None])
        y_s, u_s = _merge(y_s, ada_s[l], gpost[1], proj_s, ctx, (to_rows(s_cmp), to_rows(s_sel), to_rows(s_win)),
                          to_rows(s_c), conv_w[l], wba[l], wbb[l], wbc[l], wo[l], offs, tm=nseq, decode=True)
        y_s = _ffn(y_s, ada_s[l], gpre[2], gpost[2], wg[l, 1], wu[l, 1], wd[l, 1], sub=2, tm=nseq, tf=tf)
        ps = proj_s[0]
        o_s, o_n, o_w = offs['sb_s'], offs['nsa_s'], offs['win_s']
        outs['sb_s'].append(ps[:, o_s:o_s + KT_SB].reshape(nseq, 1, 2, H_C, HEAD_DIM))
        outs['nsa_s'].append(ps[:, o_n:o_n + KT_NSA].reshape(nseq, 1, 4, G_B, HEAD_DIM))
        win_new = ps[:, o_w:o_w + KT_WIN].reshape(nseq, 1, 2, G_B, HEAD_DIM)
        outs['win_s'].append(jnp.concatenate([state_win[l][:, 1:], win_new], axis=1))
        outs['conv_s'].append(jnp.stack([state_conv[l, :, 1], u_s[0]], axis=1))
    st = {k: jnp.stack(v) for k, v in outs.items()}
    return (y_p, y_s.reshape(nseq, 1, d), st['sb_p'], st['sb_s'], st['nsa_p'], st['nsa_s'],
            st['win_p'], st['win_s'], st['conv_p'], st['conv_s'])
```

```python
import functools
import math

import numpy as np
import jax
import jax.numpy as jnp
from jax import lax
from jax.experimental import pallas as pl
from jax.experimental.pallas import tpu as pltpu

F32 = jnp.float32
BF16 = jnp.bfloat16

HEAD_DIM = 64
CONV_CH = 512
CONV_W = 3
H_B = 8
G_B = 2
R_B = H_B // G_B
L_CMP = 32
D_CMP = 16
L_SEL = 64
N_SEL = 16
WINDOW = 512
H_C = 8
N_BUCKETS = 32
MAX_DIST = 128
N_SUB = 3
Q_BLOCK = 128
PAGE_SIZE = 128
EPS = 1e-6
NEG = -1e30
MASKED_BELOW = -1e29
SB_UNDERFLOW = -105.0
FORCE_BONUS = 1e4

W_A = CONV_CH
W_B = H_B * HEAD_DIM
W_C = H_C * HEAD_DIM
KV_B = G_B * HEAD_DIM
LANES = 128
VMEM_LIMIT = 56 * 1024 * 1024

KT_SB = 2 * W_C
KT_NSA = 4 * KV_B
KT_WIN = 2 * KV_B


def _layout(d_model):
    sizes = (W_A, W_A, W_A, W_B, KV_B, KV_B, KV_B, KV_B, KV_B, KV_B, 3 * H_B, W_C, W_C, W_C, 3 * d_model)
    starts = np.concatenate([[0], np.cumsum(sizes)]).astype(np.int64)
    names = ('a_v', 'a_b', 'a_c', 'b_q', 'b_kc', 'b_vc', 'b_ks', 'b_vs', 'b_kw', 'b_vw', 'b_g',
             'c_q', 'c_k', 'c_v', 'g_merge')
    seg = {n: (int(starts[i]), int(starts[i + 1])) for i, n in enumerate(names)}
    q0 = seg['b_q'][0]
    head = lambda h: (q0 + h * HEAD_DIM, q0 + (h + 1) * HEAD_DIM)
    bq_pairs = []
    for p in range(R_B):
        bq_pairs += [head(p), head(p + R_B)]
    row_part = [('g_merge', [seg['g_merge']], 0),
                ('conv', [seg['a_v'], seg['a_b'], seg['a_c']], 0),
                ('b_q', bq_pairs, 0),
                ('c_q', [seg['c_q']], 0),
                ('b_g', [seg['b_g']], LANES - 3 * H_B),
                ('k_rows', [seg['b_ks'], seg['b_kw']], LANES)]
    t_part = [('sb', [seg['c_k'], seg['c_v']], 0),
              ('nsa', [seg['b_kc'], seg['b_vc'], seg['b_ks'], seg['b_vs']], 0),
              ('win', [seg['b_kw'], seg['b_vw']], 0)]
    offs, o = {}, 0
    for n, rngs, padn in row_part:
        offs[n] = o
        o += sum(b - a for a, b in rngs) + padn
    n_row = o
    for n, rngs, padn in t_part:
        offs[n + '_s'] = o
        o += sum(b - a for a, b in rngs) + padn
    return offs, row_part, t_part, n_row, o


def _gather_cols(w, parts, scale_names):
    cols = []
    for n, rngs, padn in parts:
        for a, b in rngs:
            c = w[:, :, a:b]
            cols.append(c * (HEAD_DIM ** -0.5) if n in scale_names else c)
        if padn:
            cols.append(jnp.zeros(w.shape[:2] + (padn,), w.dtype))
    return jnp.concatenate(cols, axis=-1)


def _t5_thresholds():
    max_exact = N_BUCKETS // 2
    def bucket(n):
        if n < max_exact:
            return n
        b = max_exact + int(math.log(max(n, 1) / max_exact) / math.log(MAX_DIST / max_exact)
                            * (N_BUCKETS - max_exact))
        return min(b, N_BUCKETS - 1)
    thr = []
    for k in range(1, N_BUCKETS):
        n = 0
        while bucket(n) < k:
            n += 1
        thr.append(n)
    return tuple(thr)


_T5_THR = _t5_thresholds()


def _cparams(n_grid, vmem=VMEM_LIMIT):
    return pltpu.CompilerParams(dimension_semantics=("arbitrary",) * n_grid, vmem_limit_bytes=vmem)


def _dot(a, b):
    return jnp.dot(a, b, preferred_element_type=F32)


def _dot_nt(a, b):
    return lax.dot_general(a, b, (((1,), (1,)), ((), ())), preferred_element_type=F32)


def _rms(x):
    return x * lax.rsqrt(jnp.mean(x * x, axis=-1, keepdims=True) + EPS)


def _silu(x):
    return x * jax.nn.sigmoid(x)


def _lane_iota(shape):
    return lax.broadcasted_iota(jnp.int32, shape, len(shape) - 1)


def _row_iota(shape):
    return lax.broadcasted_iota(jnp.int32, shape, len(shape) - 2)


def _split_bf16(x):
    hi = x.astype(BF16)
    lo = (x - hi.astype(F32)).astype(BF16)
    return hi, lo


def _ada_kernel(c_ref, w_ref, b_ref, o_ref):
    h = _silu(c_ref[...]).astype(BF16)
    o_ref[...] = _dot(h, w_ref[...].astype(BF16)) + b_ref[...]


def _ada_all(c_all, w_ada, b_ada, tn=1536):
    depth, d, n = w_ada.shape
    rows = c_all.shape[0]
    return pl.pallas_call(
        _ada_kernel,
        grid=(depth, n // tn),
        in_specs=[pl.BlockSpec((rows, d), lambda l, j: (0, 0)),
                  pl.BlockSpec((None, d, tn), lambda l, j: (l, 0, j)),
                  pl.BlockSpec((None, 1, tn), lambda l, j: (l, 0, j))],
        out_specs=pl.BlockSpec((None, rows, tn), lambda l, j: (l, 0, j)),
        out_shape=jax.ShapeDtypeStruct((depth, rows, n), F32),
        compiler_params=_cparams(2),
        name="ada",
    )(c_all, w_ada, b_ada.reshape(depth, 1, n))


def _table_kernel(rel_ref, o_ref, *, base0, tile_step, row_step, col_step, max_dist):
    t = pl.program_id(0)
    h = pl.program_id(1)
    shape = o_ref.shape
    dist = (base0 + t * tile_step + _row_iota(shape) * row_step - _lane_iota(shape) * col_step)
    b = jnp.full(shape, rel_ref[0, h], F32)
    for k, thr in enumerate(_T5_THR):
        b = jnp.where(dist >= thr, rel_ref[k + 1, h], b)
    ok = dist >= 0
    if max_dist is not None:
        ok = ok & (dist <= max_dist)
    o_ref[...] = jnp.where(ok, b, NEG)


def _bias_table(rel_bias, n_tiles, rows, cols, *, base0, tile_step, row_step, col_step, max_dist=None,
                heads_on_lanes=False):
    n_heads = rel_bias.shape[1]
    if heads_on_lanes:
        return pl.pallas_call(
            functools.partial(_table_kernel, base0=base0, tile_step=tile_step, row_step=row_step,
                              col_step=col_step, max_dist=max_dist),
            grid=(n_tiles, n_heads),
            in_specs=[pl.BlockSpec(memory_space=pltpu.SMEM)],
            out_specs=pl.BlockSpec((None, rows, cols), lambda t, h: (t, 0, h)),
            out_shape=jax.ShapeDtypeStruct((n_tiles, rows, n_heads * cols), F32),
            compiler_params=_cparams(2),
            name="bias_table",
        )(rel_bias)
    out = pl.pallas_call(
        functools.partial(_table_kernel, base0=base0, tile_step=tile_step, row_step=row_step,
                          col_step=col_step, max_dist=max_dist),
        grid=(n_tiles, n_heads),
        in_specs=[pl.BlockSpec(memory_space=pltpu.SMEM)],
        out_specs=pl.BlockSpec((None, None, rows, cols), lambda t, h: (t, h, 0, 0)),
        out_shape=jax.ShapeDtypeStruct((n_tiles, n_heads, rows, cols), F32),
        compiler_params=_cparams(2),
        name="bias_table",
    )(rel_bias)
    return out.reshape(n_tiles, n_heads * rows, cols)


def _ffn_kernel(x_ref, ada_ref, gpre_ref, gpost_ref, wg_ref, wu_ref, wd_ref, o_ref, h_ref, acc_ref, *, sub):
    f = pl.program_id(2)

    @pl.when(f == 0)
    def _():
        h = _rms(x_ref[...]) * gpre_ref[...]
        h = h * (1.0 + ada_ref[3 * sub + 1]) + ada_ref[3 * sub]
        h_ref[...] = h.astype(BF16)
        acc_ref[...] = jnp.zeros_like(acc_ref)

    h = h_ref[...]
    a = _silu(_dot(h, wg_ref[...])) * _dot(h, wu_ref[...])
    acc_ref[...] += _dot(a.astype(BF16), wd_ref[...])

    @pl.when(f == pl.num_programs(2) - 1)
    def _():
        y = _rms(acc_ref[...]) * gpost_ref[...]
        o_ref[...] = x_ref[...] + 0.5 * (ada_ref[3 * sub + 2] * y)


def _ffn(x, ada, gpre, gpost, wg, wu, wd, *, sub, tm, tf):
    bg, t, d = x.shape
    f = wg.shape[1]
    mrows = ada.shape[2]
    return pl.pallas_call(
        functools.partial(_ffn_kernel, sub=sub),
        grid=(bg, t // tm, f // tf),
        in_specs=[pl.BlockSpec((None, tm, d), lambda b, i, j: (b, i, 0)),
                  pl.BlockSpec((None, 3 * N_SUB, mrows, d), lambda b, i, j: (b, 0, 0, 0)),
                  pl.BlockSpec((1, d), lambda b, i, j: (0, 0)),
                  pl.BlockSpec((1, d), lambda b, i, j: (0, 0)),
                  pl.BlockSpec((d, tf), lambda b, i, j: (0, j)),
                  pl.BlockSpec((d, tf), lambda b, i, j: (0, j)),
                  pl.BlockSpec((tf, d), lambda b, i, j: (j, 0))],
        out_specs=pl.BlockSpec((None, tm, d), lambda b, i, j: (b, i, 0)),
        out_shape=jax.ShapeDtypeStruct(x.shape, F32),
        scratch_shapes=[pltpu.VMEM((tm, d), BF16), pltpu.VMEM((tm, d), F32)],
        compiler_params=_cparams(3),
        name="ffn",
    )(x, ada, gpre, gpost, wg, wu, wd)


def _mod1(x_ref, ada_ref, gpre_ref):
    h = _rms(x_ref[...]) * gpre_ref[...]
    return (h * (1.0 + ada_ref[4]) + ada_ref[3]).astype(BF16)


def _proj_kernel(x_ref, ada_ref, gpre_ref, w_ref, o_ref, h_ref):
    @pl.when(pl.program_id(2) == 0)
    def _():
        h_ref[...] = _mod1(x_ref, ada_ref, gpre_ref)

    o_ref[...] = _dot(h_ref[...], w_ref[...])


def _proj(x, ada, gpre, w, *, tm, tn):
    bg, t, d = x.shape
    n = w.shape[1]
    mrows = ada.shape[2]
    return pl.pallas_call(
        _proj_kernel,
        grid=(bg, t // tm, n // tn),
        in_specs=[pl.BlockSpec((None, tm, d), lambda b, i, j: (b, i, 0)),
                  pl.BlockSpec((None, 3 * N_SUB, mrows, d), lambda b, i, j: (b, 0, 0, 0)),
                  pl.BlockSpec((1, d), lambda b, i, j: (0, 0)),
                  pl.BlockSpec((d, tn), lambda b, i, j: (0, j))],
        out_specs=pl.BlockSpec((None, tm, tn), lambda b, i, j: (b, i, j)),
        out_shape=jax.ShapeDtypeStruct((bg, t, n), F32),
        scratch_shapes=[pltpu.VMEM((tm, d), BF16)],
        compiler_params=_cparams(3),
        name="proj",
    )(x, ada, gpre, w)


def _proj_t_kernel(x_ref, ada_ref, gpre_ref, w_ref, sb_ref, nsa_ref, win_ref):
    h = _mod1(x_ref, ada_ref, gpre_ref)
    kt = _dot_nt(w_ref[...], h)
    sb_ref[...] = kt[0:KT_SB]
    nsa_ref[...] = kt[KT_SB:KT_SB + KT_NSA]
    win_ref[...] = kt[KT_SB + KT_NSA:KT_SB + KT_NSA + KT_WIN]


def _proj_t(x, ada, gpre, w_t, *, tm):
    bg, t, d = x.shape
    nf = w_t.shape[0]
    out = lambda rows: (pl.BlockSpec((None, rows, tm), lambda b, i: (b, 0, i)),
                        jax.ShapeDtypeStruct((bg, rows, t), F32))
    specs, shapes = zip(out(KT_SB), out(KT_NSA), out(KT_WIN))
    return pl.pallas_call(
        _proj_t_kernel,
        grid=(bg, t // tm),
        in_specs=[pl.BlockSpec((None, tm, d), lambda b, i: (b, i, 0)),
                  pl.BlockSpec((None, 3 * N_SUB, 1, d), lambda b, i: (b, 0, 0, 0)),
                  pl.BlockSpec((1, d), lambda b, i: (0, 0)),
                  pl.BlockSpec((nf, d), lambda b, i: (0, 0))],
        out_specs=list(specs),
        out_shape=list(shapes),
        compiler_params=_cparams(2),
        name="proj_t",
    )(x, ada, gpre, w_t)


def _compress_kernel(h_ref, pos_ref, w1_ref, w2_ref, o_ref):
    ncp = h_ref.shape[1]
    half = h_ref.shape[2]
    out = None
    for g in range(G_B):
        hg = h_ref[g]
        a = _dot((hg + pos_ref[0]).astype(BF16), w1_ref[0:half, :])
        bm = _dot((hg + pos_ref[1]).astype(BF16), w1_ref[half:2 * half, :])
        pre = a + pltpu.roll(bm, ncp - 1, 0)
        o = _dot(_silu(pre).astype(BF16), w2_ref[g])
        out = o if out is None else out + o
    o_ref[...] = out


def _compress_prompt(hrows, pos, w1, w2p):
    b, _, g, ncp, half = hrows.shape
    hid = w1.shape[2]
    return pl.pallas_call(
        _compress_kernel,
        grid=(b, 2),
        in_specs=[pl.BlockSpec((None, None, g, ncp, half), lambda i, k: (i, k, 0, 0, 0)),
                  pl.BlockSpec((None, 2, 1, half), lambda i, k: (k, 0, 0, 0)),
                  pl.BlockSpec((None, 2 * half, hid), lambda i, k: (k, 0, 0)),
                  pl.BlockSpec((None, g, hid, LANES), lambda i, k: (k, 0, 0, 0))],
        out_specs=pl.BlockSpec((None, None, ncp, LANES), lambda i, k: (i, k, 0, 0)),
        out_shape=jax.ShapeDtypeStruct((b, 2, ncp, LANES), F32),
        compiler_params=_cparams(2),
        name="compress",
    )(hrows, pos, w1, w2p)


def _stack_heads(blk_fn, tq):
    lane = _lane_iota((tq, LANES))
    lo = [jnp.where(lane < HEAD_DIM, blk_fn(p), 0.0) for p in range(R_B)]
    hi = [jnp.where(lane >= HEAD_DIM, blk_fn(p), 0.0) for p in range(R_B)]
    return jnp.concatenate(lo + hi, axis=0)


def _unstack_gated(o, gsig, branch, tq):
    lane = _lane_iota((tq, LANES))
    out = []
    for p in range(R_B):
        c_lo = 3 * p + branch
        c_hi = 3 * (p + R_B) + branch
        blk = jnp.where(lane < HEAD_DIM, o[p * tq:(p + 1) * tq], o[(p + R_B) * tq:(p + R_B + 1) * tq])
        gate = jnp.where(lane < HEAD_DIM, gsig[:, c_lo:c_lo + 1], gsig[:, c_hi:c_hi + 1])
        out.append(blk * gate)
    return out


def _topk_penalty(score_t, n_top):
    nb = score_t.shape[0]
    jrow = _row_iota(score_t.shape).astype(F32)
    x = score_t
    sel = jnp.zeros(score_t.shape, F32)
    for _ in range(n_top):
        mx = jnp.max(x, axis=0, keepdims=True)
        first = jnp.min(jnp.where(x == mx, jrow, float(nb)), axis=0, keepdims=True)
        hit = jrow == first
        sel = jnp.where(hit, 1.0, sel)
        x = jnp.where(hit, -jnp.inf, x)
    return jnp.where(sel > 0.5, 0.0, NEG)


def _cmp_kernel(q_ref, kc_ref, vc_ref, tab_ref, ovl_ref, bg_ref, o_ref, pen_ref, *, n_top):
    tq = q_ref.shape[0]
    i = pl.program_id(1)
    q8 = _stack_heads(lambda p: q_ref[:, p * LANES:(p + 1) * LANES], tq).astype(BF16)
    s = _dot_nt(q8, kc_ref[...].astype(BF16)) + tab_ref[...]
    ok = s > MASKED_BELOW
    e = jnp.where(ok, jnp.exp(s - jnp.max(s, axis=-1, keepdims=True)), 0.0)
    p = (e / jnp.maximum(jnp.sum(e, axis=-1, keepdims=True), 1e-30)).astype(BF16)
    o = _dot(p, vc_ref[...].astype(BF16))
    gsig = jax.nn.sigmoid(bg_ref[...])
    for pi, blk in enumerate(_unstack_gated(o, gsig, 0, tq)):
        o_ref[:, pi * LANES:(pi + 1) * LANES] = blk
    imp8 = _dot(p, ovl_ref[...])
    lane = _lane_iota((tq, LANES))
    g0 = imp8[0:tq] + imp8[tq:2 * tq] + imp8[2 * tq:3 * tq] + imp8[3 * tq:4 * tq]
    g1 = imp8[4 * tq:5 * tq] + imp8[5 * tq:6 * tq] + imp8[6 * tq:7 * tq] + imp8[7 * tq:8 * tq]
    imp = jnp.where(lane < L_SEL, g0, g1)
    qpos = i * tq + _row_iota((tq, LANES))
    j = lane % L_SEL
    cur = qpos // L_SEL
    forced = (j == 0) | (j == cur) | (j == cur - 1)
    score = jnp.where(j * L_SEL <= qpos, imp + FORCE_BONUS * forced.astype(F32), NEG)
    st = score.T
    pen_t = jnp.concatenate([_topk_penalty(st[g * L_SEL:(g + 1) * L_SEL], n_top) for g in range(G_B)], axis=0)
    pen_ref[...] = pen_t


def _cmp_prompt(projr, kvc, tab, ovl, offs, *, n_top, tq=Q_BLOCK):
    b, t, _ = projr.shape
    ncp = kvc.shape[2]
    qb = offs['b_q'] // W_B
    gb = offs['b_g'] // LANES
    return pl.pallas_call(
        functools.partial(_cmp_kernel, n_top=n_top),
        grid=(b, t // tq),
        in_specs=[pl.BlockSpec((None, tq, W_B), lambda bi, i: (bi, i, qb)),
                  pl.BlockSpec((None, None, ncp, LANES), lambda bi, i: (bi, 0, 0, 0)),
                  pl.BlockSpec((None, None, ncp, LANES), lambda bi, i: (bi, 1, 0, 0)),
                  pl.BlockSpec((None, H_B * tq, ncp), lambda bi, i: (i, 0, 0)),
                  pl.BlockSpec((ncp, LANES), lambda bi, i: (0, 0)),
                  pl.BlockSpec((None, tq, LANES), lambda bi, i: (bi, i, gb))],
        out_specs=[pl.BlockSpec((None, tq, W_B), lambda bi, i: (bi, i, 0)),
                   pl.BlockSpec((None, LANES, tq), lambda bi, i: (bi, 0, i))],
        out_shape=[jax.ShapeDtypeStruct((b, t, W_B), F32), jax.ShapeDtypeStruct((b, LANES, t), F32)],
        compiler_params=_cparams(2),
        name="nsa_cmp",
    )(projr, kvc, kvc, tab, ovl, projr)


def _nsa_flash_kernel(q_ref, k_ref, v_ref, tab_ref, pen_ref, bg_ref, o_ref, qa_ref, m_ref, l_ref, acc_ref, *,
                      mode, branch):
    tq = q_ref.shape[0]
    tk = 2 * tq
    i = pl.program_id(1)
    jd = i // 2
    par = i % 2

    row = _row_iota((LANES, tq))
    for p in range(R_B):
        qt = q_ref[:, p * LANES:(p + 1) * LANES].T
        qa_ref[0:LANES, p * tq:(p + 1) * tq] = jnp.where(row < HEAD_DIM, qt, 0.0).astype(BF16)
        qa_ref[0:LANES, (p + R_B) * tq:(p + R_B + 1) * tq] = jnp.where(row >= HEAD_DIM, qt, 0.0).astype(BF16)
    if mode == 'sel':
        pen = pen_ref[...]
        lo = jnp.where(row < L_SEL, pen, 0.0).astype(BF16)
        hi = jnp.where(row >= L_SEL, pen, 0.0).astype(BF16)
        qa_ref[LANES:2 * LANES, :] = jnp.concatenate([lo] * R_B + [hi] * R_B, axis=1)
    m_ref[...] = jnp.full_like(m_ref, NEG)
    l_ref[...] = jnp.zeros_like(l_ref)
    acc_ref[...] = jnp.zeros_like(acc_ref)

    def step(j, bias, uniform_bias=False, tk=tk):
        k0 = pl.multiple_of(j * tk, tk)
        kt = k_ref[pl.ds(k0, tk), :].astype(BF16)
        if mode == 'sel':
            blk = (tk // L_SEL) * j + _row_iota((tk, LANES)) // L_SEL
            et = jnp.where(blk == _lane_iota((tk, LANES)) % L_SEL, 1.0, 0.0).astype(BF16)
            kt = jnp.concatenate([kt, et], axis=1)
        s = _dot(kt, qa_ref[...])
        m_old = m_ref[...]
        if uniform_bias:
            m_new = jnp.maximum(m_old, jnp.max(s, axis=0, keepdims=True) + bias)
            p = jnp.exp(s - (m_new - bias))
        else:
            s = s + bias
            m_new = jnp.maximum(m_old, jnp.max(s, axis=0, keepdims=True))
            p = jnp.exp(s - m_new)
        alpha = jnp.exp(m_old - m_new)
        l_ref[...] = alpha * l_ref[...] + jnp.sum(p, axis=0, keepdims=True)
        acc_ref[...] = alpha * acc_ref[...] + _dot(v_ref[:, pl.ds(k0, tk)].astype(BF16), p.astype(BF16))
        m_ref[...] = m_new

    step(jd, tab_ref[par])
    n_near = tab_ref.shape[0] // 2
    for dj in range(1, n_near):
        @pl.when(jd >= dj)
        def _(dj=dj):
            step(jd - dj, tab_ref[par + 2 * dj])

    if mode == 'sel':
        far_bias = tab_ref[3, 0:1, :]

        def far_body(j, c):
            step(j, far_bias, uniform_bias=True)
            return c

        n_far = jnp.maximum(jd - (n_near - 1), 0)

        def far_body2(j, c):
            step(j, far_bias, uniform_bias=True, tk=2 * tk)
            return c

        lax.fori_loop(0, n_far // 2, far_body2, 0)

        @pl.when(n_far % 2 == 1)
        def _():
            step(n_far - 1, far_bias, uniform_bias=True)

    ot = acc_ref[...] / jnp.maximum(l_ref[...], 1e-30)
    o = jnp.concatenate([ot[:, h * tq:(h + 1) * tq].T for h in range(H_B)], axis=0)
    gsig = jax.nn.sigmoid(bg_ref[...])
    for pi, blk in enumerate(_unstack_gated(o, gsig, branch, tq)):
        o_ref[:, pi * LANES:(pi + 1) * LANES] = blk


def _nsa_flash_prompt(projr, vt, tab, pen_t, offs, *, mode, tq=Q_BLOCK):
    b, t, _ = projr.shape
    assert t % (2 * tq) == 0 and WINDOW == 4 * tq and tq == LANES
    qb = offs['b_q'] // W_B
    gb = offs['b_g'] // LANES
    kb = offs['k_rows'] // LANES
    if mode == 'sel':
        vb, branch, qa_rows = 3, 1, 2 * LANES
    else:
        kb, vb, branch, qa_rows = kb + 1, 1, 2, LANES
    return pl.pallas_call(
        functools.partial(_nsa_flash_kernel, mode=mode, branch=branch),
        grid=(b, t // tq),
        in_specs=[pl.BlockSpec((None, tq, W_B), lambda bi, i: (bi, i, qb)),
                  pl.BlockSpec((None, t, KV_B), lambda bi, i: (bi, 0, kb)),
                  pl.BlockSpec((None, KV_B, t), lambda bi, i: (bi, vb, 0)),
                  pl.BlockSpec(tab.shape, lambda bi, i: (0, 0, 0)),
                  pl.BlockSpec((None, LANES, tq), lambda bi, i: (bi, 0, i)),
                  pl.BlockSpec((None, tq, LANES), lambda bi, i: (bi, i, gb))],
        out_specs=pl.BlockSpec((None, tq, W_B), lambda bi, i: (bi, i, 0)),
        out_shape=jax.ShapeDtypeStruct((b, t, W_B), F32),
        scratch_shapes=[pltpu.VMEM((qa_rows, H_B * tq), BF16),
                        pltpu.VMEM((1, H_B * tq), F32),
                        pltpu.VMEM((1, H_B * tq), F32),
                        pltpu.VMEM((KV_B, H_B * tq), F32)],
        compiler_params=_cparams(2),
        name="nsa_" + mode,
    )(projr, projr, vt, tab, pen_t, projr)


def _sb_kernel(q_ref, k_ref, v_ref, o_ref, qs_ref, carry_ref, acc_ref):
    tq = q_ref.shape[0]
    tk = tq
    i = pl.program_id(2)
    lane = _lane_iota((tq, LANES))
    q = q_ref[...]
    qs_ref[...] = jnp.concatenate([jnp.where(lane < HEAD_DIM, q, 0.0),
                                   jnp.where(lane >= HEAD_DIM, q, 0.0)], axis=0).astype(BF16)
    carry_ref[...] = jnp.zeros_like(carry_ref)
    acc_ref[...] = jnp.zeros_like(acc_ref)
    jj = _row_iota((2 * tk, tk))
    tri2 = jnp.where(jnp.where(jj >= tk, jj - tk, jj) > _lane_iota((2 * tk, tk)), 1.0, 0.0).astype(BF16)

    def step(j, diagonal):
        k0 = pl.multiple_of(j * tk, tk)
        z = _dot(qs_ref[...], k_ref[:, pl.ds(k0, tk)].astype(BF16))
        t = jnp.log(1.0 + jnp.exp(-jnp.abs(z)))
        log_1mb = -(jnp.maximum(z, 0.0) + t)
        log_b = z + log_1mb
        if diagonal:
            rr = _row_iota((2 * tq, tk))
            causal = _lane_iota((2 * tq, tk)) < jnp.where(rr >= tq, rr - tq, rr)
            log_1mb = jnp.where(causal, log_1mb, 0.0)
        hi, lo = _split_bf16(log_1mb)
        after = _dot(jnp.concatenate([hi, lo], axis=1), tri2) + carry_ref[...]
        a = jnp.exp(log_b + after)
        if diagonal:
            a = jnp.where(causal, a, 0.0)
        carry_ref[...] += jnp.sum(log_1mb, axis=-1, keepdims=True)
        acc_ref[...] += _dot_nt(a.astype(BF16), v_ref[:, pl.ds(k0, tk)].astype(BF16))

    step(i, True)

    def still_visible():
        return jnp.max(carry_ref[...]) > SB_UNDERFLOW

    def body(c):
        j, _ = c
        step(j, False)
        return j - 1, still_visible()

    lax.while_loop(lambda c: jnp.logical_and(c[0] >= 0, c[1]), body, (i - 1, still_visible()))
    acc = acc_ref[...]
    o_ref[...] = jnp.where(lane < HEAD_DIM, acc[0:tq], acc[tq:2 * tq])


def _sb_prompt(projr, sbt, offs, *, tq=2 * Q_BLOCK):
    b, t, _ = projr.shape
    n_pairs = W_C // LANES
    qb = offs['c_q'] // LANES
    return pl.pallas_call(
        _sb_kernel,
        grid=(b, n_pairs, t // tq),
        in_specs=[pl.BlockSpec((None, tq, LANES), lambda bi, p, i: (bi, i, qb + p)),
                  pl.BlockSpec((None, LANES, t), lambda bi, p, i: (bi, p, 0)),
                  pl.BlockSpec((None, LANES, t), lambda bi, p, i: (bi, n_pairs + p, 0))],
        out_specs=pl.BlockSpec((None, tq, LANES), lambda bi, p, i: (bi, i, p)),
        out_shape=jax.ShapeDtypeStruct((b, t, W_C), F32),
        scratch_shapes=[pltpu.VMEM((2 * tq, LANES), BF16),
                        pltpu.VMEM((2 * tq, 1), F32),
                        pltpu.VMEM((2 * tq, LANES), F32)],
        compiler_params=_cparams(3),
        name="sb",
    )(projr, sbt, sbt)


def _merge_kernel(*refs, decode):
    if decode:
        (x_ref, ada_ref, gpost_ref, conv_ref, s0_ref, s1_ref, gm0_ref, gm1_ref, gm2_ref, yb0_ref, yb1_ref,
         yb2_ref, yc_ref, cw_ref, wba_ref, wbb_ref, wbc_ref, wo_ref, o_ref, u_ref) = refs
    else:
        (x_ref, ada_ref, gpost_ref, conv_ref, halo_ref, gm0_ref, gm1_ref, gm2_ref, yb0_ref, yb1_ref, yb2_ref,
         yc_ref, cw_ref, wba_ref, wbb_ref, wbc_ref, wo_ref, o_ref, u_ref) = refs
    tm = x_ref.shape[0]
    u = conv_ref[:, 2 * W_A:3 * W_A] * conv_ref[:, 0:W_A]
    if decode:
        um2 = s0_ref[...]
        um1 = s1_ref[...]
        u_ref[...] = u
    else:
        i = pl.program_id(1)
        keep = (i > 0).astype(F32)
        hu = (halo_ref[:, 2 * W_A:3 * W_A] * halo_ref[:, 0:W_A]) * keep
        row = _row_iota((tm, W_A))
        um1 = jnp.where(row >= 1, pltpu.roll(u, 1, 0), hu[7:8])
        um2 = jnp.where(row >= 2, pltpu.roll(u, 2, 0), jnp.where(row == 1, hu[7:8], hu[6:7]))

        @pl.when(i == pl.num_programs(1) - 1)
        def _():
            u_ref[...] = u[tm - 8:tm]

    y_a = conv_ref[:, W_A:2 * W_A] * (cw_ref[0:1] * um2 + cw_ref[1:2] * um1 + cw_ref[2:3] * u)
    y_b = yb0_ref[...] + yb1_ref[...] + yb2_ref[...]
    merged = (jax.nn.sigmoid(gm0_ref[...]) * _dot(y_a.astype(BF16), wba_ref[...])
              + jax.nn.sigmoid(gm1_ref[...]) * _dot(y_b.astype(BF16), wbb_ref[...])
              + jax.nn.sigmoid(gm2_ref[...]) * _dot(yc_ref[...].astype(BF16), wbc_ref[...]))
    out = _dot(merged.astype(BF16), wo_ref[...])
    o_ref[...] = x_ref[...] + ada_ref[5] * (_rms(out) * gpost_ref[...])


def _merge(x, ada, gpost, projr, conv_state, yb, yc, cw, wba, wbb, wbc, wo, offs, *, tm, decode):
    bg, t, d = x.shape
    mrows = ada.shape[2]
    cb = offs['conv'] // (3 * W_A)
    gmb = offs['g_merge'] // d
    row_spec = lambda w, cblk: pl.BlockSpec((None, tm, w), lambda b, i: (b, i, cblk))
    full = lambda a: pl.BlockSpec(a.shape, lambda b, i: (0,) * a.ndim)
    if decode:
        ctx_specs = [row_spec(W_A, 0), row_spec(W_A, 0)]
        ctx = list(conv_state)
        u_shape, u_spec = (bg, t, W_A), row_spec(W_A, 0)
    else:
        r8 = tm // 8
        ctx_specs = [pl.BlockSpec((None, 8, 3 * W_A), lambda b, i: (b, jnp.maximum(i * r8 - 1, 0), cb))]
        ctx = [projr]
        u_shape, u_spec = (bg, 8, W_A), pl.BlockSpec((None, 8, W_A), lambda b, i: (b, 0, 0))
    out, u = pl.pallas_call(
        functools.partial(_merge_kernel, decode=decode),
        grid=(bg, t // tm),
        in_specs=[row_spec(d, 0),
                  pl.BlockSpec((None, 3 * N_SUB, mrows, d), lambda b, i: (b, 0, 0, 0)),
                  pl.BlockSpec((1, d), lambda b, i: (0, 0)),
                  row_spec(3 * W_A, cb)] + ctx_specs + [
                  row_spec(d, gmb), row_spec(d, gmb + 1), row_spec(d, gmb + 2),
                  row_spec(W_B, 0), row_spec(W_B, 0), row_spec(W_B, 0), row_spec(W_C, 0),
                  full(cw), full(wba), full(wbb), full(wbc), full(wo)],
        out_specs=[row_spec(d, 0), u_spec],
        out_shape=[jax.ShapeDtypeStruct(x.shape, F32), jax.ShapeDtypeStruct(u_shape, F32)],
        compiler_params=_cparams(2),
        name="merge",
    )(x, ada, gpost, projr, *ctx, projr, projr, projr, yb[0], yb[1], yb[2], yc, cw, wba, wbb, wbc, wo)
    return out, u


def _dec_cmp_kernel(pt_ref, *refs, n_pages_step, n_top, past):
    del pt_ref
    pages = refs[:n_pages_step]
    (q_ref, pos_ref, w1_ref, w2_ref, tab_ref, ovl_ref, bg_ref, o_ref, idx_ref, h_ref) = refs[n_pages_step:]
    c = pl.program_id(1)
    lane = _lane_iota((8, LANES))
    rpp = PAGE_SIZE // D_CMP
    ii = _row_iota((PAGE_SIZE, PAGE_SIZE))
    perm = jnp.where(_lane_iota((PAGE_SIZE, PAGE_SIZE)) == D_CMP * (ii % rpp) + ii // rpp, 1.0, 0.0).astype(BF16)
    for k in range(n_pages_step):
        row0 = pl.multiple_of((c * n_pages_step + k) * rpp, rpp)
        xt = pages[k][...].reshape(2 * KV_B, PAGE_SIZE).astype(BF16)
        rows = _dot_nt(perm, xt)
        for m in range(D_CMP // 2):
            ev = rows[2 * m * rpp:(2 * m + 1) * rpp]
            od = rows[(2 * m + 1) * rpp:(2 * m + 2) * rpp]
            for kind in range(2):
                e = ev[:, kind * LANES:(kind + 1) * LANES]
                o = od[:, kind * LANES:(kind + 1) * LANES]
                h_ref[kind, 0, pl.ds(row0, rpp), m * LANES:(m + 1) * LANES] = (
                    jnp.where(lane < HEAD_DIM, e, pltpu.roll(o, HEAD_DIM, 1)))
                h_ref[kind, 1, pl.ds(row0, rpp), m * LANES:(m + 1) * LANES] = (
                    jnp.where(lane < HEAD_DIM, pltpu.roll(e, HEAD_DIM, 1), o))

    @pl.when(c == pl.num_programs(1) - 1)
    def _():
        ncp = h_ref.shape[2]
        half = h_ref.shape[3]
        kv = []
        for kind in range(2):
            out = None
            for g in range(G_B):
                hg = h_ref[kind, g]
                a = _dot((hg + pos_ref[kind, 0]).astype(BF16), w1_ref[kind, 0:half, :])
                bm = _dot((hg + pos_ref[kind, 1]).astype(BF16), w1_ref[kind, half:2 * half, :])
                pre = a + pltpu.roll(bm, ncp - 1, 0)
                o = _dot(_silu(pre).astype(BF16), w2_ref[kind, g])
                out = o if out is None else out + o
            kv.append(out.astype(BF16))
        q8 = _stack_heads(lambda p: q_ref[:, p * LANES:(p + 1) * LANES], 1).astype(BF16)
        s = _dot_nt(q8, kv[0]) + tab_ref[...]
        ok = s > MASKED_BELOW
        e = jnp.where(ok, jnp.exp(s - jnp.max(s, axis=-1, keepdims=True)), 0.0)
        p = (e / jnp.maximum(jnp.sum(e, axis=-1, keepdims=True), 1e-30)).astype(BF16)
        o = _dot(p, kv[1])
        gsig = jax.nn.sigmoid(bg_ref[...])
        for pi, blk in enumerate(_unstack_gated(o, gsig, 0, 1)):
            o_ref[:, pi * LANES:(pi + 1) * LANES] = blk
        imp8 = _dot(p, ovl_ref[...])
        imp = jnp.concatenate([imp8[0:1] + imp8[1:2] + imp8[2:3] + imp8[3:4],
                               imp8[4:5] + imp8[5:6] + imp8[6:7] + imp8[7:8]], axis=0)
        nbl = imp.shape[1]
        j = _lane_iota(imp.shape)
        cur = past // L_SEL
        forced = (j == 0) | (j == cur) | (j == cur - 1)
        x = jnp.where(j * L_SEL <= past, imp + FORCE_BONUS * forced.astype(F32), NEG)
        jf = j.astype(F32)
        slot = _lane_iota((G_B, LANES))
        idx = jnp.zeros((G_B, LANES), F32)
        for it in range(n_top):
            mx = jnp.max(x, axis=-1, keepdims=True)
            first = jnp.min(jnp.where(x == mx, jf, float(nbl)), axis=-1, keepdims=True)
            idx = jnp.where(slot == it, first, idx)
            x = jnp.where(jf == first, -jnp.inf, x)
        idx_ref[...] = jnp.zeros(idx_ref.shape, jnp.int32)
        idx_ref[0:G_B, :] = idx.astype(jnp.int32)


def _dec_cmp(layer, page_table, cache_nsa_t, proj_s, pos, w1, w2p, tab, ovl, offs, *, n_top, n_pages_step=16):
    nseq, n_pages = page_table.shape
    n_pages_step = min(n_pages_step, n_pages)
    assert n_pages % n_pages_step == 0
    past = n_pages * PAGE_SIZE
    ncp = past // D_CMP
    half = D_CMP * HEAD_DIM
    qb = offs['b_q'] // W_B
    gb = offs['b_g'] // LANES
    page_specs = [pl.BlockSpec((None, None, 2, G_B, HEAD_DIM, PAGE_SIZE),
                               functools.partial(
                                   lambda b, c, pt, k: (layer, pt[b, c * n_pages_step + k], 0, 0, 0, 0), k=k))
                  for k in range(n_pages_step)]
    full = lambda a: pl.BlockSpec(a.shape, lambda b, c, pt: (0,) * a.ndim)
    grid_spec = pltpu.PrefetchScalarGridSpec(
        num_scalar_prefetch=1,
        grid=(nseq, n_pages // n_pages_step),
        in_specs=page_specs + [pl.BlockSpec((None, 1, W_B), lambda b, c, pt: (b, 0, qb)),
                               full(pos), full(w1), full(w2p), full(tab), full(ovl),
                               pl.BlockSpec((None, 1, LANES), lambda b, c, pt: (b, 0, gb))],
        out_specs=[pl.BlockSpec((None, 1, W_B), lambda b, c, pt: (b, 0, 0)),
                   pl.BlockSpec((None, 8, LANES), lambda b, c, pt: (b, 0, 0))],
        scratch_shapes=[pltpu.VMEM((2, G_B, ncp, half), F32)],
    )
    return pl.pallas_call(
        functools.partial(_dec_cmp_kernel, n_pages_step=n_pages_step, n_top=n_top, past=past),
        grid_spec=grid_spec,
        out_shape=[jax.ShapeDtypeStruct((nseq, 1, W_B), F32), jax.ShapeDtypeStruct((nseq, 8, LANES), jnp.int32)],
        compiler_params=_cparams(2),
        name="dec_cmp",
    )(page_table, *([cache_nsa_t] * n_pages_step), proj_s, pos, w1, w2p, tab, ovl, proj_s)


def _dec_selwin_kernel(sp_ref, *refs, n_top, past, n_pages):
    n_blk = G_B * n_top
    blocks = refs[:n_blk]
    (q_ref, ksn_ref, vsn_ref, wst_ref, kwn_ref, vwn_ref, tabw_ref, relt_ref, bg_ref,
     osel_ref, owin_ref, kc_ref, vc_ref) = refs[n_blk:]
    b = pl.program_id(0)
    n_past_blk = past // L_SEL
    halves = PAGE_SIZE // L_SEL
    q8f = _stack_heads(lambda p: q_ref[:, p * LANES:(p + 1) * LANES], 1)
    q8 = q8f.astype(BF16)
    q8r = q8.astype(F32)
    gsig = jax.nn.sigmoid(bg_ref[...])
    row1 = _row_iota((H_B, 1))

    for n in range(n_blk):
        kc_ref[:, n * PAGE_SIZE:(n + 1) * PAGE_SIZE] = blocks[n][0].reshape(KV_B, PAGE_SIZE).astype(BF16)
        vc_ref[:, n * PAGE_SIZE:(n + 1) * PAGE_SIZE] = blocks[n][1].reshape(KV_B, PAGE_SIZE).astype(BF16)
    nk = n_blk * PAGE_SIZE
    s = _dot(q8, kc_ref[...])
    lane = _lane_iota((H_B, nk))
    slot = lane // PAGE_SIZE
    within = lane % PAGE_SIZE
    blk_of = jnp.zeros((H_B, nk), jnp.int32)
    has_new = [jnp.int32(0)] * G_B
    for n in range(n_blk):
        bi = sp_ref[b, n_pages + n]
        blk_of = jnp.where(slot == n, bi, blk_of)
        has_new[n // n_top] = jnp.maximum(has_new[n // n_top], (bi == n_past_blk).astype(jnp.int32))
    kpos = (blk_of // halves) * PAGE_SIZE + within
    dist = past - kpos
    row = _row_iota((H_B, nk))
    ok = ((within // L_SEL == blk_of % halves) & (blk_of < n_past_blk)
          & (row // R_B == slot // n_top))
    bias = jnp.broadcast_to(relt_ref[:, 0:1], (H_B, nk))
    for k, thr in enumerate(_T5_THR):
        bias = jnp.where(dist >= thr, relt_ref[:, k + 1:k + 2], bias)
    s = jnp.where(ok, s + bias, NEG)
    new_ok = jnp.where(row1 < R_B, has_new[0], has_new[1]) > 0
    s_new = jnp.sum(q8r * ksn_ref[...].astype(BF16).astype(F32), axis=-1, keepdims=True)
    s_new = jnp.where(new_ok, s_new + relt_ref[:, 0:1], NEG)
    m = jnp.maximum(jnp.max(s, axis=-1, keepdims=True), s_new)
    e = jnp.where(ok, jnp.exp(s - m), 0.0)
    e_new = jnp.where(new_ok, jnp.exp(s_new - m), 0.0)
    den = jnp.maximum(jnp.sum(e, axis=-1, keepdims=True) + e_new, 1e-30)
    o = (_dot_nt(e.astype(BF16), vc_ref[...]) + e_new * vsn_ref[...].astype(BF16).astype(F32)) / den
    for pi, blk in enumerate(_unstack_gated(o, gsig, 1, 1)):
        osel_ref[:, pi * LANES:(pi + 1) * LANES] = blk

    n_ctx = wst_ref.shape[-1]
    sw = _dot(q8, wst_ref[0].reshape(KV_B, n_ctx).astype(BF16)) + tabw_ref[...]
    okw = sw > MASKED_BELOW
    sw_new = jnp.sum(q8r * kwn_ref[...].astype(BF16).astype(F32), axis=-1, keepdims=True) + relt_ref[:, 0:1]
    mw = jnp.maximum(jnp.max(sw, axis=-1, keepdims=True), sw_new)
    ew = jnp.where(okw, jnp.exp(sw - mw), 0.0)
    ew_new = jnp.exp(sw_new - mw)
    denw = jnp.maximum(jnp.sum(ew, axis=-1, keepdims=True) + ew_new, 1e-30)
    ow = (_dot_nt(ew.astype(BF16), wst_ref[1].reshape(KV_B, n_ctx).astype(BF16))
          + ew_new * vwn_ref[...].astype(BF16).astype(F32)) / denw
    for pi, blk in enumerate(_unstack_gated(ow, gsig, 2, 1)):
        owin_ref[:, pi * LANES:(pi + 1) * LANES] = blk


def _dec_selwin(layer, sel_table, n_pages, cache_nsa_t, proj_s, win_state_t, tabw, rel_t, offs, *, n_top):
    nseq = sel_table.shape[0]
    past = n_pages * PAGE_SIZE
    n_blk = G_B * n_top
    n_past_blk = past // L_SEL
    halves = PAGE_SIZE // L_SEL
    n_ctx = win_state_t.shape[-1]
    qb = offs['b_q'] // W_B
    gb = offs['b_g'] // LANES
    nb = offs['nsa_s'] // LANES
    wb = offs['win_s'] // LANES

    def blk_map(b, sp, n):
        bi = jnp.clip(sp[b, n_pages + n], 0, n_past_blk - 1)
        return (layer, sp[b, bi // halves], 1, 0, 0, 0)

    blk_specs = [pl.BlockSpec((None, None, 2, G_B, HEAD_DIM, PAGE_SIZE), functools.partial(blk_map, n=n))
                 for n in range(n_blk)]
    row = lambda cblk: pl.BlockSpec((None, 1, LANES), lambda b, sp: (b, 0, cblk))
    full = lambda a: pl.BlockSpec(a.shape, lambda b, sp: (0,) * a.ndim)
    grid_spec = pltpu.PrefetchScalarGridSpec(
        num_scalar_prefetch=1,
        grid=(nseq,),
        in_specs=blk_specs + [pl.BlockSpec((None, 1, W_B), lambda b, sp: (b, 0, qb)),
                              row(nb + 2), row(nb + 3),
                              pl.BlockSpec((None, None, 2, G_B, HEAD_DIM, n_ctx),
                                           lambda b, sp: (layer, b, 0, 0, 0, 0)),
                              row(wb), row(wb + 1), full(tabw), full(rel_t), row(gb)],
        out_specs=[pl.BlockSpec((None, 1, W_B), lambda b, sp: (b, 0, 0)),
                   pl.BlockSpec((None, 1, W_B), lambda b, sp: (b, 0, 0))],
        scratch_shapes=[pltpu.VMEM((KV_B, n_blk * PAGE_SIZE), BF16), pltpu.VMEM((KV_B, n_blk * PAGE_SIZE), BF16)],
    )
    return pl.pallas_call(
        functools.partial(_dec_selwin_kernel, n_top=n_top, past=past, n_pages=n_pages),
        grid_spec=grid_spec,
        out_shape=[jax.ShapeDtypeStruct((nseq, 1, W_B), F32)] * 2,
        compiler_params=_cparams(1),
        name="dec_selwin",
    )(sel_table, *([cache_nsa_t] * n_blk), proj_s, proj_s, proj_s, win_state_t, proj_s, proj_s,
      tabw, rel_t, proj_s)


def _dec_sb_kernel(sp_ref, *refs, n_pages_step, tail, n_pages):
    pages = refs[:n_pages_step]
    if tail:
        q_ref, acc_in_ref, carry_in_ref, o_ref, qcol_ref, carry_ref, acc_ref = refs[n_pages_step:]
    else:
        q_ref, acc_out_ref, carry_out_ref, qcol_ref, carry_ref, acc_ref = refs[n_pages_step:]
    b = pl.program_id(0)
    c = pl.program_id(1)
    ps = PAGE_SIZE

    @pl.when(c == 0)
    def _():
        qt = jnp.broadcast_to(q_ref[...], (LANES, W_C)).T
        qcol_ref[...] = qt.reshape(H_C, HEAD_DIM, LANES)
        if tail:
            carry_ref[...] = carry_in_ref[:, 0:1]
            acc_ref[...] = acc_in_ref[...]
        else:
            carry_ref[...] = jnp.zeros_like(carry_ref)
            acc_ref[...] = jnp.zeros_like(acc_ref)

    def sweep():
        tri = jnp.where(_row_iota((ps, ps)) > _lane_iota((ps, ps)), 1.0, 0.0).astype(BF16)
        hrow = _row_iota((H_C, ps))
        for k in range(n_pages_step):
            z = jnp.zeros((H_C, ps), F32)
            for h in range(H_C):
                zh = jnp.sum(qcol_ref[h] * pages[k][0, h], axis=0, keepdims=True)
                z = jnp.where(hrow == h, zh, z)
            t = jnp.log1p(jnp.exp(-jnp.abs(z)))
            log_1mb = -(jnp.maximum(z, 0.0) + t)
            log_b = jnp.minimum(z, 0.0) - t
            hi, lo = _split_bf16(log_1mb)
            after = _dot(hi, tri) + _dot(lo, tri) + carry_ref[...]
            a = jnp.exp(log_b + after)
            carry_ref[...] += jnp.sum(log_1mb, axis=-1, keepdims=True)
            for h in range(H_C):
                acc_ref[h] += a[h:h + 1, :] * pages[k][1, h]

    if tail:
        pl.when(sp_ref[b, n_pages] > 0)(sweep)

        @pl.when(c == pl.num_programs(1) - 1)
        def _():
            acc = acc_ref[...].reshape(W_C, LANES)
            ones = jnp.ones((8, LANES), BF16)
            hi = acc.astype(BF16)
            r1 = acc - hi.astype(F32)
            mid = r1.astype(BF16)
            lo = (r1 - mid.astype(F32)).astype(BF16)
            tot = _dot_nt(ones, hi) + _dot_nt(ones, mid) + _dot_nt(ones, lo)
            o_ref[...] = tot[0:1]
    else:
        sweep()

        @pl.when(c == pl.num_programs(1) - 1)
        def _():
            acc_out_ref[...] = acc_ref[...]
            carry_out_ref[...] = jnp.broadcast_to(carry_ref[...], (H_C, LANES))


def _dec_sb(layer, page_table, cache_sb_t, proj_s, offs):
    nseq, n_pages = page_table.shape
    assert n_pages >= 2
    step = min(8, n_pages // 2)
    assert n_pages % step == 0
    n_head = step
    qb = offs['c_q'] // W_C
    page_block = (None, None, 2, H_C, HEAD_DIM, PAGE_SIZE)
    scratch = [pltpu.VMEM((H_C, HEAD_DIM, LANES), F32), pltpu.VMEM((H_C, 1), F32),
               pltpu.VMEM((H_C, HEAD_DIM, LANES), F32)]
    q_spec = pl.BlockSpec((None, 1, W_C), lambda b, c, sp: (b, 0, qb))
    acc_spec = pl.BlockSpec((None, H_C, HEAD_DIM, LANES), lambda b, c, sp: (b, 0, 0, 0))
    carry_spec = pl.BlockSpec((None, H_C, LANES), lambda b, c, sp: (b, 0, 0))

    head_specs = [pl.BlockSpec(page_block, functools.partial(
        lambda b, c, sp, k: (layer, sp[b, n_pages - 1 - (c * step + k)], 0, 0, 0, 0), k=k)) for k in range(step)]
    acc, carry = pl.pallas_call(
        functools.partial(_dec_sb_kernel, n_pages_step=step, tail=False, n_pages=n_pages),
        grid_spec=pltpu.PrefetchScalarGridSpec(
            num_scalar_prefetch=1, grid=(nseq, n_head // step),
            in_specs=head_specs + [q_spec], out_specs=[acc_spec, carry_spec], scratch_shapes=scratch),
        out_shape=[jax.ShapeDtypeStruct((nseq, H_C, HEAD_DIM, LANES), F32),
                   jax.ShapeDtypeStruct((nseq, H_C, LANES), F32)],
        compiler_params=_cparams(2),
        name="dec_sb_head",
    )(page_table, *([cache_sb_t] * step), proj_s)

    flag = (jnp.max(carry[:, :, 0], axis=1) > SB_UNDERFLOW).astype(jnp.int32)
    table = jnp.concatenate([page_table, flag[:, None]], axis=1)
    table = jnp.pad(table, ((0, 0), (0, -table.shape[1] % LANES)))

    def tail_map(b, c, sp, k):
        page = sp[b, n_pages - 1 - (n_head + c * step + k)]
        return (layer, jnp.where(sp[b, n_pages] > 0, page, 0), 0, 0, 0, 0)

    tail_specs = [pl.BlockSpec(page_block, functools.partial(tail_map, k=k)) for k in range(step)]
    return pl.pallas_call(
        functools.partial(_dec_sb_kernel, n_pages_step=step, tail=True, n_pages=n_pages),
        grid_spec=pltpu.PrefetchScalarGridSpec(
            num_scalar_prefetch=1, grid=(nseq, (n_pages - n_head) // step),
            in_specs=tail_specs + [q_spec, acc_spec, carry_spec],
            out_specs=pl.BlockSpec((None, 1, W_C), lambda b, c, sp: (b, 0, 0)), scratch_shapes=scratch),
        out_shape=jax.ShapeDtypeStruct((nseq, 1, W_C), F32),
        compiler_params=_cparams(2),
        name="dec_sb_tail",
    )(table, *([cache_sb_t] * step), proj_s, acc, carry)


def _overlap(ncp, n_lanes, reps):
    c0 = np.arange(ncp)[:, None] * D_CMP
    s0 = np.arange(n_lanes)[None, :] * L_SEL
    ov = ((c0 < s0 + L_SEL) & (c0 + L_CMP > s0)).astype(np.float32)
    return jnp.asarray(np.tile(ov, (1, reps)), BF16)


def _flat16_t(kt):
    b, _, t = kt.shape
    r = kt.reshape(b, G_B, HEAD_DIM, t // D_CMP, D_CMP).transpose(0, 1, 3, 4, 2)
    return r.reshape(b, G_B, t // D_CMP, D_CMP * HEAD_DIM)


def kernel(x_prompt, x_sample, cache_sb, cache_nsa, state_win, state_conv, page_table, c_prompt, c_sample,
           rel_bias, norm_pre, norm_post, w_ada, b_ada, w_ffn_gate, w_ffn_up, w_ffn_down, w_in, conv_w,
           cmp_pos, w_cmp1, w_cmp2, w_branch, w_out):
    batch, seq, d = x_prompt.shape
    nseq = x_sample.shape[0]
    depth = w_in.shape[0]
    n_pages = page_table.shape[1]
    past = n_pages * PAGE_SIZE
    n_ctx = state_win.shape[2]
    assert x_sample.shape[1] == 1 and seq % Q_BLOCK == 0 and seq // L_SEL <= L_SEL
    assert cache_sb.shape[2] == PAGE_SIZE and n_ctx == WINDOW
    offs, row_part, t_part, n_row, n_all = _layout(d)
    half = D_CMP * HEAD_DIM

    q_names = ('b_q', 'c_q')
    w_row = _gather_cols(w_in, row_part, q_names)
    w_kv = _gather_cols(w_in, t_part, q_names)
    tn = n_row // 2
    tn_s = 2048
    assert tn % LANES == 0
    n_s = -(-n_all // tn_s) * tn_s
    w_r = w_row.astype(BF16)
    w_t = w_kv.transpose(0, 2, 1).astype(BF16)
    w_s = jnp.concatenate([w_row, w_kv, jnp.zeros((depth, d, n_s - n_all), F32)], axis=-1).astype(BF16)
    wg = w_ffn_gate.astype(BF16)
    wu = w_ffn_up.astype(BF16)
    wd = w_ffn_down.astype(BF16)
    wba = w_branch[:, 0:W_A].astype(BF16)
    hb = lambda h: w_branch[:, W_A + h * HEAD_DIM:W_A + (h + 1) * HEAD_DIM]
    wbb = jnp.concatenate([x for p in range(R_B) for x in (hb(p), hb(p + R_B))], axis=1).astype(BF16)
    wbc = w_branch[:, W_A + W_B:].astype(BF16)
    wo = w_out.astype(BF16)
    w1 = w_cmp1.astype(BF16)
    z = jnp.zeros_like(w_cmp2)
    w2p = jnp.stack([jnp.concatenate([w_cmp2, z], axis=-1), jnp.concatenate([z, w_cmp2], axis=-1)],
                    axis=2).astype(BF16)
    pos = cmp_pos.reshape(depth, 2, 2, 1, half)

    rows_c = -(-(batch + nseq) // 16) * 16
    c_all = jnp.concatenate([c_prompt, c_sample, jnp.zeros((rows_c - batch - nseq, d), F32)], axis=0)
    ada_all = _ada_all(c_all, w_ada, b_ada)
    ada_p = ada_all[:, :batch].reshape(depth, batch, 3 * N_SUB, 1, d)
    ada_s = ada_all[:, batch:batch + nseq].reshape(depth, nseq, 3 * N_SUB, d).transpose(0, 2, 1, 3)[:, None]

    tq = Q_BLOCK
    ncp_p = seq // D_CMP
    tab_sel = _bias_table(rel_bias, 4, 2 * tq, tq, base0=0, tile_step=tq, row_step=-1, col_step=-1,
                          heads_on_lanes=True)
    tab_win = _bias_table(rel_bias, 6, 2 * tq, tq, base0=0, tile_step=tq, row_step=-1, col_step=-1,
                          max_dist=WINDOW, heads_on_lanes=True)
    tab_cmp = _bias_table(rel_bias, seq // tq, tq, ncp_p, base0=-(L_CMP - 1), tile_step=tq, row_step=1,
                          col_step=D_CMP)
    ncp_s = past // D_CMP
    tab_cmp_s = _bias_table(rel_bias, 1, 1, ncp_s, base0=past - (L_CMP - 1), tile_step=0, row_step=0,
                            col_step=D_CMP)[0]
    tab_win_s = _bias_table(rel_bias, 1, 1, n_ctx, base0=n_ctx, tile_step=0, row_step=0, col_step=1,
                            max_dist=WINDOW)[0]
    rel_t = rel_bias.T
    ovl_p = _overlap(ncp_p, L_SEL, G_B)
    nsl_s = -(-(past // L_SEL + 1) // LANES) * LANES
    ovl_s = _overlap(ncp_s, nsl_s, 1)
    n_top_p = min(N_SEL, -(-seq // L_SEL))
    n_top_s = min(N_SEL, -(-(past + 1) // L_SEL))

    cache_sb_t = cache_sb.transpose(0, 1, 3, 4, 5, 2)
    cache_nsa_t = cache_nsa.transpose(0, 1, 3, 4, 5, 2)
    win_state_t = state_win.transpose(0, 1, 3, 4, 5, 2)

    tm_p = 512 if seq % 512 == 0 else Q_BLOCK
    tf = w_ffn_gate.shape[-1] // 2

    y_p = x_prompt
    y_s = x_sample.reshape(1, nseq, d)
    outs = {k: [] for k in ('sb_p', 'sb_s', 'nsa_p', 'nsa_s', 'win_p', 'win_s', 'conv_p', 'conv_s')}
    for l in range(depth):
        gpre = [norm_pre[l, i][None] for i in range(N_SUB)]
        gpost = [norm_post[l, i][None] for i in range(N_SUB)]
        y_p = _ffn(y_p, ada_p[l], gpre[0], gpost[0], wg[l, 0], wu[l, 0], wd[l, 0], sub=0, tm=tm_p, tf=tf)
        projr = _proj(y_p, ada_p[l], gpre[1], w_r[l], tm=tm_p, tn=tn)
        sbt, nsat, wint = _proj_t(y_p, ada_p[l], gpre[1], w_t[l], tm=tm_p)
        hrows = jnp.stack([_flat16_t(nsat[:, 0:KV_B]), _flat16_t(nsat[:, KV_B:2 * KV_B])], axis=1)
        kvc = _compress_prompt(hrows, pos[l], w1[l], w2p[l])
        y_cmp, pen = _cmp_prompt(projr, kvc, tab_cmp, ovl_p, offs, n_top=n_top_p)
        y_sel = _nsa_flash_prompt(projr, nsat, tab_sel, pen, offs, mode='sel')
        y_win = _nsa_flash_prompt(projr, wint, tab_win, pen, offs, mode='win')
        y_c = _sb_prompt(projr, sbt, offs)
        y_p, u_tail = _merge(y_p, ada_p[l], gpost[1], projr, None, (y_cmp, y_sel, y_win), y_c, conv_w[l],
                             wba[l], wbb[l], wbc[l], wo[l], offs, tm=tm_p, decode=False)
        y_p = _ffn(y_p, ada_p[l], gpre[2], gpost[2], wg[l, 1], wu[l, 1], wd[l, 1], sub=2, tm=tm_p, tf=tf)
        n_keep = min(WINDOW, seq)
        to_state = lambda a, k, g: a.reshape(batch, k, g, HEAD_DIM, a.shape[-1]).transpose(0, 4, 1, 2, 3)
        outs['sb_p'].append(to_state(sbt, 2, H_C))
        outs['nsa_p'].append(to_state(nsat, 4, G_B))
        outs['win_p'].append(to_state(wint[:, :, seq - n_keep:], 2, G_B))
        outs['conv_p'].append(u_tail[:, 8 - (CONV_W - 1):])
        y_s = _ffn(y_s, ada_s[l], gpre[0], gpost[0], wg[l, 0], wu[l, 0], wd[l, 0], sub=0, tm=nseq, tf=tf)
        proj_s = _proj(y_s, ada_s[l], gpre[1], w_s[l], tm=nseq, tn=tn_s)
        proj_r = proj_s.reshape(nseq, 1, n_s)
        s_cmp, top_idx = _dec_cmp(l, page_table, cache_nsa_t, proj_r, pos[l], w1[l], w2p[l], tab_cmp_s, ovl_s,
                                  offs, n_top=n_top_s)
        sel_table = jnp.concatenate([page_table, top_idx[:, 0, 0:n_top_s], top_idx[:, 1, 0:n_top_s]], axis=1)
        sel_table = jnp.pad(sel_table, ((0, 0), (0, -sel_table.shape[1] % LANES)))
        s_sel, s_win = _dec_selwin(l, sel_table, n_pages, cache_nsa_t, proj_r, win_state_t, tab_win_s,
                                   rel_t, offs, n_top=n_top_s)
        s_c = _dec_sb(l, page_table, cache_sb_t, proj_r, offs)
        to_rows = lambda a: a.reshape(1, nseq, a.shape[-1])
        ctx = (state_conv[l, :, 0][None], state_conv[l, :, 1][None])
        y_s, u_s = _merge(y_s, ada_s[l], gpost[1], proj_s, ctx, (to_rows(s_cmp), to_rows(s_sel), to_rows(s_win)),
                          to_rows(s_c), conv_w[l], wba[l], wbb[l], wbc[l], wo[l], offs, tm=nseq, decode=True)
        y_s = _ffn(y_s, ada_s[l], gpre[2], gpost[2], wg[l, 1], wu[l, 1], wd[l, 1], sub=2, tm=nseq, tf=tf)
        ps = proj_s[0]
        o_s, o_n, o_w = offs['sb_s'], offs['nsa_s'], offs['win_s']
        outs['sb_s'].append(ps[:, o_s:o_s + KT_SB].reshape(nseq, 1, 2, H_C, HEAD_DIM))
        outs['nsa_s'].append(ps[:, o_n:o_n + KT_NSA].reshape(nseq, 1, 4, G_B, HEAD_DIM))
        win_new = ps[:, o_w:o_w + KT_WIN].reshape(nseq, 1, 2, G_B, HEAD_DIM)
        outs['win_s'].append(jnp.concatenate([state_win[l][:, 1:], win_new], axis=1))
        outs['conv_s'].append(jnp.stack([state_conv[l, :, 1], u_s[0]], axis=1))
    st = {k: jnp.stack(v) for k, v in outs.items()}
    return (y_p, y_s.reshape(nseq, 1, d), st['sb_p'], st['sb_s'], st['nsa_p'], st['nsa_s'],
            st['win_p'], st['win_s'], st['conv_p'], st['conv_s'])
```
